```python
import math
import numpy as np
import jax
import jax.numpy as jnp
from jax import lax

D_MODEL = 2048
BATCH = 4
SEQ = 2048
DEPTH = 4

CTX_LEN = 256
GRID_W = 64
Q_BLOCK = 128
EPS = 1e-6
ROPE_BASE = 10000.0

GQA_HEADS = 4
GQA_KV_HEADS = 2
GQA_HEAD_DIM = 128
MLA_HEADS = 4
MLA_Q_LORA = 384
MLA_KV_LORA = 256
MLA_NOPE = 64
MLA_ROPE = 32
MLA_V = 128
HY_WIDTH = 512
HY_ORDER = 2
HY_SHORT = 3
HY_BANDS = 16
HY_EMB = 2 * HY_BANDS + 1
HY_FILTER_HIDDEN = 64
HY_FAST_RATE = 15.35
HY_SLOW_RATE = 3.07
CV_WIDTH = 512
CV_KERNEL = 31
N_BRANCH = 4
BRANCH_WIDTH = 512
N_EXPERTS = 16
EXPERT_FF = 1024
CAPACITY_FACTOR = 2

GQA_KV_COLS = 2 * GQA_KV_HEADS * GQA_HEAD_DIM
MLA_KV_COLS = MLA_KV_LORA + MLA_ROPE
KV_COLS = GQA_KV_COLS + MLA_KV_COLS
Q_COLS = GQA_HEADS * GQA_HEAD_DIM + MLA_Q_LORA
HY_COLS = (HY_ORDER + 1) * HY_WIDTH
CV_COLS = 2 * CV_WIDTH
GATE_COLS = N_BRANCH * D_MODEL
IN_COLS = KV_COLS + Q_COLS + HY_COLS + CV_COLS + GATE_COLS

kernel_name = 'hybrid_hyena_gqa_mla_conformer_ec_moe_diffusion'


def rmsnorm(x, g):
    xf = x.astype(jnp.float32)
    y = xf * lax.rsqrt(jnp.mean(xf * xf, axis=-1, keepdims=True) + EPS)
    return (y * g.astype(jnp.float32)).astype(x.dtype)


def layernorm(x, g, b):
    xf = x.astype(jnp.float32)
    mu = jnp.mean(xf, axis=-1, keepdims=True)
    xc = xf - mu
    var = jnp.mean(xc * xc, axis=-1, keepdims=True)
    return (xc * lax.rsqrt(var + EPS) * g.astype(jnp.float32) + b.astype(jnp.float32)).astype(x.dtype)


def modulate(h, shift, scale):
    return h * (1.0 + scale) + shift


def split_cols(p, widths):
    return jnp.split(p, [int(o) for o in np.cumsum(widths)[:-1]], axis=-1)


def dwconv(x, w, b):
    C = x.shape[-1]
    y = lax.conv_general_dilated(x, w[:, None, :].astype(x.dtype), window_strides=(1,), padding='SAME',
                                 dimension_numbers=('NWC', 'WIO', 'NWC'), feature_group_count=C)
    return y + b.astype(x.dtype)


def _rotate(part, pos):
    m = part.shape[-1]
    inv = ROPE_BASE ** (-jnp.arange(0, m, 2, dtype=jnp.float32) / m)
    ang = pos.astype(jnp.float32)[:, None] * inv[None, :]
    cos = jnp.cos(ang)[None, :, None, :]
    sin = jnp.sin(ang)[None, :, None, :]
    pf = part.astype(jnp.float32)
    a, b = pf[..., : m // 2], pf[..., m // 2:]
    return jnp.concatenate([a * cos - b * sin, b * cos + a * sin], axis=-1).astype(part.dtype)


def axial_rope(x, pos):
    rows, cols = pos
    half = x.shape[-1] // 2
    return jnp.concatenate([_rotate(x[..., :half], rows), _rotate(x[..., half:], cols)], axis=-1)


def attend(q, k, v):
    B, Sq, Hq, dk = q.shape
    Hk, dv = k.shape[2], v.shape[-1]
    R = Hq // Hk
    nb = Sq // Q_BLOCK
    scale = dk ** -0.5
    qb = q.reshape(B, nb, Q_BLOCK, Hk, R, dk).transpose(1, 0, 2, 3, 4, 5)

    def block(qi):
        s = jnp.einsum('bqgrd,bkgd->bgrqk', qi, k, preferred_element_type=jnp.float32) * scale
        p = jax.nn.softmax(s, axis=-1).astype(v.dtype)
        return jnp.einsum('bgrqk,bkgd->bqgrd', p, v)

    o = lax.map(block, qb)
    return o.transpose(1, 0, 2, 3, 4, 5).reshape(B, Sq, Hq, dv)


def attn_keys(pkv, lp, pos):
    B, L, _ = pkv.shape
    k, v, kv_a, k_pe = split_cols(pkv, [GQA_KV_HEADS * GQA_HEAD_DIM, GQA_KV_HEADS * GQA_HEAD_DIM,
                                        MLA_KV_LORA, MLA_ROPE])
    k = rmsnorm(k.reshape(B, L, GQA_KV_HEADS, GQA_HEAD_DIM), lp['gqa_k_gain'])
    v = v.reshape(B, L, GQA_KV_HEADS, GQA_HEAD_DIM)
    kv = (rmsnorm(kv_a, lp['mla_kv_a_gain']) @ lp['mla_kv_b']).reshape(B, L, MLA_HEADS, MLA_NOPE + MLA_V)
    k_nope, v_mla = kv[..., :MLA_NOPE], kv[..., MLA_NOPE:]
    k_pe = k_pe.reshape(B, L, 1, MLA_ROPE)
    if pos is not None:
        k = axial_rope(k, pos)
        k_pe = axial_rope(k_pe, pos)
    k_mla = jnp.concatenate([k_nope, jnp.broadcast_to(k_pe, (B, L, MLA_HEADS, MLA_ROPE))], axis=-1)
    return k, v, k_mla, v_mla


def hyena_filters(L, lp):
    pos = jnp.arange(L, dtype=jnp.float32)
    t01 = pos / (L - 1)
    bands = jnp.linspace(1e-4, HY_BANDS - 1, HY_BANDS, dtype=jnp.float32)
    ang = (2.0 * math.pi / L) * pos[:, None] * bands[None, :]
    feats = jnp.concatenate([t01[:, None], jnp.cos(ang), -jnp.sin(ang)], axis=-1)
    freq = lp['hf_freq'].astype(jnp.float32)
    h = jnp.sin(freq * (feats @ lp['hf_w1'].astype(jnp.float32) + lp['hf_b1'].astype(jnp.float32)))
    h = jnp.sin(freq * (h @ lp['hf_w2'].astype(jnp.float32) + lp['hf_b2'].astype(jnp.float32)))
    h = h @ lp['hf_w3'].astype(jnp.float32)
    h = h * jnp.exp(-t01[:, None] * jnp.exp(lp['hf_log_rate'].astype(jnp.float32))[None, :])
    h = h.reshape(L, HY_ORDER, 2, HY_WIDTH)
    fwd, bwd = h[:, :, 0], h[:, :, 1]
    zero = jnp.zeros((1, HY_ORDER, HY_WIDTH), jnp.float32)
    return jnp.concatenate([fwd, zero, bwd[:0:-1]], axis=0)


def hyena_mixer(p, lp):
    B, L, _ = p.shape
    u = dwconv(p, lp['hy_conv_w'], lp['hy_conv_b'])
    v, *gates = jnp.split(u, HY_ORDER + 1, axis=-1)
    kf = jnp.fft.rfft(hyena_filters(L, lp), axis=0)
    z = v
    for n, gate in enumerate(gates):
        zf = z.astype(jnp.float32)
        conv = jnp.fft.irfft(jnp.fft.rfft(zf, n=2 * L, axis=1) * kf[None, :, n], n=2 * L, axis=1)[:, :L]
        z = gate * (conv + lp['hy_bias'][n].astype(jnp.float32) * zf).astype(p.dtype)
    return z


def conformer_conv(p, lp):
    a, b = jnp.split(p, 2, axis=-1)
    u = dwconv(a * jax.nn.sigmoid(b), lp['cv_w'], lp['cv_b'])
    return jax.nn.silu(layernorm(u, lp['cv_ln_g'], lp['cv_ln_b']))


def token_mixers(p, keys, pos, lp):
    B, L, _ = p.shape
    pq, phy, pcv, pg = split_cols(p, [Q_COLS, HY_COLS, CV_COLS, GATE_COLS])
    q, q_a = split_cols(pq, [GQA_HEADS * GQA_HEAD_DIM, MLA_Q_LORA])
    q = rmsnorm(q.reshape(B, L, GQA_HEADS, GQA_HEAD_DIM), lp['gqa_q_gain'])
    qm = (rmsnorm(q_a, lp['mla_q_a_gain']) @ lp['mla_q_b']).reshape(B, L, MLA_HEADS, MLA_NOPE + MLA_ROPE)
    q_nope, q_pe = qm[..., :MLA_NOPE], qm[..., MLA_NOPE:]
    if pos is not None:
        q = axial_rope(q, pos)
        q_pe = axial_rope(q_pe, pos)
    qm = jnp.concatenate([q_nope, q_pe], axis=-1)
    k, v, k_mla, v_mla = keys
    o_gqa = attend(q, k, v).reshape(B, L, BRANCH_WIDTH)
    o_mla = attend(qm, k_mla, v_mla).reshape(B, L, BRANCH_WIDTH)
    o_hy = hyena_mixer(phy, lp)
    o_cv = conformer_conv(pcv, lp)
    branches = jnp.stack([o_hy, o_gqa, o_mla, o_cv], axis=2)
    up = jnp.einsum('bsnw,nwd->bsnd', branches, lp['w_br'])
    gates = jax.nn.sigmoid(pg.reshape(B, L, N_BRANCH, -1))
    return jnp.einsum('bsnd,bsnd->bsd', gates, up) @ lp['w_out']


def expert_choice(h, lp):
    B, n, _ = h.shape
    cap = CAPACITY_FACTOR * n // N_EXPERTS
    aff = jax.nn.softmax(jnp.einsum('bnd,de->bne', h, lp['w_router'], preferred_element_type=jnp.float32), axis=-1)
    g, idx = lax.top_k(aff.transpose(0, 2, 1), cap)
    bidx = jnp.arange(B)[:, None, None]
    xg = h[bidx, idx]
    hid = jax.nn.silu(jnp.einsum('becd,edf->becf', xg, lp['w1'])) * jnp.einsum('becd,edf->becf', xg, lp['w3'])
    y = jnp.einsum('becf,efd->becd', hid, lp['w2']) * g[..., None].astype(h.dtype)
    return jnp.zeros_like(h).at[bidx, idx].add(y)


def setup_inputs(seed: int = 0) -> dict:
    key = jax.random.key(seed)
    ks = iter(jax.random.split(key, 40))
    D = D_MODEL

    def nrm(shape, scale):
        return jax.random.normal(next(ks), shape, jnp.float32) * scale

    def gain(shape):
        return 1.0 + nrm(shape, 0.02)

    rate0 = jnp.log(jnp.tile(jnp.linspace(HY_SLOW_RATE, HY_FAST_RATE, HY_WIDTH, dtype=jnp.float32), HY_ORDER * 2))
    return {
        'x': nrm((BATCH, SEQ, D), 1.0),
        'c': nrm((BATCH, D), 1.0),
        'ctx': nrm((BATCH, CTX_LEN, D), 1.0),
        'c_ctx': nrm((D,), 1.0),
        'w_mod': nrm((DEPTH, D, 6 * D), 0.5 * D ** -0.5),
        'b_mod': nrm((DEPTH, 6 * D), 0.01),
        'g_mix': gain((DEPTH, D)),
        'w_in': nrm((DEPTH, D, IN_COLS), D ** -0.5),
        'gqa_q_gain': gain((DEPTH, GQA_HEAD_DIM)),
        'gqa_k_gain': gain((DEPTH, GQA_HEAD_DIM)),
        'mla_q_a_gain': gain((DEPTH, MLA_Q_LORA)),
        'mla_q_b': nrm((DEPTH, MLA_Q_LORA, MLA_HEADS * (MLA_NOPE + MLA_ROPE)), MLA_Q_LORA ** -0.5),
        'mla_kv_a_gain': gain((DEPTH, MLA_KV_LORA)),
        'mla_kv_b': nrm((DEPTH, MLA_KV_LORA, MLA_HEADS * (MLA_NOPE + MLA_V)), MLA_KV_LORA ** -0.5),
        'hy_conv_w': nrm((DEPTH, HY_SHORT, HY_COLS), HY_SHORT ** -0.5),
        'hy_conv_b': nrm((DEPTH, HY_COLS), 0.01),
        'hf_w1': nrm((DEPTH, HY_EMB, HY_FILTER_HIDDEN), HY_EMB ** -0.5),
        'hf_b1': nrm((DEPTH, HY_FILTER_HIDDEN), 0.1),
        'hf_w2': nrm((DEPTH, HY_FILTER_HIDDEN, HY_FILTER_HIDDEN), HY_FILTER_HIDDEN ** -0.5),
        'hf_b2': nrm((DEPTH, HY_FILTER_HIDDEN), 0.1),
        'hf_w3': nrm((DEPTH, HY_FILTER_HIDDEN, HY_ORDER * 2 * HY_WIDTH), 0.05 * HY_FILTER_HIDDEN ** -0.5),
        'hf_freq': 1.0 + nrm((DEPTH, HY_FILTER_HIDDEN), 0.1),
        'hf_log_rate': rate0[None, :] + nrm((DEPTH, HY_ORDER * 2 * HY_WIDTH), 0.05),
        'hy_bias': nrm((DEPTH, HY_ORDER, HY_WIDTH), 0.1),
        'cv_w': nrm((DEPTH, CV_KERNEL, CV_WIDTH), CV_KERNEL ** -0.5),
        'cv_b': nrm((DEPTH, CV_WIDTH), 0.01),
        'cv_ln_g': gain((DEPTH, CV_WIDTH)),
        'cv_ln_b': nrm((DEPTH, CV_WIDTH), 0.01),
        'w_br': nrm((DEPTH, N_BRANCH, BRANCH_WIDTH, D), BRANCH_WIDTH ** -0.5),
        'w_out': nrm((DEPTH, D, D), D ** -0.5),
        'g_ffn': gain((DEPTH, D)),
        'w_router': nrm((DEPTH, D, N_EXPERTS), D ** -0.5),
        'w1': nrm((DEPTH, N_EXPERTS, D, EXPERT_FF), D ** -0.5),
        'w3': nrm((DEPTH, N_EXPERTS, D, EXPERT_FF), D ** -0.5),
        'w2': nrm((DEPTH, N_EXPERTS, EXPERT_FF, D), EXPERT_FF ** -0.5),
        'g_final': gain((D,)),
    }


def reference(x, c, ctx, c_ctx, w_mod, b_mod, g_mix, w_in, gqa_q_gain, gqa_k_gain, mla_q_a_gain, mla_q_b,
              mla_kv_a_gain, mla_kv_b, hy_conv_w, hy_conv_b, hf_w1, hf_b1, hf_w2, hf_b2, hf_w3, hf_freq,
              hf_log_rate, hy_bias, cv_w, cv_b, cv_ln_g, cv_ln_b, w_br, w_out, g_ffn, w_router, w1, w3, w2,
              g_final):
    S = x.shape[1]
    ROWS = S // GRID_W
    rows = jnp.repeat(jnp.arange(ROWS, dtype=jnp.int32), GRID_W)
    cols = jnp.tile(jnp.arange(GRID_W, dtype=jnp.int32), ROWS)
    pos = (rows, cols)
    xc = ctx
    for i in range(DEPTH):
        last = i == DEPTH - 1
        lp = {
            'gqa_q_gain': gqa_q_gain[i], 'gqa_k_gain': gqa_k_gain[i],
            'mla_q_a_gain': mla_q_a_gain[i], 'mla_q_b': mla_q_b[i],
            'mla_kv_a_gain': mla_kv_a_gain[i], 'mla_kv_b': mla_kv_b[i],
            'hy_conv_w': hy_conv_w[i], 'hy_conv_b': hy_conv_b[i],
            'hf_w1': hf_w1[i], 'hf_b1': hf_b1[i], 'hf_w2': hf_w2[i], 'hf_b2': hf_b2[i], 'hf_w3': hf_w3[i],
            'hf_freq': hf_freq[i], 'hf_log_rate': hf_log_rate[i], 'hy_bias': hy_bias[i],
            'cv_w': cv_w[i], 'cv_b': cv_b[i], 'cv_ln_g': cv_ln_g[i], 'cv_ln_b': cv_ln_b[i],
            'w_br': w_br[i], 'w_out': w_out[i],
            'w_router': w_router[i], 'w1': w1[i], 'w3': w3[i], 'w2': w2[i],
        }
        mod_x = (jax.nn.silu(c) @ w_mod[i] + b_mod[i])[:, None, :]
        mod_c = (jax.nn.silu(c_ctx) @ w_mod[i] + b_mod[i])[None, None, :]
        sh1, sc1, g1, sh2, sc2, g2 = jnp.split(mod_x, 6, axis=-1)
        csh1, csc1, cg1, csh2, csc2, cg2 = jnp.split(mod_c, 6, axis=-1)

        hx = modulate(rmsnorm(x, g_mix[i]), sh1, sc1)
        hc = modulate(rmsnorm(xc, g_mix[i]), csh1, csc1)
        pc = hc @ (w_in[i][:, :KV_COLS] if last else w_in[i])
        kc = attn_keys(pc[..., :KV_COLS], lp, None)
        px = hx @ w_in[i]
        kx = attn_keys(px[..., :KV_COLS], lp, pos)
        keys_x = tuple(jnp.concatenate([a, b], axis=1) for a, b in zip(kx, kc))

        x = x + g1 * token_mixers(px[..., KV_COLS:], keys_x, pos, lp)
        x = x + g2 * expert_choice(modulate(rmsnorm(x, g_ffn[i]), sh2, sc2), lp)
        if not last:
            xc = xc + cg1 * token_mixers(pc[..., KV_COLS:], kc, None, lp)
            xc = xc + cg2 * expert_choice(modulate(rmsnorm(xc, g_ffn[i]), csh2, csc2), lp)
    return rmsnorm(x, g_final)
```

```python
import functools
import math

import numpy as np
import jax
import jax.numpy as jnp
from jax import lax
from jax.experimental import pallas as pl
from jax.experimental.pallas import tpu as pltpu

F32 = jnp.float32
BF16 = jnp.bfloat16
HI = lax.Precision.HIGHEST

GRID_W = 64
EPS = 1e-6
ROPE_BASE = 10000.0
GQA_HEADS, GQA_KV_HEADS, GQA_HEAD_DIM = 4, 2, 128
MLA_HEADS, MLA_Q_LORA, MLA_KV_LORA, MLA_NOPE, MLA_ROPE, MLA_V = 4, 384, 256, 64, 32, 128
HY_WIDTH, HY_ORDER, HY_BANDS, HY_FILTER_HIDDEN = 512, 2, 16, 64
CV_WIDTH, CV_KERNEL = 512, 31
N_BRANCH, BRANCH_WIDTH = 4, 512
N_EXPERTS, EXPERT_FF, CAPACITY_FACTOR = 16, 1024, 2
LANES = 128
CONV_PAD = 16
VMEM_LIMIT = 56 * 1024 * 1024


def _cp(sem, vmem=VMEM_LIMIT):
    return pltpu.CompilerParams(dimension_semantics=sem, vmem_limit_bytes=vmem)


def _t(n, pref):
    return pref if n % pref == 0 else n


def _dot(a, b):
    return jnp.dot(a, b, preferred_element_type=F32)


def _mod_kernel(c_ref, w_ref, b_ref, o_ref):
    c = c_ref[...]
    a = (c * jax.nn.sigmoid(c)).astype(BF16)
    o_ref[0] = _dot(a, w_ref[0].astype(BF16)) + b_ref[0]


def _mod_all(cvec, w_mod, b_mod):
    depth, d, n6 = w_mod.shape
    tn = _t(n6, 1024)
    return pl.pallas_call(
        _mod_kernel, grid=(depth, n6 // tn),
        in_specs=[pl.BlockSpec((8, d), lambda l, j: (0, 0)),
                  pl.BlockSpec((1, d, tn), lambda l, j: (l, 0, j)),
                  pl.BlockSpec((1, 1, tn), lambda l, j: (l, 0, j))],
        out_specs=pl.BlockSpec((1, 8, tn), lambda l, j: (l, 0, j)),
        out_shape=jax.ShapeDtypeStruct((depth, 8, n6), F32),
        compiler_params=_cp(("arbitrary", "arbitrary")), name="mod",
    )(cvec, w_mod, b_mod.reshape(depth, 1, n6))


def _norm_h(x_ref, g_ref, sh_ref, sc_ref):
    x = x_ref[0]
    y = x * lax.rsqrt(jnp.mean(x * x, axis=-1, keepdims=True) + EPS)
    return (y * g_ref[...]) * (1.0 + sc_ref[0]) + sh_ref[0]


def _norm_mod_kernel(x_ref, g_ref, sh_ref, sc_ref, o_ref):
    o_ref[0] = _norm_h(x_ref, g_ref, sh_ref, sc_ref).astype(o_ref.dtype)


def _norm_router_kernel(x_ref, g_ref, sh_ref, sc_ref, wr_ref, o_ref, aff_ref):
    h = _norm_h(x_ref, g_ref, sh_ref, sc_ref)
    o_ref[0] = h.astype(o_ref.dtype)
    logits = jnp.dot(h, wr_ref[...], preferred_element_type=F32, precision=HI)
    lane = lax.broadcasted_iota(jnp.int32, logits.shape, 1)
    logits = jnp.where(lane < N_EXPERTS, logits, -1e30)
    e = jnp.exp(logits - jnp.max(logits, axis=-1, keepdims=True))
    aff_ref[0] = e / jnp.sum(e, axis=-1, keepdims=True)


def _norm_mod(x, g, sh, sc, w_router=None):
    b, l, d = x.shape
    tm = _t(l, 256)
    specs = [pl.BlockSpec((1, tm, d), lambda i, j: (i, j, 0)),
             pl.BlockSpec((1, d), lambda i, j: (0, 0)),
             pl.BlockSpec((1, 1, d), lambda i, j: (i, 0, 0)),
             pl.BlockSpec((1, 1, d), lambda i, j: (i, 0, 0))]
    h_spec = pl.BlockSpec((1, tm, d), lambda i, j: (i, j, 0))
    h_shape = jax.ShapeDtypeStruct((b, l, d), BF16)
    if w_router is None:
        return pl.pallas_call(_norm_mod_kernel, grid=(b, l // tm), in_specs=specs, out_specs=h_spec,
                              out_shape=h_shape, compiler_params=_cp(("arbitrary", "arbitrary")),
                              name="norm_mod")(x, g.reshape(1, d), sh, sc)
    wr = jnp.pad(w_router, ((0, 0), (0, LANES - w_router.shape[1])))
    return pl.pallas_call(
        _norm_router_kernel, grid=(b, l // tm),
        in_specs=specs + [pl.BlockSpec((d, LANES), lambda i, j: (0, 0))],
        out_specs=[h_spec, pl.BlockSpec((1, tm, LANES), lambda i, j: (i, j, 0))],
        out_shape=[h_shape, jax.ShapeDtypeStruct((b, l, LANES), F32)],
        compiler_params=_cp(("arbitrary", "arbitrary")), name="norm_router",
    )(x, g.reshape(1, d), sh, sc, wr)


def _final_norm_kernel(x_ref, g_ref, o_ref):
    x = x_ref[0]
    o_ref[0] = x * lax.rsqrt(jnp.mean(x * x, axis=-1, keepdims=True) + EPS) * g_ref[...]


def _final_norm(x, g):
    b, l, d = x.shape
    tm = _t(l, 256)
    return pl.pallas_call(
        _final_norm_kernel, grid=(b, l // tm),
        in_specs=[pl.BlockSpec((1, tm, d), lambda i, j: (i, j, 0)), pl.BlockSpec((1, d), lambda i, j: (0, 0))],
        out_specs=pl.BlockSpec((1, tm, d), lambda i, j: (i, j, 0)),
        out_shape=jax.ShapeDtypeStruct((b, l, d), F32),
        compiler_params=_cp(("arbitrary", "arbitrary")), name="final_norm",
    )(x, g.reshape(1, d))


def _mm_kernel(a_ref, w_ref, o_ref):
    o_ref[...] = _dot(a_ref[...].astype(BF16), w_ref[...].astype(BF16)).astype(o_ref.dtype)


def _matmul(a, w, out_dtype, tm, tn):
    m, k = a.shape
    n = w.shape[1]
    tm, tn = _t(m, tm), _t(n, tn)
    return pl.pallas_call(
        _mm_kernel, grid=(n // tn, m // tm),
        in_specs=[pl.BlockSpec((tm, k), lambda j, i: (i, 0)), pl.BlockSpec((k, tn), lambda j, i: (0, j))],
        out_specs=pl.BlockSpec((tm, tn), lambda j, i: (i, j)),
        out_shape=jax.ShapeDtypeStruct((m, n), out_dtype),
        compiler_params=_cp(("arbitrary", "arbitrary")), name="matmul",
    )(a, w)


def _mm_res_kernel(a_ref, w_ref, x_ref, g_ref, o_ref):
    o_ref[0] = x_ref[0] + g_ref[0] * _dot(a_ref[0], w_ref[...].astype(BF16))


def _matmul_residual(a, w, x, g):
    b, l, k = a.shape
    n = w.shape[1]
    tm, tn = _t(l, 512), _t(n, 512)
    return pl.pallas_call(
        _mm_res_kernel, grid=(n // tn, b, l // tm),
        in_specs=[pl.BlockSpec((1, tm, k), lambda j, bi, i: (bi, i, 0)),
                  pl.BlockSpec((k, tn), lambda j, bi, i: (0, j)),
                  pl.BlockSpec((1, tm, tn), lambda j, bi, i: (bi, i, j)),
                  pl.BlockSpec((1, 1, tn), lambda j, bi, i: (bi, 0, j))],
        out_specs=pl.BlockSpec((1, tm, tn), lambda j, bi, i: (bi, i, j)),
        out_shape=jax.ShapeDtypeStruct((b, l, n), F32),
        compiler_params=_cp(("arbitrary", "arbitrary", "arbitrary")), name="matmul_residual",
    )(a, w, x, g)


ATT_K, ATT_V, ATT_KVA, ATT_KPE, ATT_Q, ATT_QA, ATT_COLS = 0, 256, 512, 768, 896, 1408, 1792


def _rope(xh, tab_ref, half):
    lane = lax.broadcasted_iota(jnp.int32, xh.shape, 1)
    partner = jnp.where((lane % (2 * half)) < half, pltpu.roll(xh, LANES - half, 1), pltpu.roll(xh, half, 1))
    return xh * tab_ref[0] + partner * tab_ref[1]


def _rms(x, gain):
    return x * lax.rsqrt(jnp.mean(x * x, axis=-1, keepdims=True) + EPS) * gain


def _prep_kernel(*refs, use_pos):
    if use_pos:
        px_ref, tg_ref, tm_ref = refs[:3]
        refs = refs[3:]
    else:
        px_ref = refs[0]
        refs = refs[1:]
    gq_ref, gk_ref, gqa_ref, gkva_ref, qb_ref, kvbk_ref, kvbv_ref, kg_ref, vg_ref, km_ref, vm_ref, qg_ref, qm_ref = refs
    p = px_ref[0]
    hd = GQA_HEAD_DIM
    for h in range(GQA_KV_HEADS):
        kh = _rms(p[:, ATT_K + h * hd:ATT_K + (h + 1) * hd], gk_ref[...])
        if use_pos:
            kh = _rope(kh, tg_ref, 32)
        kg_ref[0, :, h * hd:(h + 1) * hd] = kh.astype(BF16)
    vg_ref[0] = p[:, ATT_V:ATT_V + GQA_KV_HEADS * hd].astype(BF16)
    for h in range(GQA_HEADS):
        qh = _rms(p[:, ATT_Q + h * hd:ATT_Q + (h + 1) * hd], gq_ref[...])
        if use_pos:
            qh = _rope(qh, tg_ref, 32)
        qg_ref[0, :, h * hd:(h + 1) * hd] = qh.astype(BF16)
    kvn = _rms(p[:, ATT_KVA:ATT_KVA + MLA_KV_LORA], gkva_ref[...]).astype(BF16)
    knope = _dot(kvn, kvbk_ref[...].astype(BF16))
    vm_ref[0] = _dot(kvn, kvbv_ref[...].astype(BF16)).astype(BF16)
    kpe = p[:, ATT_KPE:ATT_KPE + LANES]
    if use_pos:
        kpe = _rope(kpe, tm_ref, 8)
    qan = _rms(p[:, ATT_QA:ATT_QA + MLA_Q_LORA], gqa_ref[...]).astype(BF16)
    qmf = _dot(qan, qb_ref[...].astype(BF16))
    for h in range(MLA_HEADS):
        km_ref[0, :, h * LANES:(h + 1) * LANES] = (knope[:, h * LANES:(h + 1) * LANES] + kpe).astype(BF16)
        qh = qmf[:, h * LANES:(h + 1) * LANES]
        if use_pos:
            qh = _rope(qh, tm_ref, 8)
        qm_ref[0, :, h * LANES:(h + 1) * LANES] = qh.astype(BF16)


def _attn_prep(px, tabs, lw):
    b, l, _ = px.shape
    tm = _t(l, 256)
    use_pos = tabs is not None
    full = lambda shape: pl.BlockSpec(shape, lambda i, j: (0,) * len(shape))
    in_specs = [pl.BlockSpec((1, tm, ATT_COLS), lambda i, j: (i, j, 0))]
    args = [px]
    if use_pos:
        in_specs += [pl.BlockSpec((2, tm, LANES), lambda i, j: (0, j, 0))] * 2
        args += list(tabs)
    small = [lw['gqa_q_gain'].reshape(1, -1), lw['gqa_k_gain'].reshape(1, -1), lw['mla_q_a_gain'].reshape(1, -1),
             lw['mla_kv_a_gain'].reshape(1, -1), lw['q_b'], lw['kv_b_k'], lw['kv_b_v']]
    in_specs += [full(a.shape) for a in small]
    widths = [256, 256, 512, 512, 512, 512]
    return pl.pallas_call(
        functools.partial(_prep_kernel, use_pos=use_pos), grid=(b, l // tm), in_specs=in_specs,
        out_specs=[pl.BlockSpec((1, tm, w), lambda i, j: (i, j, 0)) for w in widths],
        out_shape=[jax.ShapeDtypeStruct((b, l, w), BF16) for w in widths],
        compiler_params=_cp(("arbitrary", "arbitrary")), name="attn_prep",
    )(*args, *small)


def _attn_kernel(*refs, n_src, scale):
    q = refs[0][0]
    o_ref = refs[-1]
    ss = [lax.dot_general(q, refs[1 + 2 * j][0], (((1,), (1,)), ((), ())), preferred_element_type=F32)
          for j in range(n_src)]
    m = jnp.max(ss[0], axis=-1, keepdims=True)
    for s in ss[1:]:
        m = jnp.maximum(m, jnp.max(s, axis=-1, keepdims=True))
    acc, den = None, None
    for j, s in enumerate(ss):
        p = jnp.exp((s - m) * scale)
        d = jnp.sum(p, axis=-1, keepdims=True)
        a = _dot(p.astype(BF16), refs[2 + 2 * j][0])
        acc = a if acc is None else acc + a
        den = d if den is None else den + d
    o_ref[0] = (acc / den).astype(o_ref.dtype)


def _attention(q, srcs, heads, kv_heads, dk):
    b, s, _ = q.shape
    r = heads // kv_heads
    tq = _t(s, 512)
    in_specs = [pl.BlockSpec((1, tq, LANES), lambda bi, h, i: (bi, i, h))]
    args = [q]
    for k, v in srcs:
        lk = k.shape[1]
        in_specs += [pl.BlockSpec((1, lk, LANES), lambda bi, h, i: (bi, 0, h // r))] * 2
        args += [k, v]
    return pl.pallas_call(
        functools.partial(_attn_kernel, n_src=len(srcs), scale=float(dk) ** -0.5),
        grid=(b, heads, s // tq), in_specs=in_specs,
        out_specs=pl.BlockSpec((1, tq, LANES), lambda bi, h, i: (bi, i, h)),
        out_shape=jax.ShapeDtypeStruct((b, s, heads * LANES), BF16),
        compiler_params=_cp(("arbitrary", "arbitrary", "arbitrary")), name="attention",
    )(*args)


def _dwconv_kernel(*refs, taps, glu, post_ln, chunk, sub):
    if glu:
        ap, ac, an, bp, bc, bn, w_ref, b_ref, lg_ref, lb_ref, o_ref, scr = refs
        load = lambda a, g: a[0] * jax.nn.sigmoid(g[0])
        prev, cur, nxt = load(ap, bp), load(ac, bc), load(an, bn)
    else:
        ap, ac, an, w_ref, b_ref, o_ref, scr = refs
        prev, cur, nxt = ap[0], ac[0], an[0]
    i = pl.program_id(2)
    last = pl.num_programs(2) - 1
    pad = CONV_PAD
    scr[0:pad, :] = jnp.where(i > 0, prev, 0.0)
    scr[pad:pad + chunk, :] = cur
    scr[pad + chunk:2 * pad + chunk, :] = jnp.where(i < last, nxt, 0.0)
    lo = (taps - 1) // 2
    for c in range(chunk // sub):
        acc = None
        for j in range(taps):
            term = w_ref[j:j + 1, :] * scr[pl.ds(c * sub + pad - lo + j, sub), :]
            acc = term if acc is None else acc + term
        y = acc + b_ref[...]
        if post_ln:
            mu = jnp.mean(y, axis=-1, keepdims=True)
            yc = y - mu
            var = jnp.mean(yc * yc, axis=-1, keepdims=True)
            y = yc * lax.rsqrt(var + EPS) * lg_ref[...] + lb_ref[...]
            y = y * jax.nn.sigmoid(y)
        o_ref[0, c * sub:(c + 1) * sub, :] = y.astype(o_ref.dtype)


def _dwconv(x, w, bias, ncol, *, glu=False, ln=None, out_dtype=F32):
    b, l, _ = x.shape
    taps = w.shape[0]
    cw = 512
    chunk = _t(l, 256)
    nblk = l // chunk
    per = chunk // CONV_PAD
    nsmall = l // CONV_PAD

    def views(coff):
        return [pl.BlockSpec((1, CONV_PAD, cw), lambda bi, c, i: (bi, jnp.maximum(i * per - 1, 0), c + coff)),
                pl.BlockSpec((1, chunk, cw), lambda bi, c, i: (bi, i, c + coff)),
                pl.BlockSpec((1, CONV_PAD, cw), lambda bi, c, i: (bi, jnp.minimum((i + 1) * per, nsmall - 1), c + coff))]

    in_specs = views(0)
    args = [x, x, x]
    if glu:
        in_specs += views(ncol)
        args += [x, x, x]
    vec = lambda: pl.BlockSpec((1, cw), lambda bi, c, i: (0, c))
    in_specs += [pl.BlockSpec((taps, cw), lambda bi, c, i: (0, c)), vec()]
    args += [w, bias.reshape(1, -1)]
    if ln is not None:
        in_specs += [vec(), vec()]
        args += [ln[0].reshape(1, -1), ln[1].reshape(1, -1)]
    return pl.pallas_call(
        functools.partial(_dwconv_kernel, taps=taps, glu=glu, post_ln=ln is not None, chunk=chunk, sub=32),
        grid=(b, ncol, nblk), in_specs=in_specs,
        out_specs=pl.BlockSpec((1, chunk, cw), lambda bi, c, i: (bi, i, c)),
        out_shape=jax.ShapeDtypeStruct((b, l, ncol * cw), out_dtype),
        scratch_shapes=[pltpu.VMEM((chunk + 2 * CONV_PAD, cw), F32)],
        compiler_params=_cp(("arbitrary", "arbitrary", "arbitrary")), name="dwconv",
    )(*args)


def _dft_tables(l):
    n2 = 2 * l
    k = jnp.arange(l, dtype=jnp.int32)
    m = (k[:, None] * k[None, :]) & (n2 - 1)
    ang = m.astype(F32) * (2.0 * math.pi / n2)
    c = jnp.cos(ang)
    s = -jnp.sin(ang)
    nyq = jnp.where(k % 2 == 0, 1.0, -1.0).astype(F32)
    s = jnp.where((k == 0)[:, None], nyq[None, :], s)
    p = jnp.stack([c, s]).astype(BF16)
    return p, jnp.swapaxes(p, 1, 2)


def _hy_feats(l):
    pos = jnp.arange(l, dtype=F32)
    t01 = pos / (l - 1)
    bands = jnp.linspace(1e-4, HY_BANDS - 1, HY_BANDS, dtype=F32)
    ang = (2.0 * math.pi / l) * pos[:, None] * bands[None, :]
    feats = jnp.concatenate([t01[:, None], jnp.cos(ang), -jnp.sin(ang)], axis=-1)
    return jnp.pad(feats, ((0, 0), (0, LANES - feats.shape[1])))


def _hyfilt_kernel(f_ref, w1_ref, b1_ref, fr_ref, w2_ref, b2_ref, w3_ref, lr_ref, o_ref, *, tl):
    feats = f_ref[...]
    fr = fr_ref[...]
    hdot = lambda a, b: jnp.dot(a, b, preferred_element_type=F32, precision=HI)
    h = jnp.sin(fr * (hdot(feats, w1_ref[...]) + b1_ref[...]))
    h = jnp.sin(fr * (hdot(h, w2_ref[...]) + b2_ref[...]))
    h = hdot(h, w3_ref[...])
    h = h * jnp.exp(-feats[:, 0:1] * jnp.exp(lr_ref[...]))
    row = pl.program_id(0) * tl + lax.broadcasted_iota(jnp.int32, h.shape, 0)
    col = lax.broadcasted_iota(jnp.int32, h.shape, 1)
    is_bwd = ((col // HY_WIDTH) % 2) == 1
    o_ref[...] = jnp.where((row == 0) & is_bwd, 0.0, h).astype(o_ref.dtype)


def _hy_filters(l, lw):
    feats = _hy_feats(l)
    hid = HY_FILTER_HIDDEN
    padc = lambda a: jnp.pad(a, ((0, 0), (0, LANES - a.shape[1])))
    w1 = jnp.pad(lw['hf_w1'], ((0, LANES - lw['hf_w1'].shape[0]), (0, LANES - hid)))
    w2 = jnp.pad(lw['hf_w2'], ((0, LANES - hid), (0, LANES - hid)))
    w3 = jnp.pad(lw['hf_w3'], ((0, LANES - hid), (0, 0)))
    b1, b2, fr = padc(lw['hf_b1'].reshape(1, -1)), padc(lw['hf_b2'].reshape(1, -1)), padc(lw['hf_freq'].reshape(1, -1))
    lr = lw['hf_log_rate'].reshape(1, -1)
    nc = w3.shape[1]
    tl = _t(l, 256)
    full = lambda a: pl.BlockSpec(a.shape, lambda i: (0, 0))
    return pl.pallas_call(
        functools.partial(_hyfilt_kernel, tl=tl), grid=(l // tl,),
        in_specs=[pl.BlockSpec((tl, LANES), lambda i: (i, 0)), full(w1), full(b1), full(fr), full(w2), full(b2),
                  full(w3), full(lr)],
        out_specs=pl.BlockSpec((tl, nc), lambda i: (i, 0)),
        out_shape=jax.ShapeDtypeStruct((l, nc), BF16),
        compiler_params=_cp(("arbitrary",)), name="hy_filter",
    )(feats, w1, b1, fr, w2, b2, w3, lr)


def _hyspec_kernel(p_ref, h_ref, o_ref, *, tk, scale):
    hm = h_ref[...]
    ar = _dot(p_ref[0], hm)
    ai = _dot(p_ref[1], hm)
    c = HY_WIDTH
    row0 = (pl.program_id(0) * tk + lax.broadcasted_iota(jnp.int32, (tk, c), 0)) == 0
    s = jnp.where(row0, 0.5 * scale, scale)
    for n in range(HY_ORDER):
        o = 2 * n * c
        o_ref[n, 0] = (ar[:, o:o + c] + ar[:, o + c:o + 2 * c]) * s
        fi, bi = ai[:, o:o + c], ai[:, o + c:o + 2 * c]
        o_ref[n, 1] = jnp.where(row0, fi + bi, fi - bi) * s


def _hy_spectrum(p, filt):
    l = p.shape[1]
    tk = _t(l, 256)
    return pl.pallas_call(
        functools.partial(_hyspec_kernel, tk=tk, scale=1.0 / l), grid=(l // tk,),
        in_specs=[pl.BlockSpec((2, tk, l), lambda i: (0, i, 0)), pl.BlockSpec(filt.shape, lambda i: (0, 0))],
        out_specs=pl.BlockSpec((HY_ORDER, 2, tk, HY_WIDTH), lambda i: (0, 0, i, 0)),
        out_shape=jax.ShapeDtypeStruct((HY_ORDER, 2, l, HY_WIDTH), F32),
        compiler_params=_cp(("arbitrary",)), name="hy_spectrum",
    )(p, filt)


def _hyfwd_kernel(p_ref, z_ref, k_ref, y_ref, *, tk):
    z = z_ref[0].astype(BF16)
    xr = _dot(p_ref[0], z)
    xi = _dot(p_ref[1], z)
    kr, ki = k_ref[0, 0], k_ref[0, 1]
    row0 = (pl.program_id(0) * tk + lax.broadcasted_iota(jnp.int32, xr.shape, 0)) == 0
    xiki = xi * ki
    y_ref[0, 0] = (xr * kr - jnp.where(row0, 0.0, xiki)).astype(y_ref.dtype)
    y_ref[0, 1] = jnp.where(row0, xiki, xr * ki + xi * kr).astype(y_ref.dtype)


def _hy_forward(p, z, zcol, kf, order):
    b, l, _ = z.shape
    c = HY_WIDTH
    tk = _t(l, 512)
    return pl.pallas_call(
        functools.partial(_hyfwd_kernel, tk=tk), grid=(l // tk, b),
        in_specs=[pl.BlockSpec((2, tk, l), lambda i, bi: (0, i, 0)),
                  pl.BlockSpec((1, l, c), lambda i, bi: (bi, 0, zcol)),
                  pl.BlockSpec((1, 2, tk, c), lambda i, bi: (order, 0, i, 0))],
        out_specs=pl.BlockSpec((1, 2, tk, c), lambda i, bi: (bi, 0, i, 0)),
        out_shape=jax.ShapeDtypeStruct((b, 2, l, c), BF16),
        compiler_params=_cp(("arbitrary", "arbitrary")), name="hy_forward",
    )(p, z, kf)


def _hyinv_kernel(pt_ref, y_ref, g_ref, z_ref, bias_ref, o_ref):
    conv = _dot(pt_ref[0], y_ref[0, 0]) + _dot(pt_ref[1], y_ref[0, 1])
    o_ref[0] = (g_ref[0] * (conv + bias_ref[...] * z_ref[0])).astype(o_ref.dtype)


def _hy_inverse(pt, y, u, gcol, z, zcol, bias, out_dtype):
    b, _, l, c = y.shape
    tn = _t(l, 512)
    return pl.pallas_call(
        _hyinv_kernel, grid=(l // tn, b),
        in_specs=[pl.BlockSpec((2, tn, l), lambda i, bi: (0, i, 0)),
                  pl.BlockSpec((1, 2, l, c), lambda i, bi: (bi, 0, 0, 0)),
                  pl.BlockSpec((1, tn, c), lambda i, bi: (bi, i, gcol)),
                  pl.BlockSpec((1, tn, c), lambda i, bi: (bi, i, zcol)),
                  pl.BlockSpec((1, c), lambda i, bi: (0, 0))],
        out_specs=pl.BlockSpec((1, tn, c), lambda i, bi: (bi, i, 0)),
        out_shape=jax.ShapeDtypeStruct((b, l, c), out_dtype),
        compiler_params=_cp(("arbitrary", "arbitrary")), name="hy_inverse",
    )(pt, y, u, z, bias.reshape(1, c))


def _hyena(phy, lw, dft):
    p, pt, kf = dft
    u = _dwconv(phy, lw['hy_conv_w'], lw['hy_conv_b'], HY_ORDER + 1)
    y = _hy_forward(p, u, 0, kf, 0)
    z = _hy_inverse(pt, y, u, 1, u, 0, lw['hy_bias'][0], F32)
    y = _hy_forward(p, z, 0, kf, 1)
    return _hy_inverse(pt, y, u, 2, z, 0, lw['hy_bias'][1], BF16)


def _merge_kernel(hx_ref, g0, g1, g2, g3, b0, b1, b2, b3, wbr_ref, o_ref):
    hx = hx_ref[...]
    acc = None
    for n, (wg, br) in enumerate(((g0, b0), (g1, b1), (g2, b2), (g3, b3))):
        gate = jax.nn.sigmoid(_dot(hx, wg[...]))
        term = gate * _dot(br[...], wbr_ref[n].astype(BF16))
        acc = term if acc is None else acc + term
    o_ref[...] = acc.astype(o_ref.dtype)


def _merge(hx, w_gate, branches, w_br):
    m, d = hx.shape
    tm, tn = _t(m, 512), _t(d, 512)
    nj = d // tn
    bw = BRANCH_WIDTH
    in_specs = [pl.BlockSpec((tm, d), lambda j, i: (i, 0))]
    in_specs += [pl.BlockSpec((d, tn), functools.partial(lambda j, i, n: (0, n * nj + j), n=n)) for n in range(N_BRANCH)]
    in_specs += [pl.BlockSpec((tm, bw), lambda j, i: (i, 0))] * N_BRANCH
    in_specs += [pl.BlockSpec((N_BRANCH, bw, tn), lambda j, i: (0, 0, j))]
    return pl.pallas_call(
        _merge_kernel, grid=(nj, m // tm), in_specs=in_specs,
        out_specs=pl.BlockSpec((tm, tn), lambda j, i: (i, j)),
        out_shape=jax.ShapeDtypeStruct((m, d), BF16),
        compiler_params=_cp(("arbitrary", "arbitrary")), name="merge",
    )(hx, w_gate, w_gate, w_gate, w_gate, *branches, w_br)


def _topk_kernel(aff_ref, tri_ref, sp_ref, spt_ref, *, cap):
    afft = aff_ref[0].T[:N_EXPERTS, :]
    bits = lax.bitcast_convert_type(afft, jnp.int32)

    def body(i, prefix):
        cand = prefix | jnp.left_shift(jnp.int32(1), 30 - i)
        cnt = jnp.sum((bits >= cand).astype(F32), axis=1, keepdims=True)
        return jnp.where(cnt >= cap, cand, prefix)

    thr = lax.fori_loop(0, 31, body, jnp.zeros((N_EXPERTS, 1), jnp.int32))
    gt = bits > thr
    eq = bits == thr
    need = cap - jnp.sum(gt.astype(F32), axis=1, keepdims=True)
    tri = tri_ref[...]
    rank_eq = _dot(eq.astype(BF16), tri)
    sel = gt | (eq & (rank_eq <= need))
    pos = _dot(sel.astype(BF16), tri) - 1.0
    sp = jnp.where(sel, pos, -1.0)
    sp_ref[0] = sp.astype(jnp.int32)
    n = sp.shape[1]
    padded = jnp.concatenate([sp, jnp.full((LANES - N_EXPERTS, n), -1.0, F32)], axis=0)
    spt_ref[0] = padded.T


def _topk(aff, cap):
    b, n, _ = aff.shape
    t = jnp.arange(n, dtype=jnp.int32)
    tri = (t[:, None] <= t[None, :]).astype(BF16)
    return pl.pallas_call(
        functools.partial(_topk_kernel, cap=float(cap)), grid=(b,),
        in_specs=[pl.BlockSpec((1, n, LANES), lambda i: (i, 0, 0)), pl.BlockSpec((n, n), lambda i: (0, 0))],
        out_specs=[pl.BlockSpec((1, N_EXPERTS, n), lambda i: (i, 0, 0)), pl.BlockSpec((1, n, LANES), lambda i: (i, 0, 0))],
        out_shape=[jax.ShapeDtypeStruct((b, N_EXPERTS, n), jnp.int32), jax.ShapeDtypeStruct((b, n, LANES), F32)],
        compiler_params=_cp(("arbitrary",)), name="topk",
    )(aff, tri)


def _gather_kernel(sp_ref, h_ref, o_ref, *, cap):
    sp = sp_ref[0, 0]
    slot = lax.broadcasted_iota(jnp.int32, (cap, sp.shape[1]), 0)
    onehot = (slot == sp).astype(BF16)
    o_ref[0] = _dot(onehot, h_ref[0]).astype(o_ref.dtype)


def _gather(sp, h, cap):
    b, n, d = h.shape
    return pl.pallas_call(
        functools.partial(_gather_kernel, cap=cap), grid=(b, N_EXPERTS),
        in_specs=[pl.BlockSpec((1, 1, 1, n), lambda bi, e: (bi, e, 0, 0)), pl.BlockSpec((1, n, d), lambda bi, e: (bi, 0, 0))],
        out_specs=pl.BlockSpec((1, cap, d), lambda bi, e: (e, bi, 0)),
        out_shape=jax.ShapeDtypeStruct((N_EXPERTS, b * cap, d), BF16),
        compiler_params=_cp(("arbitrary", "arbitrary")), name="moe_gather",
    )(sp.reshape(b, N_EXPERTS, 1, n), h)


def _ffn_kernel(*refs, n_x):
    xs = refs[:n_x]
    w1_ref, w3_ref, w2_ref = refs[n_x:n_x + 3]
    ys = refs[n_x + 3:2 * n_x + 3]
    accs = refs[2 * n_x + 3:]
    f = pl.program_id(1)
    last = pl.num_programs(1) - 1
    w1, w3, w2 = w1_ref[0].astype(BF16), w3_ref[0].astype(BF16), w2_ref[0].astype(BF16)
    for x_ref, y_ref, acc in zip(xs, ys, accs):
        x = x_ref[0]
        a = _dot(x, w1)
        hid = (a * jax.nn.sigmoid(a) * _dot(x, w3)).astype(BF16)
        part = _dot(hid, w2)

        @pl.when(f == 0)
        def _():
            acc[...] = part

        @pl.when(f > 0)
        def _():
            acc[...] += part

        @pl.when(f == last)
        def _():
            y_ref[0] = acc[...].astype(y_ref.dtype)


def _expert_ffn(xs, w1, w3, w2):
    e, d, ff = w1.shape
    tf = _t(ff, 256)
    in_specs = [pl.BlockSpec((1, x.shape[1], d), lambda ei, f: (ei, 0, 0)) for x in xs]
    in_specs += [pl.BlockSpec((1, d, tf), lambda ei, f: (ei, 0, f))] * 2 + [pl.BlockSpec((1, tf, d), lambda ei, f: (ei, f, 0))]
    return pl.pallas_call(
        functools.partial(_ffn_kernel, n_x=len(xs)), grid=(e, ff // tf), in_specs=in_specs,
        out_specs=[pl.BlockSpec((1, x.shape[1], d), lambda ei, f: (ei, 0, 0)) for x in xs],
        out_shape=[jax.ShapeDtypeStruct(x.shape, BF16) for x in xs],
        scratch_shapes=[pltpu.VMEM((x.shape[1], d), F32) for x in xs],
        compiler_params=_cp(("arbitrary", "arbitrary")), name="expert_ffn",
    )(*xs, w1, w3, w2)


def _scatter_kernel(spt_ref, aff_ref, y_ref, x_ref, g_ref, o_ref, acc_ref, *, cap):
    e = pl.program_id(2)
    lane = lax.broadcasted_iota(jnp.int32, spt_ref.shape[1:], 1)
    pick = lambda v: jnp.sum(jnp.where(lane == e, v, 0.0), axis=1, keepdims=True)
    sp = pick(spt_ref[0])
    a = pick(aff_ref[0])
    slot = lax.broadcasted_iota(jnp.int32, (sp.shape[0], cap), 1).astype(F32)
    onehot = (slot == sp).astype(BF16)
    term = a * _dot(onehot, y_ref[0])

    @pl.when(e == 0)
    def _():
        acc_ref[...] = term

    @pl.when(e > 0)
    def _():
        acc_ref[...] += term

    @pl.when(e == pl.num_programs(2) - 1)
    def _():
        o_ref[0] = x_ref[0] + g_ref[0] * acc_ref[...]


def _scatter(spt, aff, y, x, g, cap):
    b, n, d = x.shape
    tt = _t(n, 512)
    return pl.pallas_call(
        functools.partial(_scatter_kernel, cap=cap), grid=(b, n // tt, N_EXPERTS),
        in_specs=[pl.BlockSpec((1, tt, LANES), lambda bi, i, e: (bi, i, 0)),
                  pl.BlockSpec((1, tt, LANES), lambda bi, i, e: (bi, i, 0)),
                  pl.BlockSpec((1, cap, d), lambda bi, i, e: (e, bi, 0)),
                  pl.BlockSpec((1, tt, d), lambda bi, i, e: (bi, i, 0)),
                  pl.BlockSpec((1, 1, d), lambda bi, i, e: (bi, 0, 0))],
        out_specs=pl.BlockSpec((1, tt, d), lambda bi, i, e: (bi, i, 0)),
        out_shape=jax.ShapeDtypeStruct((b, n, d), F32),
        scratch_shapes=[pltpu.VMEM((tt, d), F32)],
        compiler_params=_cp(("arbitrary", "arbitrary", "arbitrary")), name="moe_scatter",
    )(spt, aff, y, x, g)


def _rope_tables(s):
    rows = jnp.repeat(jnp.arange(s // GRID_W, dtype=jnp.int32), GRID_W).astype(F32)
    cols = jnp.tile(jnp.arange(GRID_W, dtype=jnp.int32), s // GRID_W).astype(F32)

    def half_tables(m):
        inv = ROPE_BASE ** (-jnp.arange(0, m, 2, dtype=F32) / m)
        out = []
        for pos in (rows, cols):
            ang = pos[:, None] * inv[None, :]
            c, sn = jnp.cos(ang), jnp.sin(ang)
            out.append((jnp.concatenate([c, c], axis=1), jnp.concatenate([-sn, sn], axis=1)))
        return (jnp.concatenate([out[0][0], out[1][0]], axis=1), jnp.concatenate([out[0][1], out[1][1]], axis=1))

    cg, sg = half_tables(GQA_HEAD_DIM // 2)
    cm, sm = half_tables(MLA_ROPE // 2)
    padm = lambda a, fill: jnp.concatenate(
        [jnp.full((s, MLA_NOPE), fill, F32), a, jnp.full((s, LANES - MLA_NOPE - MLA_ROPE), fill, F32)], axis=1)
    return jnp.stack([cg, sg]), jnp.stack([padm(cm, 1.0), padm(sm, 0.0)])


def _layer_weights(i, w):
    d = w['w_in'].shape[1]
    wi = w['w_in'][i]
    kv_cols = 2 * GQA_KV_HEADS * GQA_HEAD_DIM + MLA_KV_LORA + MLA_ROPE
    q_cols = GQA_HEADS * GQA_HEAD_DIM + MLA_Q_LORA
    o_kpe = kv_cols - MLA_ROPE
    o_q = kv_cols
    o_hy = o_q + q_cols
    o_cv = o_hy + (HY_ORDER + 1) * HY_WIDTH
    o_g = o_cv + 2 * CV_WIDTH
    z = lambda n: jnp.zeros((d, n), wi.dtype)
    w_attn = jnp.concatenate([wi[:, :o_kpe], z(MLA_NOPE), wi[:, o_kpe:o_q], z(LANES - MLA_NOPE - MLA_ROPE),
                              wi[:, o_q:o_hy]], axis=1).astype(BF16)
    qb = w['mla_q_b'][i].reshape(MLA_Q_LORA, MLA_HEADS, MLA_NOPE + MLA_ROPE)
    qb = jnp.pad(qb, ((0, 0), (0, 0), (0, LANES - MLA_NOPE - MLA_ROPE))).reshape(MLA_Q_LORA, MLA_HEADS * LANES)
    kvb = w['mla_kv_b'][i].reshape(MLA_KV_LORA, MLA_HEADS, MLA_NOPE + MLA_V)
    kvb_k = jnp.pad(kvb[:, :, :MLA_NOPE], ((0, 0), (0, 0), (0, LANES - MLA_NOPE))).reshape(MLA_KV_LORA, MLA_HEADS * LANES)
    kvb_v = kvb[:, :, MLA_NOPE:].reshape(MLA_KV_LORA, MLA_HEADS * MLA_V)
    lw = {k: w[k][i] for k in ('gqa_q_gain', 'gqa_k_gain', 'mla_q_a_gain', 'mla_kv_a_gain', 'hy_conv_w', 'hy_conv_b',
                               'hf_w1', 'hf_b1', 'hf_w2', 'hf_b2', 'hf_w3', 'hf_freq', 'hf_log_rate', 'hy_bias',
                               'cv_w', 'cv_b', 'cv_ln_g', 'cv_ln_b', 'w_br', 'w_out', 'w_router', 'w1', 'w3', 'w2',
                               'g_mix', 'g_ffn')}
    lw.update(w_attn=w_attn, w_hy=wi[:, o_hy:o_cv].astype(BF16), w_cv=wi[:, o_cv:o_g].astype(BF16),
              w_gate=wi[:, o_g:].astype(BF16), q_b=qb, kv_b_k=kvb_k, kv_b_v=kvb_v)
    return lw


def _mixer_branches(hx, keys_extra, tabs, lw, dft):
    b, l, d = hx.shape
    hx2 = hx.reshape(b * l, d)
    px = _matmul(hx2, lw['w_attn'], F32, 512, 896).reshape(b, l, -1)
    kg, vg, km, vm, qg, qm = _attn_prep(px, tabs, lw)
    own = (kg, vg, km, vm)
    if dft is None:
        return None, own
    ek = keys_extra
    o_gqa = _attention(qg, [(kg, vg)] + ([(ek[0], ek[1])] if ek else []), GQA_HEADS, GQA_KV_HEADS, GQA_HEAD_DIM)
    o_mla = _attention(qm, [(km, vm)] + ([(ek[2], ek[3])] if ek else []), MLA_HEADS, MLA_HEADS, MLA_NOPE + MLA_ROPE)
    phy = _matmul(hx2, lw['w_hy'], F32, 512, 768).reshape(b, l, -1)
    o_hy = _hyena(phy, lw, dft)
    pcv = _matmul(hx2, lw['w_cv'], F32, 512, 1024).reshape(b, l, -1)
    o_cv = _dwconv(pcv, lw['cv_w'], lw['cv_b'], 1, glu=True, ln=(lw['cv_ln_g'], lw['cv_ln_b']), out_dtype=BF16)
    flat = lambda a: a.reshape(b * l, -1)
    return [flat(o_hy), flat(o_gqa), flat(o_mla), flat(o_cv)], own


def kernel(x, c, ctx, c_ctx, w_mod, b_mod, g_mix, w_in, gqa_q_gain, gqa_k_gain, mla_q_a_gain, mla_q_b, mla_kv_a_gain, mla_kv_b, hy_conv_w, hy_conv_b, hf_w1, hf_b1, hf_w2, hf_b2, hf_w3, hf_freq, hf_log_rate, hy_bias, cv_w, cv_b, cv_ln_g, cv_ln_b, w_br, w_out, g_ffn, w_router, w1, w3, w2, g_final):
    w = dict(g_mix=g_mix, w_in=w_in, gqa_q_gain=gqa_q_gain, gqa_k_gain=gqa_k_gain, mla_q_a_gain=mla_q_a_gain,
             mla_q_b=mla_q_b, mla_kv_a_gain=mla_kv_a_gain, mla_kv_b=mla_kv_b, hy_conv_w=hy_conv_w, hy_conv_b=hy_conv_b,
             hf_w1=hf_w1, hf_b1=hf_b1, hf_w2=hf_w2, hf_b2=hf_b2, hf_w3=hf_w3, hf_freq=hf_freq, hf_log_rate=hf_log_rate,
             hy_bias=hy_bias, cv_w=cv_w, cv_b=cv_b, cv_ln_g=cv_ln_g, cv_ln_b=cv_ln_b, w_br=w_br, w_out=w_out,
             g_ffn=g_ffn, w_router=w_router, w1=w1, w3=w3, w2=w2)
    bsz, s, d = x.shape
    lc = ctx.shape[1]
    depth = w_mod.shape[0]
    assert bsz < 8
    cvec = jnp.concatenate([c, c_ctx[None, :], jnp.zeros((8 - bsz - 1, d), F32)], axis=0)
    mod = _mod_all(cvec, w_mod, b_mod)
    tabs = _rope_tables(s)
    p_x, pt_x = _dft_tables(s)
    p_c, pt_c = _dft_tables(lc)
    cap_x = CAPACITY_FACTOR * s // N_EXPERTS
    cap_c = CAPACITY_FACTOR * lc // N_EXPERTS
    xc = ctx
    for i in range(depth):
        last = i == depth - 1
        lw = _layer_weights(i, w)
        mx = [mod[i, :bsz, j * d:(j + 1) * d].reshape(bsz, 1, d) for j in range(6)]
        mc = [jnp.broadcast_to(mod[i, bsz:bsz + 1, j * d:(j + 1) * d].reshape(1, 1, d), (bsz, 1, d)) for j in range(6)]

        hc = _norm_mod(xc, lw['g_mix'], mc[0], mc[1])
        dft_c = None if last else (p_c, pt_c, _hy_spectrum(p_c, _hy_filters(lc, lw)))
        br_c, keys_c = _mixer_branches(hc, None, None, lw, dft_c)
        hx = _norm_mod(x, lw['g_mix'], mx[0], mx[1])
        dft_x = (p_x, pt_x, _hy_spectrum(p_x, _hy_filters(s, lw)))
        br_x, _ = _mixer_branches(hx, keys_c, tabs, lw, dft_x)

        merged = _merge(hx.reshape(bsz * s, d), lw['w_gate'], br_x, lw['w_br']).reshape(bsz, s, d)
        x = _matmul_residual(merged, lw['w_out'], x, mx[2])
        hf, aff = _norm_mod(x, lw['g_ffn'], mx[3], mx[4], lw['w_router'])
        sp, spt = _topk(aff, cap_x)
        xs = [_gather(sp, hf, cap_x)]
        if not last:
            merged_c = _merge(hc.reshape(bsz * lc, d), lw['w_gate'], br_c, lw['w_br']).reshape(bsz, lc, d)
            xc = _matmul_residual(merged_c, lw['w_out'], xc, mc[2])
            hfc, aff_c = _norm_mod(xc, lw['g_ffn'], mc[3], mc[4], lw['w_router'])
            sp_c, spt_c = _topk(aff_c, cap_c)
            xs.append(_gather(sp_c, hfc, cap_c))
        ys = _expert_ffn(xs, lw['w1'], lw['w3'], lw['w2'])
        x = _scatter(spt, aff, ys[0], x, mx[5], cap_x)
        if not last:
            xc = _scatter(spt_c, aff_c, ys[1], xc, mc[5], cap_c)
    return _final_norm(x, g_final)
```

```python
import functools
import math

import numpy as np
import jax
import jax.numpy as jnp
from jax import lax
from jax.experimental import pallas as pl
from jax.experimental.pallas import tpu as pltpu

F32 = jnp.float32
BF16 = jnp.bfloat16
HI = lax.Precision.HIGHEST

GRID_W = 64
EPS = 1e-6
ROPE_BASE = 10000.0
GQA_HEADS, GQA_KV_HEADS, GQA_HEAD_DIM = 4, 2, 128
MLA_HEADS, MLA_Q_LORA, MLA_KV_LORA, MLA_NOPE, MLA_ROPE, MLA_V = 4, 384, 256, 64, 32, 128
HY_WIDTH, HY_ORDER, HY_BANDS, HY_FILTER_HIDDEN = 512, 2, 16, 64
CV_WIDTH, CV_KERNEL = 512, 31
N_BRANCH, BRANCH_WIDTH = 4, 512
N_EXPERTS, EXPERT_FF, CAPACITY_FACTOR = 16, 1024, 2
LANES = 128
SUBLANES = 8
CONV_PAD = 16
VMEM_LIMIT = 56 * 1024 * 1024


def _cp(sem, vmem=VMEM_LIMIT):
    return pltpu.CompilerParams(dimension_semantics=sem, vmem_limit_bytes=vmem)


def _t(n, pref):
    return pref if n % pref == 0 else n


def _dot(a, b):
    return jnp.dot(a, b, preferred_element_type=F32)


def _mod_kernel(c_ref, w_ref, b_ref, o_ref):
    c = c_ref[...]
    a = (c * jax.nn.sigmoid(c)).astype(BF16)
    o_ref[0] = _dot(a, w_ref[0].astype(BF16)) + b_ref[0]


def _mod_all(cvec, w_mod, b_mod):
    depth, d, n6 = w_mod.shape
    tn = _t(n6, 1024)
    return pl.pallas_call(
        _mod_kernel, grid=(depth, n6 // tn),
        in_specs=[pl.BlockSpec((8, d), lambda l, j: (0, 0)),
                  pl.BlockSpec((1, d, tn), lambda l, j: (l, 0, j)),
                  pl.BlockSpec((1, 1, tn), lambda l, j: (l, 0, j))],
        out_specs=pl.BlockSpec((1, 8, tn), lambda l, j: (l, 0, j)),
        out_shape=jax.ShapeDtypeStruct((depth, 8, n6), F32),
        compiler_params=_cp(("arbitrary", "arbitrary")), name="mod",
    )(cvec, w_mod, b_mod.reshape(depth, 1, n6))


def _norm_h(x_ref, g_ref, sh_ref, sc_ref):
    x = x_ref[0]
    y = x * lax.rsqrt(jnp.mean(x * x, axis=-1, keepdims=True) + EPS)
    return (y * g_ref[...]) * (1.0 + sc_ref[0]) + sh_ref[0]


def _norm_mod_kernel(x_ref, g_ref, sh_ref, sc_ref, o_ref):
    o_ref[0] = _norm_h(x_ref, g_ref, sh_ref, sc_ref).astype(o_ref.dtype)


def _norm_router_kernel(x_ref, g_ref, sh_ref, sc_ref, wr_ref, o_ref, aff_ref):
    h = _norm_h(x_ref, g_ref, sh_ref, sc_ref)
    o_ref[0] = h.astype(o_ref.dtype)
    logits = jnp.dot(h, wr_ref[...], preferred_element_type=F32, precision=HI)
    lane = lax.broadcasted_iota(jnp.int32, logits.shape, 1)
    logits = jnp.where(lane < N_EXPERTS, logits, -1e30)
    e = jnp.exp(logits - jnp.max(logits, axis=-1, keepdims=True))
    aff_ref[0] = e / jnp.sum(e, axis=-1, keepdims=True)


def _norm_mod(x, g, sh, sc, w_router=None):
    b, l, d = x.shape
    tm = _t(l, 256)
    specs = [pl.BlockSpec((1, tm, d), lambda i, j: (i, j, 0)),
             pl.BlockSpec((1, d), lambda i, j: (0, 0)),
             pl.BlockSpec((1, 1, d), lambda i, j: (i, 0, 0)),
             pl.BlockSpec((1, 1, d), lambda i, j: (i, 0, 0))]
    h_spec = pl.BlockSpec((1, tm, d), lambda i, j: (i, j, 0))
    h_shape = jax.ShapeDtypeStruct((b, l, d), BF16)
    if w_router is None:
        return pl.pallas_call(_norm_mod_kernel, grid=(b, l // tm), in_specs=specs, out_specs=h_spec,
                              out_shape=h_shape, compiler_params=_cp(("arbitrary", "arbitrary")),
                              name="norm_mod")(x, g.reshape(1, d), sh, sc)
    wr = jnp.pad(w_router, ((0, 0), (0, LANES - w_router.shape[1])))
    return pl.pallas_call(
        _norm_router_kernel, grid=(b, l // tm),
        in_specs=specs + [pl.BlockSpec((d, LANES), lambda i, j: (0, 0))],
        out_specs=[h_spec, pl.BlockSpec((1, tm, LANES), lambda i, j: (i, j, 0))],
        out_shape=[h_shape, jax.ShapeDtypeStruct((b, l, LANES), F32)],
        compiler_params=_cp(("arbitrary", "arbitrary")), name="norm_router",
    )(x, g.reshape(1, d), sh, sc, wr)


def _final_norm_kernel(x_ref, g_ref, o_ref):
    x = x_ref[0]
    o_ref[0] = x * lax.rsqrt(jnp.mean(x * x, axis=-1, keepdims=True) + EPS) * g_ref[...]


def _final_norm(x, g):
    b, l, d = x.shape
    tm = _t(l, 256)
    return pl.pallas_call(
        _final_norm_kernel, grid=(b, l // tm),
        in_specs=[pl.BlockSpec((1, tm, d), lambda i, j: (i, j, 0)), pl.BlockSpec((1, d), lambda i, j: (0, 0))],
        out_specs=pl.BlockSpec((1, tm, d), lambda i, j: (i, j, 0)),
        out_shape=jax.ShapeDtypeStruct((b, l, d), F32),
        compiler_params=_cp(("arbitrary", "arbitrary")), name="final_norm",
    )(x, g.reshape(1, d))


def _mm_kernel(a_ref, w_ref, o_ref):
    o_ref[...] = _dot(a_ref[...].astype(BF16), w_ref[...].astype(BF16)).astype(o_ref.dtype)


def _matmul(a, w, out_dtype, tm, tn):
    m, k = a.shape
    n = w.shape[1]
    tm, tn = _t(m, tm), _t(n, tn)
    return pl.pallas_call(
        _mm_kernel, grid=(n // tn, m // tm),
        in_specs=[pl.BlockSpec((tm, k), lambda j, i: (i, 0)), pl.BlockSpec((k, tn), lambda j, i: (0, j))],
        out_specs=pl.BlockSpec((tm, tn), lambda j, i: (i, j)),
        out_shape=jax.ShapeDtypeStruct((m, n), out_dtype),
        compiler_params=_cp(("arbitrary", "arbitrary")), name="matmul",
    )(a, w)


def _mm_res_kernel(a_ref, w_ref, x_ref, g_ref, o_ref):
    o_ref[0] = x_ref[0] + g_ref[0] * _dot(a_ref[0], w_ref[...].astype(BF16))


def _matmul_residual(a, w, x, g):
    b, l, k = a.shape
    n = w.shape[1]
    tm, tn = _t(l, 1024), _t(n, 512)
    return pl.pallas_call(
        _mm_res_kernel, grid=(n // tn, b, l // tm),
        in_specs=[pl.BlockSpec((1, tm, k), lambda j, bi, i: (bi, i, 0)),
                  pl.BlockSpec((k, tn), lambda j, bi, i: (0, j)),
                  pl.BlockSpec((1, tm, tn), lambda j, bi, i: (bi, i, j)),
                  pl.BlockSpec((1, 1, tn), lambda j, bi, i: (bi, 0, j))],
        out_specs=pl.BlockSpec((1, tm, tn), lambda j, bi, i: (bi, i, j)),
        out_shape=jax.ShapeDtypeStruct((b, l, n), F32),
        compiler_params=_cp(("arbitrary", "arbitrary", "arbitrary")), name="matmul_residual",
    )(a, w, x, g)


ATT_K, ATT_V, ATT_KVA, ATT_KPE, ATT_KV_COLS = 0, 256, 512, 768, 896
ATT_Q, ATT_QA, ATT_Q_COLS = 0, 512, 896


def _rope(xh, tab_ref, half):
    lane = lax.broadcasted_iota(jnp.int32, xh.shape, 1)
    partner = jnp.where((lane % (2 * half)) < half, pltpu.roll(xh, LANES - half, 1), pltpu.roll(xh, half, 1))
    return xh * tab_ref[0] + partner * tab_ref[1]


def _rms(x, gain):
    return x * lax.rsqrt(jnp.mean(x * x, axis=-1, keepdims=True) + EPS) * gain


def _prep_kernel(*refs, use_pos):
    if use_pos:
        px_ref, pq_ref, tg_ref, tm_ref = refs[:4]
        refs = refs[4:]
    else:
        px_ref, pq_ref = refs[:2]
        refs = refs[2:]
    gq_ref, gk_ref, gqa_ref, gkva_ref, qb_ref, kvbk_ref, kvbv_ref, kg_ref, vg_ref, km_ref, vm_ref, qg_ref, qm_ref = refs
    p = px_ref[0]
    pq = pq_ref[0]
    hd = GQA_HEAD_DIM
    for h in range(GQA_KV_HEADS):
        kh = _rms(p[:, ATT_K + h * hd:ATT_K + (h + 1) * hd], gk_ref[...])
        if use_pos:
            kh = _rope(kh, tg_ref, 32)
        kg_ref[0, :, h * hd:(h + 1) * hd] = kh.astype(BF16)
    vg_ref[0] = p[:, ATT_V:ATT_V + GQA_KV_HEADS * hd].astype(BF16)
    for h in range(GQA_HEADS):
        qh = _rms(pq[:, ATT_Q + h * hd:ATT_Q + (h + 1) * hd], gq_ref[...])
        if use_pos:
            qh = _rope(qh, tg_ref, 32)
        qg_ref[0, :, h * hd:(h + 1) * hd] = qh.astype(BF16)
    kvn = _rms(p[:, ATT_KVA:ATT_KVA + MLA_KV_LORA], gkva_ref[...]).astype(BF16)
    knope = _dot(kvn, kvbk_ref[...].astype(BF16))
    vm_ref[0] = _dot(kvn, kvbv_ref[...].astype(BF16)).astype(BF16)
    kpe = p[:, ATT_KPE:ATT_KPE + LANES]
    lane = lax.broadcasted_iota(jnp.int32, kpe.shape, 1)
    kpe = pltpu.roll(jnp.where(lane < MLA_ROPE, kpe, 0.0), MLA_NOPE, 1)
    if use_pos:
        kpe = _rope(kpe, tm_ref, 8)
    qan = _rms(pq[:, ATT_QA:ATT_QA + MLA_Q_LORA], gqa_ref[...]).astype(BF16)
    qmf = _dot(qan, qb_ref[...].astype(BF16))
    for h in range(MLA_HEADS):
        km_ref[0, :, h * LANES:(h + 1) * LANES] = (knope[:, h * LANES:(h + 1) * LANES] + kpe).astype(BF16)
        qh = qmf[:, h * LANES:(h + 1) * LANES]
        if use_pos:
            qh = _rope(qh, tm_ref, 8)
        qm_ref[0, :, h * LANES:(h + 1) * LANES] = qh.astype(BF16)


def _attn_prep(px, pq, tabs, lw):
    b, l, _ = px.shape
    tm = _t(l, 256)
    use_pos = tabs is not None
    full = lambda shape: pl.BlockSpec(shape, lambda i, j: (0,) * len(shape))
    in_specs = [pl.BlockSpec((1, tm, ATT_KV_COLS), lambda i, j: (i, j, 0)),
                pl.BlockSpec((1, tm, ATT_Q_COLS), lambda i, j: (i, j, 0))]
    args = [px, pq]
    if use_pos:
        in_specs += [pl.BlockSpec((2, tm, LANES), lambda i, j: (0, j, 0))] * 2
        args += list(tabs)
    small = [lw['gqa_q_gain'].reshape(1, -1), lw['gqa_k_gain'].reshape(1, -1), lw['mla_q_a_gain'].reshape(1, -1),
             lw['mla_kv_a_gain'].reshape(1, -1), lw['q_b'], lw['kv_b_k'], lw['kv_b_v']]
    in_specs += [full(a.shape) for a in small]
    widths = [256, 256, 512, 512, 512, 512]
    return pl.pallas_call(
        functools.partial(_prep_kernel, use_pos=use_pos), grid=(b, l // tm), in_specs=in_specs,
        out_specs=[pl.BlockSpec((1, tm, w), lambda i, j: (i, j, 0)) for w in widths],
        out_shape=[jax.ShapeDtypeStruct((b, l, w), BF16) for w in widths],
        compiler_params=_cp(("arbitrary", "arbitrary")), name="attn_prep",
    )(*args, *small)


def _attn_kernel(*refs, n_src, scale, nsplit):
    q_ref, o_ref = refs[0], refs[-1]
    rows = q_ref.shape[1] // nsplit
    c2 = scale * math.log2(math.e)
    for c in range(nsplit):
        q = q_ref[0, c * rows:(c + 1) * rows, :]
        ss = [lax.dot_general(q, refs[1 + 2 * j][0], (((1,), (1,)), ((), ())), preferred_element_type=F32)
              for j in range(n_src)]
        m = jnp.max(ss[0], axis=-1, keepdims=True)
        for s in ss[1:]:
            m = jnp.maximum(m, jnp.max(s, axis=-1, keepdims=True))
        acc, den = None, None
        for j, s in enumerate(ss):
            p = jnp.exp2((s - m) * c2)
            d = jnp.sum(p, axis=-1, keepdims=True)
            a = _dot(p.astype(BF16), refs[2 + 2 * j][0])
            acc = a if acc is None else acc + a
            den = d if den is None else den + d
        o_ref[0, c * rows:(c + 1) * rows, :] = (acc / den).astype(o_ref.dtype)


def _attention(q, srcs, heads, kv_heads, dk):
    b, s, _ = q.shape
    r = heads // kv_heads
    tq = _t(s, 512)
    in_specs = [pl.BlockSpec((1, tq, LANES), lambda bi, h, i: (bi, i, h))]
    args = [q]
    for k, v in srcs:
        lk = k.shape[1]
        in_specs += [pl.BlockSpec((1, lk, LANES), lambda bi, h, i: (bi, 0, h // r))] * 2
        args += [k, v]
    return pl.pallas_call(
        functools.partial(_attn_kernel, n_src=len(srcs), scale=float(dk) ** -0.5, nsplit=2 if tq % 256 == 0 else 1),
        grid=(b, heads, s // tq), in_specs=in_specs,
        out_specs=pl.BlockSpec((1, tq, LANES), lambda bi, h, i: (bi, i, h)),
        out_shape=jax.ShapeDtypeStruct((b, s, heads * LANES), BF16),
        compiler_params=_cp(("arbitrary", "arbitrary", "arbitrary")), name="attention",
    )(*args)


def _dwconv_kernel(*refs, taps, glu, post_ln, chunk, sub):
    if glu:
        ap, ac, an, bp, bc, bn, w_ref, b_ref, lg_ref, lb_ref, o_ref, scr = refs
        load = lambda a, g: a[0] * jax.nn.sigmoid(g[0])
        prev, cur, nxt = load(ap, bp), load(ac, bc), load(an, bn)
    else:
        ap, ac, an, w_ref, b_ref, o_ref, scr = refs
        prev, cur, nxt = ap[0], ac[0], an[0]
    i = pl.program_id(2)
    last = pl.num_programs(2) - 1
    pad = CONV_PAD
    scr[0, 0:pad, :] = jnp.where(i > 0, prev, 0.0)
    scr[0, pad:pad + chunk, :] = cur
    scr[0, pad + chunk:2 * pad + chunk, :] = jnp.where(i < last, nxt, 0.0)
    lo = (taps - 1) // 2
    span = chunk + 2 * pad - SUBLANES
    for r in sorted({(pad - lo + j) % SUBLANES for j in range(taps)} - {0}):
        scr[r, 0:span, :] = scr[0, pl.ds(r, span), :]
    for c in range(chunk // sub):
        acc = None
        for j in range(taps):
            o = c * sub + pad - lo + j
            term = w_ref[j:j + 1, :] * scr[o % SUBLANES, pl.ds(o - o % SUBLANES, sub), :]
            acc = term if acc is None else acc + term
        y = acc + b_ref[...]
        if post_ln:
            mu = jnp.mean(y, axis=-1, keepdims=True)
            yc = y - mu
            var = jnp.mean(yc * yc, axis=-1, keepdims=True)
            y = yc * lax.rsqrt(var + EPS) * lg_ref[...] + lb_ref[...]
            y = y * jax.nn.sigmoid(y)
        o_ref[0, c * sub:(c + 1) * sub, :] = y.astype(o_ref.dtype)


def _dwconv(x, w, bias, ncol, *, glu=False, ln=None, out_dtype=F32):
    b, l, _ = x.shape
    taps = w.shape[0]
    cw = 512
    chunk = _t(l, 256)
    nblk = l // chunk
    per = chunk // CONV_PAD
    nsmall = l // CONV_PAD

    def views(coff):
        return [pl.BlockSpec((1, CONV_PAD, cw), lambda bi, c, i: (bi, jnp.maximum(i * per - 1, 0), c + coff)),
                pl.BlockSpec((1, chunk, cw), lambda bi, c, i: (bi, i, c + coff)),
                pl.BlockSpec((1, CONV_PAD, cw), lambda bi, c, i: (bi, jnp.minimum((i + 1) * per, nsmall - 1), c + coff))]

    in_specs = views(0)
    args = [x, x, x]
    if glu:
        in_specs += views(ncol)
        args += [x, x, x]
    vec = lambda: pl.BlockSpec((1, cw), lambda bi, c, i: (0, c))
    in_specs += [pl.BlockSpec((taps, cw), lambda bi, c, i: (0, c)), vec()]
    args += [w, bias.reshape(1, -1)]
    if ln is not None:
        in_specs += [vec(), vec()]
        args += [ln[0].reshape(1, -1), ln[1].reshape(1, -1)]
    return pl.pallas_call(
        functools.partial(_dwconv_kernel, taps=taps, glu=glu, post_ln=ln is not None, chunk=chunk, sub=32),
        grid=(b, ncol, nblk), in_specs=in_specs,
        out_specs=pl.BlockSpec((1, chunk, cw), lambda bi, c, i: (bi, i, c)),
        out_shape=jax.ShapeDtypeStruct((b, l, ncol * cw), out_dtype),
        scratch_shapes=[pltpu.VMEM((SUBLANES, chunk + 2 * CONV_PAD, cw), F32)],
        compiler_params=_cp(("arbitrary", "arbitrary", "arbitrary")), name="dwconv",
    )(*args)


def _dft_tables(l):
    n2 = 2 * l
    k = jnp.arange(l, dtype=jnp.int32)
    m = (k[:, None] * k[None, :]) & (n2 - 1)
    ang = m.astype(F32) * (2.0 * math.pi / n2)
    c = jnp.cos(ang)
    s = -jnp.sin(ang)
    nyq = jnp.where(k % 2 == 0, 1.0, -1.0).astype(F32)
    s = jnp.where((k == 0)[:, None], nyq[None, :], s)
    p = jnp.stack([c, s]).astype(BF16)
    return p, jnp.swapaxes(p, 1, 2)


def _hy_feats(l):
    pos = jnp.arange(l, dtype=F32)
    t01 = pos / (l - 1)
    bands = jnp.linspace(1e-4, HY_BANDS - 1, HY_BANDS, dtype=F32)
    ang = (2.0 * math.pi / l) * pos[:, None] * bands[None, :]
    feats = jnp.concatenate([t01[:, None], jnp.cos(ang), -jnp.sin(ang)], axis=-1)
    return jnp.pad(feats, ((0, 0), (0, LANES - feats.shape[1])))


def _hyfilt_kernel(f_ref, w1_ref, b1_ref, fr_ref, w2_ref, b2_ref, w3_ref, lr_ref, o_ref, *, tl):
    feats = f_ref[...]
    fr = fr_ref[...]
    hdot = lambda a, b: jnp.dot(a, b, preferred_element_type=F32, precision=HI)
    h = jnp.sin(fr * (hdot(feats, w1_ref[...]) + b1_ref[...]))
    h = jnp.sin(fr * (hdot(h, w2_ref[...]) + b2_ref[...]))
    h = hdot(h, w3_ref[...])
    h = h * jnp.exp(-feats[:, 0:1] * jnp.exp(lr_ref[...]))
    row = pl.program_id(0) * tl + lax.broadcasted_iota(jnp.int32, h.shape, 0)
    col = lax.broadcasted_iota(jnp.int32, h.shape, 1)
    is_bwd = ((col // HY_WIDTH) % 2) == 1
    o_ref[...] = jnp.where((row == 0) & is_bwd, 0.0, h).astype(o_ref.dtype)


def _hy_filters(l, lw):
    feats = _hy_feats(l)
    hid = HY_FILTER_HIDDEN
    padc = lambda a: jnp.pad(a, ((0, 0), (0, LANES - a.shape[1])))
    w1 = jnp.pad(lw['hf_w1'], ((0, LANES - lw['hf_w1'].shape[0]), (0, LANES - hid)))
    w2 = jnp.pad(lw['hf_w2'], ((0, LANES - hid), (0, LANES - hid)))
    w3 = jnp.pad(lw['hf_w3'], ((0, LANES - hid), (0, 0)))
    b1, b2, fr = padc(lw['hf_b1'].reshape(1, -1)), padc(lw['hf_b2'].reshape(1, -1)), padc(lw['hf_freq'].reshape(1, -1))
    lr = lw['hf_log_rate'].reshape(1, -1)
    nc = w3.shape[1]
    tl = _t(l, 256)
    full = lambda a: pl.BlockSpec(a.shape, lambda i: (0, 0))
    return pl.pallas_call(
        functools.partial(_hyfilt_kernel, tl=tl), grid=(l // tl,),
        in_specs=[pl.BlockSpec((tl, LANES), lambda i: (i, 0)), full(w1), full(b1), full(fr), full(w2), full(b2),
                  full(w3), full(lr)],
        out_specs=pl.BlockSpec((tl, nc), lambda i: (i, 0)),
        out_shape=jax.ShapeDtypeStruct((l, nc), BF16),
        compiler_params=_cp(("arbitrary",)), name="hy_filter",
    )(feats, w1, b1, fr, w2, b2, w3, lr)


def _hyspec_kernel(p_ref, h_ref, o_ref, *, tk, scale):
    hm = h_ref[...]
    ar = _dot(p_ref[0], hm)
    ai = _dot(p_ref[1], hm)
    c = HY_WIDTH
    row0 = (pl.program_id(0) * tk + lax.broadcasted_iota(jnp.int32, (tk, c), 0)) == 0
    s = jnp.where(row0, 0.5 * scale, scale)
    for n in range(HY_ORDER):
        o = 2 * n * c
        o_ref[n, 0] = (ar[:, o:o + c] + ar[:, o + c:o + 2 * c]) * s
        fi, bi = ai[:, o:o + c], ai[:, o + c:o + 2 * c]
        o_ref[n, 1] = jnp.where(row0, fi + bi, fi - bi) * s


def _hy_spectrum(p, filt):
    l = p.shape[1]
    tk = _t(l, 256)
    return pl.pallas_call(
        functools.partial(_hyspec_kernel, tk=tk, scale=1.0 / l), grid=(l // tk,),
        in_specs=[pl.BlockSpec((2, tk, l), lambda i: (0, i, 0)), pl.BlockSpec(filt.shape, lambda i: (0, 0))],
        out_specs=pl.BlockSpec((HY_ORDER, 2, tk, HY_WIDTH), lambda i: (0, 0, i, 0)),
        out_shape=jax.ShapeDtypeStruct((HY_ORDER, 2, l, HY_WIDTH), F32),
        compiler_params=_cp(("arbitrary",)), name="hy_spectrum",
    )(p, filt)


def _hyfwd_kernel(p_ref, z_ref, k_ref, y_ref, *, tk):
    z = z_ref[0].astype(BF16)
    xr = _dot(p_ref[0], z)
    xi = _dot(p_ref[1], z)
    kr, ki = k_ref[0, 0], k_ref[0, 1]
    row0 = (pl.program_id(0) * tk + lax.broadcasted_iota(jnp.int32, xr.shape, 0)) == 0
    xiki = xi * ki
    y_ref[0, 0] = (xr * kr - jnp.where(row0, 0.0, xiki)).astype(y_ref.dtype)
    y_ref[0, 1] = jnp.where(row0, xiki, xr * ki + xi * kr).astype(y_ref.dtype)


def _hy_forward(p, z, zcol, kf, order):
    b, l, _ = z.shape
    c = HY_WIDTH
    tk = _t(l, 1024)
    return pl.pallas_call(
        functools.partial(_hyfwd_kernel, tk=tk), grid=(l // tk, b),
        in_specs=[pl.BlockSpec((2, tk, l), lambda i, bi: (0, i, 0)),
                  pl.BlockSpec((1, l, c), lambda i, bi: (bi, 0, zcol)),
                  pl.BlockSpec((1, 2, tk, c), lambda i, bi: (order, 0, i, 0))],
        out_specs=pl.BlockSpec((1, 2, tk, c), lambda i, bi: (bi, 0, i, 0)),
        out_shape=jax.ShapeDtypeStruct((b, 2, l, c), BF16),
        compiler_params=_cp(("arbitrary", "arbitrary")), name="hy_forward",
    )(p, z, kf)


def _hyinv_kernel(pt_ref, y_ref, g_ref, z_ref, bias_ref, o_ref):
    conv = _dot(pt_ref[0], y_ref[0, 0]) + _dot(pt_ref[1], y_ref[0, 1])
    o_ref[0] = (g_ref[0] * (conv + bias_ref[...] * z_ref[0])).astype(o_ref.dtype)


def _hy_inverse(pt, y, u, gcol, z, zcol, bias, out_dtype):
    b, _, l, c = y.shape
    tn = _t(l, 1024)
    return pl.pallas_call(
        _hyinv_kernel, grid=(l // tn, b),
        in_specs=[pl.BlockSpec((2, tn, l), lambda i, bi: (0, i, 0)),
                  pl.BlockSpec((1, 2, l, c), lambda i, bi: (bi, 0, 0, 0)),
                  pl.BlockSpec((1, tn, c), lambda i, bi: (bi, i, gcol)),
                  pl.BlockSpec((1, tn, c), lambda i, bi: (bi, i, zcol)),
                  pl.BlockSpec((1, c), lambda i, bi: (0, 0))],
        out_specs=pl.BlockSpec((1, tn, c), lambda i, bi: (bi, i, 0)),
        out_shape=jax.ShapeDtypeStruct((b, l, c), out_dtype),
        compiler_params=_cp(("arbitrary", "arbitrary")), name="hy_inverse",
    )(pt, y, u, z, bias.reshape(1, c))


def _hyena(phy, lw, dft):
    p, pt, kf = dft
    u = _dwconv(phy, lw['hy_conv_w'], lw['hy_conv_b'], HY_ORDER + 1)
    y = _hy_forward(p, u, 0, kf, 0)
    z = _hy_inverse(pt, y, u, 1, u, 0, lw['hy_bias'][0], F32)
    y = _hy_forward(p, z, 0, kf, 1)
    return _hy_inverse(pt, y, u, 2, z, 0, lw['hy_bias'][1], BF16)


def _merge_kernel(hx_ref, g0, g1, g2, g3, b0, b1, b2, b3, wbr_ref, o_ref):
    hx = hx_ref[...]
    acc = None
    for n, (wg, br) in enumerate(((g0, b0), (g1, b1), (g2, b2), (g3, b3))):
        gate = jax.nn.sigmoid(_dot(hx, wg[...]))
        term = gate * _dot(br[...], wbr_ref[n].astype(BF16))
        acc = term if acc is None else acc + term
    o_ref[...] = acc.astype(o_ref.dtype)


def _merge(hx, w_gate, branches, w_br):
    m, d = hx.shape
    tm, tn = _t(m, 512), _t(d, 512)
    nj = d // tn
    bw = BRANCH_WIDTH
    in_specs = [pl.BlockSpec((tm, d), lambda j, i: (i, 0))]
    in_specs += [pl.BlockSpec((d, tn), functools.partial(lambda j, i, n: (0, n * nj + j), n=n)) for n in range(N_BRANCH)]
    in_specs += [pl.BlockSpec((tm, bw), lambda j, i: (i, 0))] * N_BRANCH
    in_specs += [pl.BlockSpec((N_BRANCH, bw, tn), lambda j, i: (0, 0, j))]
    return pl.pallas_call(
        _merge_kernel, grid=(nj, m // tm), in_specs=in_specs,
        out_specs=pl.BlockSpec((tm, tn), lambda j, i: (i, j)),
        out_shape=jax.ShapeDtypeStruct((m, d), BF16),
        compiler_params=_cp(("arbitrary", "arbitrary")), name="merge",
    )(hx, w_gate, w_gate, w_gate, w_gate, *branches, w_br)


def _topk_kernel(aff_ref, tri_ref, sp_ref, spt_ref, g_ref, *, cap):
    afft = aff_ref[0].T[:N_EXPERTS, :]
    bits = lax.bitcast_convert_type(afft, jnp.int32)

    def body(i, prefix):
        cand = prefix | jnp.left_shift(jnp.int32(1), 30 - i)
        cnt = jnp.sum((bits >= cand).astype(F32), axis=1, keepdims=True)
        return jnp.where(cnt >= cap, cand, prefix)

    thr = lax.fori_loop(0, 31, body, jnp.zeros((N_EXPERTS, 1), jnp.int32))
    gt = bits > thr
    eq = bits == thr
    need = cap - jnp.sum(gt.astype(F32), axis=1, keepdims=True)
    tri = tri_ref[...]
    rank_eq = _dot(eq.astype(BF16), tri)
    sel = gt | (eq & (rank_eq <= need))
    pos = _dot(sel.astype(BF16), tri) - 1.0
    sp = jnp.where(sel, pos, -1.0)
    sp_ref[0] = sp.astype(jnp.int32)
    n = sp.shape[1]
    padded = jnp.concatenate([sp, jnp.full((LANES - N_EXPERTS, n), -1.0, F32)], axis=0)
    spt_ref[0] = padded.T
    icap = g_ref.shape[2]
    slot = lax.broadcasted_iota(jnp.int32, (icap, n), 0).astype(F32)
    for e in range(N_EXPERTS):
        hit = slot == sp[e:e + 1, :]
        g_ref[0, e] = jnp.sum(jnp.where(hit, afft[e:e + 1, :], 0.0), axis=1, keepdims=True)


def _topk(aff, cap):
    b, n, _ = aff.shape
    t = jnp.arange(n, dtype=jnp.int32)
    tri = (t[:, None] <= t[None, :]).astype(BF16)
    return pl.pallas_call(
        functools.partial(_topk_kernel, cap=float(cap)), grid=(b,),
        in_specs=[pl.BlockSpec((1, n, LANES), lambda i: (i, 0, 0)), pl.BlockSpec((n, n), lambda i: (0, 0))],
        out_specs=[pl.BlockSpec((1, N_EXPERTS, n), lambda i: (i, 0, 0)), pl.BlockSpec((1, n, LANES), lambda i: (i, 0, 0)),
                   pl.BlockSpec((1, N_EXPERTS, cap, 1), lambda i: (i, 0, 0, 0))],
        out_shape=[jax.ShapeDtypeStruct((b, N_EXPERTS, n), jnp.int32), jax.ShapeDtypeStruct((b, n, LANES), F32),
                   jax.ShapeDtypeStruct((b, N_EXPERTS, cap, 1), F32)],
        compiler_params=_cp(("arbitrary",)), name="topk",
    )(aff, tri)


def _gather_kernel(sp_ref, h_ref, o_ref, *, cap):
    sp = sp_ref[0, 0]
    slot = lax.broadcasted_iota(jnp.int32, (cap, sp.shape[1]), 0)
    onehot = (slot == sp).astype(BF16)
    o_ref[0] = _dot(onehot, h_ref[0]).astype(o_ref.dtype)


def _gather(sp, h, cap):
    b, n, d = h.shape
    return pl.pallas_call(
        functools.partial(_gather_kernel, cap=cap), grid=(b, N_EXPERTS),
        in_specs=[pl.BlockSpec((1, 1, 1, n), lambda bi, e: (bi, e, 0, 0)), pl.BlockSpec((1, n, d), lambda bi, e: (bi, 0, 0))],
        out_specs=pl.BlockSpec((1, cap, d), lambda bi, e: (e, bi, 0)),
        out_shape=jax.ShapeDtypeStruct((N_EXPERTS, b * cap, d), BF16),
        compiler_params=_cp(("arbitrary", "arbitrary")), name="moe_gather",
    )(sp.reshape(b, N_EXPERTS, 1, n), h)


def _ffn_kernel(*refs, n_x):
    xs = refs[:n_x]
    gs = refs[n_x:2 * n_x]
    w1_ref, w3_ref, w2_ref = refs[2 * n_x:2 * n_x + 3]
    ys = refs[2 * n_x + 3:3 * n_x + 3]
    accs = refs[3 * n_x + 3:]
    f = pl.program_id(1)
    last = pl.num_programs(1) - 1
    w1, w3, w2 = w1_ref[0].astype(BF16), w3_ref[0].astype(BF16), w2_ref[0].astype(BF16)
    for x_ref, g_ref, y_ref, acc in zip(xs, gs, ys, accs):
        rows = x_ref.shape[1]
        grp = 512 if rows % 512 == 0 else rows
        parts = []
        for r in range(rows // grp):
            x = x_ref[0, r * grp:(r + 1) * grp, :]
            a = _dot(x, w1)
            hid = (a * jax.nn.sigmoid(a) * _dot(x, w3)).astype(BF16)
            parts.append(_dot(hid, w2))

        @pl.when(f == 0)
        def _():
            for r, part in enumerate(parts):
                acc[r * grp:(r + 1) * grp, :] = part

        @pl.when(f > 0)
        def _():
            for r, part in enumerate(parts):
                acc[r * grp:(r + 1) * grp, :] += part

        @pl.when(f == last)
        def _():
            nb, _, cap, _ = y_ref.shape
            for bi in range(nb):
                y_ref[bi, 0] = (acc[bi * cap:(bi + 1) * cap, :] * g_ref[bi, 0]).astype(y_ref.dtype)


def _expert_ffn(xs, gs, w1, w3, w2):
    e, d, ff = w1.shape
    tf = _t(ff, 256)
    in_specs = [pl.BlockSpec((1, x.shape[1], d), lambda ei, f: (ei, 0, 0)) for x in xs]
    in_specs += [pl.BlockSpec((g.shape[0], 1, g.shape[2], 1), lambda ei, f: (0, ei, 0, 0)) for g in gs]
    in_specs += [pl.BlockSpec((1, d, tf), lambda ei, f: (ei, 0, f))] * 2 + [pl.BlockSpec((1, tf, d), lambda ei, f: (ei, f, 0))]
    return pl.pallas_call(
        functools.partial(_ffn_kernel, n_x=len(xs)), grid=(e, ff // tf), in_specs=in_specs,
        out_specs=[pl.BlockSpec((g.shape[0], 1, g.shape[2], d), lambda ei, f: (0, ei, 0, 0)) for g in gs],
        out_shape=[jax.ShapeDtypeStruct((g.shape[0], e, g.shape[2], d), BF16) for g in gs],
        scratch_shapes=[pltpu.VMEM((x.shape[1], d), F32) for x in xs],
        compiler_params=_cp(("arbitrary", "arbitrary")), name="expert_ffn",
    )(*xs, *gs, w1, w3, w2)


def _scatter_kernel(spt_ref, y_ref, x_ref, g_ref, o_ref, acc_ref, *, cap, kc):
    k = pl.program_id(2)
    tt = spt_ref.shape[1]
    blk = y_ref.shape[1]
    spt = spt_ref[0].astype(BF16)
    term = None
    for j in range(blk // kc):
        first = (k * blk + j * kc) // cap
        col = lax.broadcasted_iota(jnp.int32, (LANES, kc), 1)
        row = lax.broadcasted_iota(jnp.int32, (LANES, kc), 0)
        rep = (row == first + col // cap).astype(BF16)
        mine = _dot(spt, rep)
        slot = (lax.broadcasted_iota(jnp.int32, (tt, kc), 1) % cap).astype(F32)
        part = _dot((mine == slot).astype(BF16), y_ref[0, j * kc:(j + 1) * kc, :])
        term = part if term is None else term + part

    @pl.when(k == 0)
    def _():
        acc_ref[...] = term

    @pl.when(k > 0)
    def _():
        acc_ref[...] += term

    @pl.when(k == pl.num_programs(2) - 1)
    def _():
        o_ref[0] = x_ref[0] + g_ref[0] * acc_ref[...]


def _scatter(spt, y, x, g, cap):
    b, n, d = x.shape
    tt = _t(n, 512)
    kc = _t(N_EXPERTS * cap, 1024)
    blk = _t(N_EXPERTS * cap, 2 * kc)
    assert kc % cap == 0 and cap <= 256
    return pl.pallas_call(
        functools.partial(_scatter_kernel, cap=cap, kc=kc), grid=(b, n // tt, N_EXPERTS * cap // blk),
        in_specs=[pl.BlockSpec((1, tt, LANES), lambda bi, i, k: (bi, i, 0)),
                  pl.BlockSpec((1, blk, d), lambda bi, i, k: (bi, k, 0)),
                  pl.BlockSpec((1, tt, d), lambda bi, i, k: (bi, i, 0)),
                  pl.BlockSpec((1, 1, d), lambda bi, i, k: (bi, 0, 0))],
        out_specs=pl.BlockSpec((1, tt, d), lambda bi, i, k: (bi, i, 0)),
        out_shape=jax.ShapeDtypeStruct((b, n, d), F32),
        scratch_shapes=[pltpu.VMEM((tt, d), F32)],
        compiler_params=_cp(("arbitrary", "arbitrary", "arbitrary")), name="moe_scatter",
    )(spt, y, x, g)


def _shift_kernel(a_ref, b_ref, o_ref, *, shift, wt):
    rows = a_ref.shape[1]
    lane = lax.broadcasted_iota(jnp.int32, (rows, LANES), 1)
    nt = wt // LANES
    for u in range(nt):
        a = a_ref[0, :, u * LANES:(u + 1) * LANES]
        b = a_ref[0, :, (u + 1) * LANES:(u + 2) * LANES] if u + 1 < nt else b_ref[0]
        keep = LANES - shift
        o_ref[:, u * LANES:(u + 1) * LANES] = jnp.where(
            lane < keep, pltpu.roll(a, keep, 1), pltpu.roll(b, keep, 1)).astype(o_ref.dtype)


def _shift_cols(w, layer, off, width, wt):
    _, rows, cols = w.shape
    shift = off % LANES
    base = off - shift
    assert shift and base % wt == 0 and wt % LANES == 0
    nt = -(-width // wt)
    last = (cols - 1) // LANES
    return pl.pallas_call(
        functools.partial(_shift_kernel, shift=shift, wt=wt), grid=(nt,),
        in_specs=[pl.BlockSpec((1, rows, wt), lambda t: (layer, 0, base // wt + t)),
                  pl.BlockSpec((1, rows, LANES), lambda t: (layer, 0, jnp.minimum((base + (t + 1) * wt) // LANES, last)))],
        out_specs=pl.BlockSpec((rows, wt), lambda t: (0, t)),
        out_shape=jax.ShapeDtypeStruct((rows, nt * wt), BF16),
        compiler_params=_cp(("arbitrary",)), name="shift_cols",
    )(w, w)


def _rope_tables(s):
    rows = jnp.repeat(jnp.arange(s // GRID_W, dtype=jnp.int32), GRID_W).astype(F32)
    cols = jnp.tile(jnp.arange(GRID_W, dtype=jnp.int32), s // GRID_W).astype(F32)

    def half_tables(m):
        inv = ROPE_BASE ** (-jnp.arange(0, m, 2, dtype=F32) / m)
        out = []
        for pos in (rows, cols):
            ang = pos[:, None] * inv[None, :]
            c, sn = jnp.cos(ang), jnp.sin(ang)
            out.append((jnp.concatenate([c, c], axis=1), jnp.concatenate([-sn, sn], axis=1)))
        return (jnp.concatenate([out[0][0], out[1][0]], axis=1), jnp.concatenate([out[0][1], out[1][1]], axis=1))

    cg, sg = half_tables(GQA_HEAD_DIM // 2)
    cm, sm = half_tables(MLA_ROPE // 2)
    padm = lambda a, fill: jnp.concatenate(
        [jnp.full((s, MLA_NOPE), fill, F32), a, jnp.full((s, LANES - MLA_NOPE - MLA_ROPE), fill, F32)], axis=1)
    return jnp.stack([cg, sg]), jnp.stack([padm(cm, 1.0), padm(sm, 0.0)])


def _layer_weights(i, w):
    d = w['w_in'].shape[1]
    wi = w['w_in'][i]
    kv_cols = 2 * GQA_KV_HEADS * GQA_HEAD_DIM + MLA_KV_LORA + MLA_ROPE
    q_cols = GQA_HEADS * GQA_HEAD_DIM + MLA_Q_LORA
    hy_cols = (HY_ORDER + 1) * HY_WIDTH
    o_q = kv_cols
    o_hy = o_q + q_cols
    o_cv = o_hy + hy_cols
    o_g = o_cv + 2 * CV_WIDTH
    assert o_q + LANES - MLA_ROPE == ATT_KV_COLS and q_cols == ATT_Q_COLS
    w_kv = wi[:, :ATT_KV_COLS].astype(BF16)
    w_q = _shift_cols(w['w_in'], i, o_q, q_cols, LANES)
    w_hy = _shift_cols(w['w_in'], i, o_hy, hy_cols, LANES)
    w_cv = _shift_cols(w['w_in'], i, o_cv, 2 * CV_WIDTH, LANES)
    w_gate = _shift_cols(w['w_in'], i, o_g, N_BRANCH * d, 3 * LANES)
    qb =w['mla_q_b'][i].reshape(MLA_Q_LORA, MLA_HEADS, MLA_NOPE + MLA_ROPE)
    qb = jnp.pad(qb, ((0, 0), (0, 0), (0, LANES - MLA_NOPE - MLA_ROPE))).reshape(MLA_Q_LORA, MLA_HEADS * LANES)
    kvb = w['mla_kv_b'][i].reshape(MLA_KV_LORA, MLA_HEADS, MLA_NOPE + MLA_V)
    kvb_k = jnp.pad(kvb[:, :, :MLA_NOPE], ((0, 0), (0, 0), (0, LANES - MLA_NOPE))).reshape(MLA_KV_LORA, MLA_HEADS * LANES)
    kvb_v = kvb[:, :, MLA_NOPE:].reshape(MLA_KV_LORA, MLA_HEADS * MLA_V)
    lw = {k: w[k][i] for k in ('gqa_q_gain', 'gqa_k_gain', 'mla_q_a_gain', 'mla_kv_a_gain', 'hy_conv_w', 'hy_conv_b',
                               'hf_w1', 'hf_b1', 'hf_w2', 'hf_b2', 'hf_w3', 'hf_freq', 'hf_log_rate', 'hy_bias',
                               'cv_w', 'cv_b', 'cv_ln_g', 'cv_ln_b', 'w_router', 'w1', 'w3', 'w2', 'g_mix', 'g_ffn')}
    lw.update(w_kv=w_kv, w_q=w_q, w_hy=w_hy, w_cv=w_cv, w_gate=w_gate, q_b=qb, kv_b_k=kvb_k, kv_b_v=kvb_v,
              w_br=w['w_br'][i].astype(BF16), w_out=w['w_out'][i].astype(BF16))
    return lw


def _mixer_branches(hx, keys_extra, tabs, lw, dft):
    b, l, d = hx.shape
    hx2 = hx.reshape(b * l, d)
    px = _matmul(hx2, lw['w_kv'], F32, 1024, 896).reshape(b, l, -1)
    pq = _matmul(hx2, lw['w_q'], F32, 1024, 896).reshape(b, l, -1)
    kg, vg, km, vm, qg, qm = _attn_prep(px, pq, tabs, lw)
    own = (kg, vg, km, vm)
    if dft is None:
        return None, own
    ek = keys_extra
    o_gqa = _attention(qg, [(kg, vg)] + ([(ek[0], ek[1])] if ek else []), GQA_HEADS, GQA_KV_HEADS, GQA_HEAD_DIM)
    o_mla = _attention(qm, [(km, vm)] + ([(ek[2], ek[3])] if ek else []), MLA_HEADS, MLA_HEADS, MLA_NOPE + MLA_ROPE)
    phy = _matmul(hx2, lw['w_hy'], F32, 1024, 768).reshape(b, l, -1)
    o_hy = _hyena(phy, lw, dft)
    pcv = _matmul(hx2, lw['w_cv'], F32, 1024, 1024).reshape(b, l, -1)
    o_cv = _dwconv(pcv, lw['cv_w'], lw['cv_b'], 1, glu=True, ln=(lw['cv_ln_g'], lw['cv_ln_b']), out_dtype=BF16)
    flat = lambda a: a.reshape(b * l, -1)
    return [flat(o_hy), flat(o_gqa), flat(o_mla), flat(o_cv)], own


def kernel(x, c, ctx, c_ctx, w_mod, b_mod, g_mix, w_in, gqa_q_gain, gqa_k_gain, mla_q_a_gain, mla_q_b, mla_kv_a_gain, mla_kv_b, hy_conv_w, hy_conv_b, hf_w1, hf_b1, hf_w2, hf_b2, hf_w3, hf_freq, hf_log_rate, hy_bias, cv_w, cv_b, cv_ln_g, cv_ln_b, w_br, w_out, g_ffn, w_router, w1, w3, w2, g_final):
    w = dict(g_mix=g_mix, w_in=w_in, gqa_q_gain=gqa_q_gain, gqa_k_gain=gqa_k_gain, mla_q_a_gain=mla_q_a_gain,
             mla_q_b=mla_q_b, mla_kv_a_gain=mla_kv_a_gain, mla_kv_b=mla_kv_b, hy_conv_w=hy_conv_w, hy_conv_b=hy_conv_b,
             hf_w1=hf_w1, hf_b1=hf_b1, hf_w2=hf_w2, hf_b2=hf_b2, hf_w3=hf_w3, hf_freq=hf_freq, hf_log_rate=hf_log_rate,
             hy_bias=hy_bias, cv_w=cv_w, cv_b=cv_b, cv_ln_g=cv_ln_g, cv_ln_b=cv_ln_b, w_br=w_br, w_out=w_out,
             g_ffn=g_ffn, w_router=w_router, w1=w1, w3=w3, w2=w2)
    bsz, s, d = x.shape
    lc = ctx.shape[1]
    depth = w_mod.shape[0]
    assert bsz < 8
    cvec = jnp.concatenate([c, c_ctx[None, :], jnp.zeros((8 - bsz - 1, d), F32)], axis=0)
    mod = _mod_all(cvec, w_mod, b_mod)
    tabs = _rope_tables(s)
    p_x, pt_x = _dft_tables(s)
    p_c, pt_c = _dft_tables(lc)
    cap_x = CAPACITY_FACTOR * s // N_EXPERTS
    cap_c = CAPACITY_FACTOR * lc // N_EXPERTS
    xc = ctx
    for i in range(depth):
        last = i == depth - 1
        lw = _layer_weights(i, w)
        mx = [mod[i, :bsz, j * d:(j + 1) * d].reshape(bsz, 1, d) for j in range(6)]
        mc = [jnp.broadcast_to(mod[i, bsz:bsz + 1, j * d:(j + 1) * d].reshape(1, 1, d), (bsz, 1, d)) for j in range(6)]

        hc = _norm_mod(xc, lw['g_mix'], mc[0], mc[1])
        dft_c = None if last else (p_c, pt_c, _hy_spectrum(p_c, _hy_filters(lc, lw)))
        br_c, keys_c = _mixer_branches(hc, None, None, lw, dft_c)
        hx = _norm_mod(x, lw['g_mix'], mx[0], mx[1])
        dft_x = (p_x, pt_x, _hy_spectrum(p_x, _hy_filters(s, lw)))
        br_x, _ = _mixer_branches(hx, keys_c, tabs, lw, dft_x)

        merged = _merge(hx.reshape(bsz * s, d), lw['w_gate'], br_x, lw['w_br']).reshape(bsz, s, d)
        x = _matmul_residual(merged, lw['w_out'], x, mx[2])
        hf, aff = _norm_mod(x, lw['g_ffn'], mx[3], mx[4], lw['w_router'])
        sp, spt, gslot = _topk(aff, cap_x)
        xs, gs = [_gather(sp, hf, cap_x)], [gslot]
        if not last:
            merged_c = _merge(hc.reshape(bsz * lc, d), lw['w_gate'], br_c, lw['w_br']).reshape(bsz, lc, d)
            xc = _matmul_residual(merged_c, lw['w_out'], xc, mc[2])
            hfc, aff_c = _norm_mod(xc, lw['g_ffn'], mc[3], mc[4], lw['w_router'])
            sp_c, spt_c, gslot_c = _topk(aff_c, cap_c)
            xs.append(_gather(sp_c, hfc, cap_c))
            gs.append(gslot_c)
        ys = _expert_ffn(xs, gs, lw['w1'], lw['w3'], lw['w2'])
        x = _scatter(spt, ys[0].reshape(bsz, N_EXPERTS * cap_x, d), x, mx[5], cap_x)
        if not last:
            xc = _scatter(spt_c, ys[1].reshape(bsz, N_EXPERTS * cap_c, d), xc, mc[5], cap_c)
    return _final_norm(x, g_final)
```

```python
import functools
import math

import numpy as np
import jax
import jax.numpy as jnp
from jax import lax
from jax.experimental import pallas as pl
from jax.experimental.pallas import tpu as pltpu

F32 = jnp.float32
BF16 = jnp.bfloat16
HI = lax.Precision.HIGHEST

GRID_W = 64
EPS = 1e-6
ROPE_BASE = 10000.0
GQA_HEADS, GQA_KV_HEADS, GQA_HEAD_DIM = 4, 2, 128
MLA_HEADS, MLA_Q_LORA, MLA_KV_LORA, MLA_NOPE, MLA_ROPE, MLA_V = 4, 384, 256, 64, 32, 128
HY_WIDTH, HY_ORDER, HY_BANDS, HY_FILTER_HIDDEN = 512, 2, 16, 64
CV_WIDTH, CV_KERNEL = 512, 31
N_BRANCH, BRANCH_WIDTH = 4, 512
N_EXPERTS, EXPERT_FF, CAPACITY_FACTOR = 16, 1024, 2
LANES = 128
SUBLANES = 8
BF16_ROWS = 16
CONV_PAD = 16
VMEM_LIMIT = 56 * 1024 * 1024


def _cp(sem, vmem=VMEM_LIMIT):
    return pltpu.CompilerParams(dimension_semantics=sem, vmem_limit_bytes=vmem)


def _t(n, pref):
    return pref if n % pref == 0 else n


def _dot(a, b):
    return jnp.dot(a, b, preferred_element_type=F32)


def _mod_kernel(c_ref, w_ref, b_ref, o_ref):
    c = c_ref[...]
    a = (c * jax.nn.sigmoid(c)).astype(BF16)
    o_ref[0] = _dot(a, w_ref[0].astype(BF16)) + b_ref[0]


def _mod_all(cvec, w_mod, b_mod):
    depth, d, n6 = w_mod.shape
    tn = _t(n6, 1024)
    return pl.pallas_call(
        _mod_kernel, grid=(depth, n6 // tn),
        in_specs=[pl.BlockSpec((8, d), lambda l, j: (0, 0)),
                  pl.BlockSpec((1, d, tn), lambda l, j: (l, 0, j)),
                  pl.BlockSpec((1, 1, tn), lambda l, j: (l, 0, j))],
        out_specs=pl.BlockSpec((1, 8, tn), lambda l, j: (l, 0, j)),
        out_shape=jax.ShapeDtypeStruct((depth, 8, n6), F32),
        compiler_params=_cp(("arbitrary", "arbitrary")), name="mod",
    )(cvec, w_mod, b_mod.reshape(depth, 1, n6))


def _norm_h(x_ref, g_ref, sh_ref, sc_ref):
    x = x_ref[0]
    y = x * lax.rsqrt(jnp.mean(x * x, axis=-1, keepdims=True) + EPS)
    return (y * g_ref[...]) * (1.0 + sc_ref[0]) + sh_ref[0]


def _norm_mod_kernel(x_ref, g_ref, sh_ref, sc_ref, o_ref):
    o_ref[0] = _norm_h(x_ref, g_ref, sh_ref, sc_ref).astype(o_ref.dtype)


def _norm_router_kernel(x_ref, g_ref, sh_ref, sc_ref, wr_ref, o_ref, aff_ref):
    h = _norm_h(x_ref, g_ref, sh_ref, sc_ref)
    o_ref[0] = h.astype(o_ref.dtype)
    logits = jnp.dot(h, wr_ref[...], preferred_element_type=F32, precision=HI)
    lane = lax.broadcasted_iota(jnp.int32, logits.shape, 1)
    logits = jnp.where(lane < N_EXPERTS, logits, -1e30)
    e = jnp.exp(logits - jnp.max(logits, axis=-1, keepdims=True))
    aff_ref[0] = e / jnp.sum(e, axis=-1, keepdims=True)


def _norm_mod(x, g, sh, sc, w_router=None):
    b, l, d = x.shape
    tm = _t(l, 256)
    specs = [pl.BlockSpec((1, tm, d), lambda i, j: (i, j, 0)),
             pl.BlockSpec((1, d), lambda i, j: (0, 0)),
             pl.BlockSpec((1, 1, d), lambda i, j: (i, 0, 0)),
             pl.BlockSpec((1, 1, d), lambda i, j: (i, 0, 0))]
    h_spec = pl.BlockSpec((1, tm, d), lambda i, j: (i, j, 0))
    h_shape = jax.ShapeDtypeStruct((b, l, d), BF16)
    if w_router is None:
        return pl.pallas_call(_norm_mod_kernel, grid=(b, l // tm), in_specs=specs, out_specs=h_spec,
                              out_shape=h_shape, compiler_params=_cp(("arbitrary", "arbitrary")),
                              name="norm_mod")(x, g.reshape(1, d), sh, sc)
    wr = jnp.pad(w_router, ((0, 0), (0, LANES - w_router.shape[1])))
    return pl.pallas_call(
        _norm_router_kernel, grid=(b, l // tm),
        in_specs=specs + [pl.BlockSpec((d, LANES), lambda i, j: (0, 0))],
        out_specs=[h_spec, pl.BlockSpec((1, tm, LANES), lambda i, j: (i, j, 0))],
        out_shape=[h_shape, jax.ShapeDtypeStruct((b, l, LANES), F32)],
        compiler_params=_cp(("arbitrary", "arbitrary")), name="norm_router",
    )(x, g.reshape(1, d), sh, sc, wr)


def _final_norm_kernel(x_ref, g_ref, o_ref):
    x = x_ref[0]
    o_ref[0] = x * lax.rsqrt(jnp.mean(x * x, axis=-1, keepdims=True) + EPS) * g_ref[...]


def _final_norm(x, g):
    b, l, d = x.shape
    tm = _t(l, 256)
    return pl.pallas_call(
        _final_norm_kernel, grid=(b, l // tm),
        in_specs=[pl.BlockSpec((1, tm, d), lambda i, j: (i, j, 0)), pl.BlockSpec((1, d), lambda i, j: (0, 0))],
        out_specs=pl.BlockSpec((1, tm, d), lambda i, j: (i, j, 0)),
        out_shape=jax.ShapeDtypeStruct((b, l, d), F32),
        compiler_params=_cp(("arbitrary", "arbitrary")), name="final_norm",
    )(x, g.reshape(1, d))


def _dot_nt(a, w):
    return lax.dot_general(a, w, (((1,), (1,)), ((), ())), preferred_element_type=F32)


def _mm_kernel(a_ref, w_ref, o_ref):
    o_ref[...] = _dot_nt(a_ref[...], w_ref[0]).astype(o_ref.dtype)


def _matmul_nt(a, w, layer, row0, n, out_dtype, tm, tn):
    m, k = a.shape
    tm, tn = _t(m, tm), _t(n, tn)
    return pl.pallas_call(
        _mm_kernel, grid=(n // tn, m // tm),
        in_specs=[pl.BlockSpec((tm, k), lambda j, i: (i, 0)),
                  pl.BlockSpec((pl.Element(1), pl.Element(tn), pl.Element(k)), lambda j, i: (layer, pl.multiple_of(row0 + j * tn, BF16_ROWS), 0))],
        out_specs=pl.BlockSpec((tm, tn), lambda j, i: (i, j)),
        out_shape=jax.ShapeDtypeStruct((m, n), out_dtype),
        compiler_params=_cp(("arbitrary", "arbitrary")), name="matmul",
    )(a, w)


def _mm_res_kernel(a_ref, w_ref, x_ref, g_ref, o_ref):
    o_ref[0] = x_ref[0] + g_ref[0] * _dot(a_ref[0], w_ref[...].astype(BF16))


def _matmul_residual(a, w, x, g):
    b, l, k = a.shape
    n = w.shape[1]
    tm, tn = _t(l, 1024), _t(n, 512)
    return pl.pallas_call(
        _mm_res_kernel, grid=(n // tn, b, l // tm),
        in_specs=[pl.BlockSpec((1, tm, k), lambda j, bi, i: (bi, i, 0)),
                  pl.BlockSpec((k, tn), lambda j, bi, i: (0, j)),
                  pl.BlockSpec((1, tm, tn), lambda j, bi, i: (bi, i, j)),
                  pl.BlockSpec((1, 1, tn), lambda j, bi, i: (bi, 0, j))],
        out_specs=pl.BlockSpec((1, tm, tn), lambda j, bi, i: (bi, i, j)),
        out_shape=jax.ShapeDtypeStruct((b, l, n), F32),
        compiler_params=_cp(("arbitrary", "arbitrary", "arbitrary")), name="matmul_residual",
    )(a, w, x, g)


ATT_K, ATT_V, ATT_KVA, ATT_KPE, ATT_KV_COLS = 0, 256, 512, 768, 896
ATT_Q, ATT_QA, ATT_Q_COLS = 0, 512, 896


def _rope(xh, tab_ref, half):
    lane = lax.broadcasted_iota(jnp.int32, xh.shape, 1)
    partner = jnp.where((lane % (2 * half)) < half, pltpu.roll(xh, LANES - half, 1), pltpu.roll(xh, half, 1))
    return xh * tab_ref[0] + partner * tab_ref[1]


def _rms(x, gain):
    return x * lax.rsqrt(jnp.mean(x * x, axis=-1, keepdims=True) + EPS) * gain


def _prep_kernel(*refs, use_pos):
    if use_pos:
        px_ref, pq_ref, tg_ref, tm_ref = refs[:4]
        refs = refs[4:]
    else:
        px_ref, pq_ref = refs[:2]
        refs = refs[2:]
    gq_ref, gk_ref, gqa_ref, gkva_ref, qb_ref, kvbk_ref, kvbv_ref, kg_ref, vg_ref, km_ref, vm_ref, qg_ref, qm_ref = refs
    p = px_ref[0]
    pq = pq_ref[0]
    hd = GQA_HEAD_DIM
    for h in range(GQA_KV_HEADS):
        kh = _rms(p[:, ATT_K + h * hd:ATT_K + (h + 1) * hd], gk_ref[...])
        if use_pos:
            kh = _rope(kh, tg_ref, 32)
        kg_ref[0, :, h * hd:(h + 1) * hd] = kh.astype(BF16)
    vg_ref[0] = p[:, ATT_V:ATT_V + GQA_KV_HEADS * hd].astype(BF16)
    for h in range(GQA_HEADS):
        qh = _rms(pq[:, ATT_Q + h * hd:ATT_Q + (h + 1) * hd], gq_ref[...])
        if use_pos:
            qh = _rope(qh, tg_ref, 32)
        qg_ref[0, :, h * hd:(h + 1) * hd] = qh.astype(BF16)
    kvn = _rms(p[:, ATT_KVA:ATT_KVA + MLA_KV_LORA], gkva_ref[...]).astype(BF16)
    knope = _dot(kvn, kvbk_ref[...].astype(BF16))
    vm_ref[0] = _dot(kvn, kvbv_ref[...].astype(BF16)).astype(BF16)
    kpe = p[:, ATT_KPE:ATT_KPE + LANES]
    lane = lax.broadcasted_iota(jnp.int32, kpe.shape, 1)
    kpe = pltpu.roll(jnp.where(lane < MLA_ROPE, kpe, 0.0), MLA_NOPE, 1)
    if use_pos:
        kpe = _rope(kpe, tm_ref, 8)
    qan = _rms(pq[:, ATT_QA:ATT_QA + MLA_Q_LORA], gqa_ref[...]).astype(BF16)
    qmf = _dot(qan, qb_ref[...].astype(BF16))
    for h in range(MLA_HEADS):
        km_ref[0, :, h * LANES:(h + 1) * LANES] = (knope[:, h * LANES:(h + 1) * LANES] + kpe).astype(BF16)
        qh = qmf[:, h * LANES:(h + 1) * LANES]
        if use_pos:
            qh = _rope(qh, tm_ref, 8)
        qm_ref[0, :, h * LANES:(h + 1) * LANES] = qh.astype(BF16)


def _attn_prep(px, pq, tabs, lw):
    b, l, _ = px.shape
    tm = _t(l, 256)
    use_pos = tabs is not None
    full = lambda shape: pl.BlockSpec(shape, lambda i, j: (0,) * len(shape))
    in_specs = [pl.BlockSpec((1, tm, ATT_KV_COLS), lambda i, j: (i, j, 0)),
                pl.BlockSpec((1, tm, ATT_Q_COLS), lambda i, j: (i, j, 0))]
    args = [px, pq]
    if use_pos:
        in_specs += [pl.BlockSpec((2, tm, LANES), lambda i, j: (0, j, 0))] * 2
        args += list(tabs)
    small = [lw['gqa_q_gain'].reshape(1, -1), lw['gqa_k_gain'].reshape(1, -1), lw['mla_q_a_gain'].reshape(1, -1),
             lw['mla_kv_a_gain'].reshape(1, -1), lw['q_b'], lw['kv_b_k'], lw['kv_b_v']]
    in_specs += [full(a.shape) for a in small]
    widths = [256, 256, 512, 512, 512, 512]
    return pl.pallas_call(
        functools.partial(_prep_kernel, use_pos=use_pos), grid=(b, l // tm), in_specs=in_specs,
        out_specs=[pl.BlockSpec((1, tm, w), lambda i, j: (i, j, 0)) for w in widths],
        out_shape=[jax.ShapeDtypeStruct((b, l, w), BF16) for w in widths],
        compiler_params=_cp(("arbitrary", "arbitrary")), name="attn_prep",
    )(*args, *small)


def _attn_kernel(*refs, n_src, scale, nsplit):
    q_ref, o_ref = refs[0], refs[-1]
    rows = q_ref.shape[1] // nsplit
    c2 = scale * math.log2(math.e)
    for c in range(nsplit):
        q = q_ref[0, c * rows:(c + 1) * rows, :]
        ss = [lax.dot_general(q, refs[1 + 2 * j][0], (((1,), (1,)), ((), ())), preferred_element_type=F32)
              for j in range(n_src)]
        m = jnp.max(ss[0], axis=-1, keepdims=True)
        for s in ss[1:]:
            m = jnp.maximum(m, jnp.max(s, axis=-1, keepdims=True))
        acc, den = None, None
        for j, s in enumerate(ss):
            p = jnp.exp2((s - m) * c2)
            d = jnp.sum(p, axis=-1, keepdims=True)
            a = _dot(p.astype(BF16), refs[2 + 2 * j][0])
            acc = a if acc is None else acc + a
            den = d if den is None else den + d
        o_ref[0, c * rows:(c + 1) * rows, :] = (acc / den).astype(o_ref.dtype)


def _attention(q, srcs, heads, kv_heads, dk):
    b, s, _ = q.shape
    r = heads // kv_heads
    tq = _t(s, 512)
    in_specs = [pl.BlockSpec((1, tq, LANES), lambda bi, h, i: (bi, i, h))]
    args = [q]
    for k, v in srcs:
        lk = k.shape[1]
        in_specs += [pl.BlockSpec((1, lk, LANES), lambda bi, h, i: (bi, 0, h // r))] * 2
        args += [k, v]
    return pl.pallas_call(
        functools.partial(_attn_kernel, n_src=len(srcs), scale=float(dk) ** -0.5, nsplit=2 if tq % 256 == 0 else 1),
        grid=(b, heads, s // tq), in_specs=in_specs,
        out_specs=pl.BlockSpec((1, tq, LANES), lambda bi, h, i: (bi, i, h)),
        out_shape=jax.ShapeDtypeStruct((b, s, heads * LANES), BF16),
        compiler_params=_cp(("arbitrary", "arbitrary", "arbitrary")), name="attention",
    )(*args)


def _dwconv_kernel(*refs, taps, glu, post_ln, chunk, sub):
    if glu:
        ap, ac, an, bp, bc, bn, w_ref, b_ref, lg_ref, lb_ref, o_ref, scr = refs
        load = lambda a, g: a[0] * jax.nn.sigmoid(g[0])
        prev, cur, nxt = load(ap, bp), load(ac, bc), load(an, bn)
    else:
        ap, ac, an, w_ref, b_ref, o_ref, scr = refs
        prev, cur, nxt = ap[0], ac[0], an[0]
    i = pl.program_id(2)
    last = pl.num_programs(2) - 1
    pad = CONV_PAD
    scr[0, 0:pad, :] = jnp.where(i > 0, prev, 0.0)
    scr[0, pad:pad + chunk, :] = cur
    scr[0, pad + chunk:2 * pad + chunk, :] = jnp.where(i < last, nxt, 0.0)
    lo = (taps - 1) // 2
    span = chunk + 2 * pad - SUBLANES
    for r in sorted({(pad - lo + j) % SUBLANES for j in range(taps)} - {0}):
        scr[r, 0:span, :] = scr[0, pl.ds(r, span), :]
    for c in range(chunk // sub):
        acc = None
        for j in range(taps):
            o = c * sub + pad - lo + j
            term = w_ref[j:j + 1, :] * scr[o % SUBLANES, pl.ds(o - o % SUBLANES, sub), :]
            acc = term if acc is None else acc + term
        y = acc + b_ref[...]
        if post_ln:
            mu = jnp.mean(y, axis=-1, keepdims=True)
            yc = y - mu
            var = jnp.mean(yc * yc, axis=-1, keepdims=True)
            y = yc * lax.rsqrt(var + EPS) * lg_ref[...] + lb_ref[...]
            y = y * jax.nn.sigmoid(y)
        o_ref[0, c * sub:(c + 1) * sub, :] = y.astype(o_ref.dtype)


def _dwconv(x, w, bias, ncol, *, glu=False, ln=None, out_dtype=F32):
    b, l, _ = x.shape
    taps = w.shape[0]
    cw = 512
    chunk = _t(l, 256)
    nblk = l // chunk
    per = chunk // CONV_PAD
    nsmall = l // CONV_PAD

    def views(coff):
        return [pl.BlockSpec((1, CONV_PAD, cw), lambda bi, c, i: (bi, jnp.maximum(i * per - 1, 0), c + coff)),
                pl.BlockSpec((1, chunk, cw), lambda bi, c, i: (bi, i, c + coff)),
                pl.BlockSpec((1, CONV_PAD, cw), lambda bi, c, i: (bi, jnp.minimum((i + 1) * per, nsmall - 1), c + coff))]

    in_specs = views(0)
    args = [x, x, x]
    if glu:
        in_specs += views(ncol)
        args += [x, x, x]
    vec = lambda: pl.BlockSpec((1, cw), lambda bi, c, i: (0, c))
    in_specs += [pl.BlockSpec((taps, cw), lambda bi, c, i: (0, c)), vec()]
    args += [w, bias.reshape(1, -1)]
    if ln is not None:
        in_specs += [vec(), vec()]
        args += [ln[0].reshape(1, -1), ln[1].reshape(1, -1)]
    return pl.pallas_call(
        functools.partial(_dwconv_kernel, taps=taps, glu=glu, post_ln=ln is not None, chunk=chunk, sub=32),
        grid=(b, ncol, nblk), in_specs=in_specs,
        out_specs=pl.BlockSpec((1, chunk, cw), lambda bi, c, i: (bi, i, c)),
        out_shape=jax.ShapeDtypeStruct((b, l, ncol * cw), out_dtype),
        scratch_shapes=[pltpu.VMEM((SUBLANES, chunk + 2 * CONV_PAD, cw), F32)],
        compiler_params=_cp(("arbitrary", "arbitrary", "arbitrary")), name="dwconv",
    )(*args)


def _dft_tables(l):
    n2 = 2 * l
    k = jnp.arange(l, dtype=jnp.int32)
    m = (k[:, None] * k[None, :]) & (n2 - 1)
    ang = m.astype(F32) * (2.0 * math.pi / n2)
    c = jnp.cos(ang)
    s = -jnp.sin(ang)
    nyq = jnp.where(k % 2 == 0, 1.0, -1.0).astype(F32)
    s = jnp.where((k == 0)[:, None], nyq[None, :], s)
    p = jnp.stack([c, s]).astype(BF16)
    return p, jnp.swapaxes(p, 1, 2)


def _hy_feats(l):
    pos = jnp.arange(l, dtype=F32)
    t01 = pos / (l - 1)
    bands = jnp.linspace(1e-4, HY_BANDS - 1, HY_BANDS, dtype=F32)
    ang = (2.0 * math.pi / l) * pos[:, None] * bands[None, :]
    feats = jnp.concatenate([t01[:, None], jnp.cos(ang), -jnp.sin(ang)], axis=-1)
    return jnp.pad(feats, ((0, 0), (0, LANES - feats.shape[1])))


def _hyfilt_kernel(f_ref, w1_ref, b1_ref, fr_ref, w2_ref, b2_ref, w3_ref, lr_ref, o_ref, *, tl):
    feats = f_ref[...]
    fr = fr_ref[...]
    hdot = lambda a, b: jnp.dot(a, b, preferred_element_type=F32, precision=HI)
    h = jnp.sin(fr * (hdot(feats, w1_ref[...]) + b1_ref[...]))
    h = jnp.sin(fr * (hdot(h, w2_ref[...]) + b2_ref[...]))
    h = hdot(h, w3_ref[...])
    h = h * jnp.exp(-feats[:, 0:1] * jnp.exp(lr_ref[...]))
    row = pl.program_id(0) * tl + lax.broadcasted_iota(jnp.int32, h.shape, 0)
    col = lax.broadcasted_iota(jnp.int32, h.shape, 1)
    is_bwd = ((col // HY_WIDTH) % 2) == 1
    o_ref[...] = jnp.where((row == 0) & is_bwd, 0.0, h).astype(o_ref.dtype)


def _hy_filters(l, lw):
    feats = _hy_feats(l)
    hid = HY_FILTER_HIDDEN
    padc = lambda a: jnp.pad(a, ((0, 0), (0, LANES - a.shape[1])))
    w1 = jnp.pad(lw['hf_w1'], ((0, LANES - lw['hf_w1'].shape[0]), (0, LANES - hid)))
    w2 = jnp.pad(lw['hf_w2'], ((0, LANES - hid), (0, LANES - hid)))
    w3 = jnp.pad(lw['hf_w3'], ((0, LANES - hid), (0, 0)))
    b1, b2, fr = padc(lw['hf_b1'].reshape(1, -1)), padc(lw['hf_b2'].reshape(1, -1)), padc(lw['hf_freq'].reshape(1, -1))
    lr = lw['hf_log_rate'].reshape(1, -1)
    nc = w3.shape[1]
    tl = _t(l, 256)
    full = lambda a: pl.BlockSpec(a.shape, lambda i: (0, 0))
    return pl.pallas_call(
        functools.partial(_hyfilt_kernel, tl=tl), grid=(l // tl,),
        in_specs=[pl.BlockSpec((tl, LANES), lambda i: (i, 0)), full(w1), full(b1), full(fr), full(w2), full(b2),
                  full(w3), full(lr)],
        out_specs=pl.BlockSpec((tl, nc), lambda i: (i, 0)),
        out_shape=jax.ShapeDtypeStruct((l, nc), BF16),
        compiler_params=_cp(("arbitrary",)), name="hy_filter",
    )(feats, w1, b1, fr, w2, b2, w3, lr)


def _hyspec_kernel(p_ref, h_ref, o_ref, *, tk, scale):
    hm = h_ref[...]
    ar = _dot(p_ref[0], hm)
    ai = _dot(p_ref[1], hm)
    c = HY_WIDTH
    row0 = (pl.program_id(0) * tk + lax.broadcasted_iota(jnp.int32, (tk, c), 0)) == 0
    s = jnp.where(row0, 0.5 * scale, scale)
    for n in range(HY_ORDER):
        o = 2 * n * c
        o_ref[n, 0] = (ar[:, o:o + c] + ar[:, o + c:o + 2 * c]) * s
        fi, bi = ai[:, o:o + c], ai[:, o + c:o + 2 * c]
        o_ref[n, 1] = jnp.where(row0, fi + bi, fi - bi) * s


def _hy_spectrum(p, filt):
    l = p.shape[1]
    tk = _t(l, 256)
    return pl.pallas_call(
        functools.partial(_hyspec_kernel, tk=tk, scale=1.0 / l), grid=(l // tk,),
        in_specs=[pl.BlockSpec((2, tk, l), lambda i: (0, i, 0)), pl.BlockSpec(filt.shape, lambda i: (0, 0))],
        out_specs=pl.BlockSpec((HY_ORDER, 2, tk, HY_WIDTH), lambda i: (0, 0, i, 0)),
        out_shape=jax.ShapeDtypeStruct((HY_ORDER, 2, l, HY_WIDTH), F32),
        compiler_params=_cp(("arbitrary",)), name="hy_spectrum",
    )(p, filt)


def _hyfwd_kernel(p_ref, z_ref, k_ref, y_ref, *, tk):
    z = z_ref[0].astype(BF16)
    xr = _dot(p_ref[0], z)
    xi = _dot(p_ref[1], z)
    kr, ki = k_ref[0, 0], k_ref[0, 1]
    row0 = (pl.program_id(0) * tk + lax.broadcasted_iota(jnp.int32, xr.shape, 0)) == 0
    xiki = xi * ki
    y_ref[0, 0] = (xr * kr - jnp.where(row0, 0.0, xiki)).astype(y_ref.dtype)
    y_ref[0, 1] = jnp.where(row0, xiki, xr * ki + xi * kr).astype(y_ref.dtype)


def _hy_forward(p, z, zcol, kf, order):
    b, l, _ = z.shape
    c = HY_WIDTH
    tk = _t(l, 1024)
    return pl.pallas_call(
        functools.partial(_hyfwd_kernel, tk=tk), grid=(l // tk, b),
        in_specs=[pl.BlockSpec((2, tk, l), lambda i, bi: (0, i, 0)),
                  pl.BlockSpec((1, l, c), lambda i, bi: (bi, 0, zcol)),
                  pl.BlockSpec((1, 2, tk, c), lambda i, bi: (order, 0, i, 0))],
        out_specs=pl.BlockSpec((1, 2, tk, c), lambda i, bi: (bi, 0, i, 0)),
        out_shape=jax.ShapeDtypeStruct((b, 2, l, c), BF16),
        compiler_params=_cp(("arbitrary", "arbitrary")), name="hy_forward",
    )(p, z, kf)


def _hyinv_kernel(pt_ref, y_ref, g_ref, z_ref, bias_ref, o_ref):
    conv = _dot(pt_ref[0], y_ref[0, 0]) + _dot(pt_ref[1], y_ref[0, 1])
    o_ref[0] = (g_ref[0] * (conv + bias_ref[...] * z_ref[0])).astype(o_ref.dtype)


def _hy_inverse(pt, y, u, gcol, z, zcol, bias, out_dtype):
    b, _, l, c = y.shape
    tn = _t(l, 1024)
    return pl.pallas_call(
        _hyinv_kernel, grid=(l // tn, b),
        in_specs=[pl.BlockSpec((2, tn, l), lambda i, bi: (0, i, 0)),
                  pl.BlockSpec((1, 2, l, c), lambda i, bi: (bi, 0, 0, 0)),
                  pl.BlockSpec((1, tn, c), lambda i, bi: (bi, i, gcol)),
                  pl.BlockSpec((1, tn, c), lambda i, bi: (bi, i, zcol)),
                  pl.BlockSpec((1, c), lambda i, bi: (0, 0))],
        out_specs=pl.BlockSpec((1, tn, c), lambda i, bi: (bi, i, 0)),
        out_shape=jax.ShapeDtypeStruct((b, l, c), out_dtype),
        compiler_params=_cp(("arbitrary", "arbitrary")), name="hy_inverse",
    )(pt, y, u, z, bias.reshape(1, c))


def _hyena(phy, lw, dft):
    p, pt, kf = dft
    u = _dwconv(phy, lw['hy_conv_w'], lw['hy_conv_b'], HY_ORDER + 1)
    y = _hy_forward(p, u, 0, kf, 0)
    z = _hy_inverse(pt, y, u, 1, u, 0, lw['hy_bias'][0], F32)
    y = _hy_forward(p, z, 0, kf, 1)
    return _hy_inverse(pt, y, u, 2, z, 0, lw['hy_bias'][1], BF16)


def _merge_kernel(hx_ref, g0, g1, g2, g3, b0, b1, b2, b3, wbr_ref, o_ref):
    hx = hx_ref[...]
    acc = None
    for n, (wg, br) in enumerate(((g0, b0), (g1, b1), (g2, b2), (g3, b3))):
        gate = jax.nn.sigmoid(_dot_nt(hx, wg[0]))
        term = gate * _dot(br[...], wbr_ref[n])
        acc = term if acc is None else acc + term
    o_ref[...] = acc.astype(o_ref.dtype)


def _merge(hx, w_gate, layer, row0, branches, w_br):
    m, d = hx.shape
    tm, tn = _t(m, 512), _t(d, 512)
    nj = d // tn
    bw = BRANCH_WIDTH
    in_specs = [pl.BlockSpec((tm, d), lambda j, i: (i, 0))]
    in_specs += [pl.BlockSpec((pl.Element(1), pl.Element(tn), pl.Element(d)),
                              functools.partial(lambda j, i, n: (layer, pl.multiple_of(row0 + n * d + j * tn, BF16_ROWS), 0), n=n))
                 for n in range(N_BRANCH)]
    in_specs += [pl.BlockSpec((tm, bw), lambda j, i: (i, 0))] * N_BRANCH
    in_specs += [pl.BlockSpec((N_BRANCH, bw, tn), lambda j, i: (0, 0, j))]
    return pl.pallas_call(
        _merge_kernel, grid=(nj, m // tm), in_specs=in_specs,
        out_specs=pl.BlockSpec((tm, tn), lambda j, i: (i, j)),
        out_shape=jax.ShapeDtypeStruct((m, d), BF16),
        compiler_params=_cp(("arbitrary", "arbitrary")), name="merge",
    )(hx, w_gate, w_gate, w_gate, w_gate, *branches, w_br)


def _topk_kernel(aff_ref, tri_ref, sp_ref, spt_ref, g_ref, *, cap):
    afft = aff_ref[0].T[:N_EXPERTS, :]
    bits = lax.bitcast_convert_type(afft, jnp.int32)

    def body(i, prefix):
        cand = prefix | jnp.left_shift(jnp.int32(1), 30 - i)
        cnt = jnp.sum((bits >= cand).astype(F32), axis=1, keepdims=True)
        return jnp.where(cnt >= cap, cand, prefix)

    thr = lax.fori_loop(0, 31, body, jnp.zeros((N_EXPERTS, 1), jnp.int32))
    gt = bits > thr
    eq = bits == thr
    need = cap - jnp.sum(gt.astype(F32), axis=1, keepdims=True)
    tri = tri_ref[...]
    rank_eq = _dot(eq.astype(BF16), tri)
    sel = gt | (eq & (rank_eq <= need))
    pos = _dot(sel.astype(BF16), tri) - 1.0
    sp = jnp.where(sel, pos, -1.0)
    sp_ref[0] = sp.astype(jnp.int32)
    n = sp.shape[1]
    padded = jnp.concatenate([sp, jnp.full((LANES - N_EXPERTS, n), -1.0, F32)], axis=0)
    spt_ref[0] = padded.T
    icap = g_ref.shape[2]
    slot = lax.broadcasted_iota(jnp.int32, (icap, n), 0).astype(F32)
    for e in range(N_EXPERTS):
        hit = slot == sp[e:e + 1, :]
        g_ref[0, e] = jnp.sum(jnp.where(hit, afft[e:e + 1, :], 0.0), axis=1, keepdims=True)


def _topk(aff, cap):
    b, n, _ = aff.shape
    t = jnp.arange(n, dtype=jnp.int32)
    tri = (t[:, None] <= t[None, :]).astype(BF16)
    return pl.pallas_call(
        functools.partial(_topk_kernel, cap=float(cap)), grid=(b,),
        in_specs=[pl.BlockSpec((1, n, LANES), lambda i: (i, 0, 0)), pl.BlockSpec((n, n), lambda i: (0, 0))],
        out_specs=[pl.BlockSpec((1, N_EXPERTS, n), lambda i: (i, 0, 0)), pl.BlockSpec((1, n, LANES), lambda i: (i, 0, 0)),
                   pl.BlockSpec((1, N_EXPERTS, cap, 1), lambda i: (i, 0, 0, 0))],
        out_shape=[jax.ShapeDtypeStruct((b, N_EXPERTS, n), jnp.int32), jax.ShapeDtypeStruct((b, n, LANES), F32),
                   jax.ShapeDtypeStruct((b, N_EXPERTS, cap, 1), F32)],
        compiler_params=_cp(("arbitrary",)), name="topk",
    )(aff, tri)


def _gather_kernel(sp_ref, h_ref, o_ref, *, cap):
    sp = sp_ref[0, 0]
    slot = lax.broadcasted_iota(jnp.int32, (cap, sp.shape[1]), 0)
    onehot = (slot == sp).astype(BF16)
    o_ref[0] = _dot(onehot, h_ref[0]).astype(o_ref.dtype)


def _gather(sp, h, cap):
    b, n, d = h.shape
    return pl.pallas_call(
        functools.partial(_gather_kernel, cap=cap), grid=(b, N_EXPERTS),
        in_specs=[pl.BlockSpec((1, 1, 1, n), lambda bi, e: (bi, e, 0, 0)), pl.BlockSpec((1, n, d), lambda bi, e: (bi, 0, 0))],
        out_specs=pl.BlockSpec((1, cap, d), lambda bi, e: (e, bi, 0)),
        out_shape=jax.ShapeDtypeStruct((N_EXPERTS, b * cap, d), BF16),
        compiler_params=_cp(("arbitrary", "arbitrary")), name="moe_gather",
    )(sp.reshape(b, N_EXPERTS, 1, n), h)


def _ffn_kernel(*refs, n_x):
    xs = refs[:n_x]
    gs = refs[n_x:2 * n_x]
    w1_ref, w3_ref, w2_ref = refs[2 * n_x:2 * n_x + 3]
    ys = refs[2 * n_x + 3:3 * n_x + 3]
    accs = refs[3 * n_x + 3:]
    f = pl.program_id(1)
    last = pl.num_programs(1) - 1
    w1, w3, w2 = w1_ref[0, 0].astype(BF16), w3_ref[0, 0].astype(BF16), w2_ref[0, 0].astype(BF16)
    for x_ref, g_ref, y_ref, acc in zip(xs, gs, ys, accs):
        rows = x_ref.shape[1]
        grp = 512 if rows % 512 == 0 else rows
        parts = []
        for r in range(rows // grp):
            x = x_ref[0, r * grp:(r + 1) * grp, :]
            a = _dot(x, w1)
            hid = (a * jax.nn.sigmoid(a) * _dot(x, w3)).astype(BF16)
            parts.append(_dot(hid, w2))

        @pl.when(f == 0)
        def _():
            for r, part in enumerate(parts):
                acc[r * grp:(r + 1) * grp, :] = part

        @pl.when(f > 0)
        def _():
            for r, part in enumerate(parts):
                acc[r * grp:(r + 1) * grp, :] += part

        @pl.when(f == last)
        def _():
            nb, _, cap, _ = y_ref.shape
            for bi in range(nb):
                y_ref[bi, 0] = (acc[bi * cap:(bi + 1) * cap, :] * g_ref[bi, 0]).astype(y_ref.dtype)


def _expert_ffn(xs, gs, w1, w3, w2, layer):
    _, e, d, ff = w1.shape
    tf = _t(ff, 256)
    in_specs = [pl.BlockSpec((1, x.shape[1], d), lambda ei, f: (ei, 0, 0)) for x in xs]
    in_specs += [pl.BlockSpec((g.shape[0], 1, g.shape[2], 1), lambda ei, f: (0, ei, 0, 0)) for g in gs]
    in_specs += [pl.BlockSpec((1, 1, d, tf), lambda ei, f: (layer, ei, 0, f))] * 2
    in_specs += [pl.BlockSpec((1, 1, tf, d), lambda ei, f: (layer, ei, f, 0))]
    return pl.pallas_call(
        functools.partial(_ffn_kernel, n_x=len(xs)), grid=(e, ff // tf), in_specs=in_specs,
        out_specs=[pl.BlockSpec((g.shape[0], 1, g.shape[2], d), lambda ei, f: (0, ei, 0, 0)) for g in gs],
        out_shape=[jax.ShapeDtypeStruct((g.shape[0], e, g.shape[2], d), BF16) for g in gs],
        scratch_shapes=[pltpu.VMEM((x.shape[1], d), F32) for x in xs],
        compiler_params=_cp(("arbitrary", "arbitrary")), name="expert_ffn",
    )(*xs, *gs, w1, w3, w2)


def _scatter_kernel(spt_ref, y_ref, x_ref, g_ref, o_ref, acc_ref, *, cap, kc):
    k = pl.program_id(2)
    tt = spt_ref.shape[1]
    blk = y_ref.shape[1]
    spt = spt_ref[0].astype(BF16)
    term = None
    for j in range(blk // kc):
        first = (k * blk + j * kc) // cap
        col = lax.broadcasted_iota(jnp.int32, (LANES, kc), 1)
        row = lax.broadcasted_iota(jnp.int32, (LANES, kc), 0)
        rep = (row == first + col // cap).astype(BF16)
        mine = _dot(spt, rep)
        slot = (lax.broadcasted_iota(jnp.int32, (tt, kc), 1) % cap).astype(F32)
        part = _dot((mine == slot).astype(BF16), y_ref[0, j * kc:(j + 1) * kc, :])
        term = part if term is None else term + part

    @pl.when(k == 0)
    def _():
        acc_ref[...] = term

    @pl.when(k > 0)
    def _():
        acc_ref[...] += term

    @pl.when(k == pl.num_programs(2) - 1)
    def _():
        o_ref[0] = x_ref[0] + g_ref[0] * acc_ref[...]


def _scatter(spt, y, x, g, cap):
    b, n, d = x.shape
    tt = _t(n, 512)
    kc = _t(N_EXPERTS * cap, 1024)
    blk = _t(N_EXPERTS * cap, 2 * kc)
    assert kc % cap == 0 and cap <= 256
    return pl.pallas_call(
        functools.partial(_scatter_kernel, cap=cap, kc=kc), grid=(b, n // tt, N_EXPERTS * cap // blk),
        in_specs=[pl.BlockSpec((1, tt, LANES), lambda bi, i, k: (bi, i, 0)),
                  pl.BlockSpec((1, blk, d), lambda bi, i, k: (bi, k, 0)),
                  pl.BlockSpec((1, tt, d), lambda bi, i, k: (bi, i, 0)),
                  pl.BlockSpec((1, 1, d), lambda bi, i, k: (bi, 0, 0))],
        out_specs=pl.BlockSpec((1, tt, d), lambda bi, i, k: (bi, i, 0)),
        out_shape=jax.ShapeDtypeStruct((b, n, d), F32),
        scratch_shapes=[pltpu.VMEM((tt, d), F32)],
        compiler_params=_cp(("arbitrary", "arbitrary", "arbitrary")), name="moe_scatter",
    )(spt, y, x, g)


def _rope_tables(s):
    rows = jnp.repeat(jnp.arange(s // GRID_W, dtype=jnp.int32), GRID_W).astype(F32)
    cols = jnp.tile(jnp.arange(GRID_W, dtype=jnp.int32), s // GRID_W).astype(F32)

    def half_tables(m):
        inv = ROPE_BASE ** (-jnp.arange(0, m, 2, dtype=F32) / m)
        out = []
        for pos in (rows, cols):
            ang = pos[:, None] * inv[None, :]
            c, sn = jnp.cos(ang), jnp.sin(ang)
            out.append((jnp.concatenate([c, c], axis=1), jnp.concatenate([-sn, sn], axis=1)))
        return (jnp.concatenate([out[0][0], out[1][0]], axis=1), jnp.concatenate([out[0][1], out[1][1]], axis=1))

    cg, sg = half_tables(GQA_HEAD_DIM // 2)
    cm, sm = half_tables(MLA_ROPE // 2)
    padm = lambda a, fill: jnp.concatenate(
        [jnp.full((s, MLA_NOPE), fill, F32), a, jnp.full((s, LANES - MLA_NOPE - MLA_ROPE), fill, F32)], axis=1)
    return jnp.stack([cg, sg]), jnp.stack([padm(cm, 1.0), padm(sm, 0.0)])


W_IN_Q = 2 * GQA_KV_HEADS * GQA_HEAD_DIM + MLA_KV_LORA + MLA_ROPE
W_IN_HY = W_IN_Q + GQA_HEADS * GQA_HEAD_DIM + MLA_Q_LORA
W_IN_CV = W_IN_HY + (HY_ORDER + 1) * HY_WIDTH
W_IN_GATE = W_IN_CV + 2 * CV_WIDTH


def _layer_weights(i, w):
    assert W_IN_Q + LANES - MLA_ROPE == ATT_KV_COLS and W_IN_HY - W_IN_Q == ATT_Q_COLS
    qb = w['mla_q_b'][i].reshape(MLA_Q_LORA, MLA_HEADS, MLA_NOPE + MLA_ROPE)
    qb = jnp.pad(qb, ((0, 0), (0, 0), (0, LANES - MLA_NOPE - MLA_ROPE))).reshape(MLA_Q_LORA, MLA_HEADS * LANES)
    kvb = w['mla_kv_b'][i].reshape(MLA_KV_LORA, MLA_HEADS, MLA_NOPE + MLA_V)
    kvb_k = jnp.pad(kvb[:, :, :MLA_NOPE], ((0, 0), (0, 0), (0, LANES - MLA_NOPE))).reshape(MLA_KV_LORA, MLA_HEADS * LANES)
    kvb_v = kvb[:, :, MLA_NOPE:].reshape(MLA_KV_LORA, MLA_HEADS * MLA_V)
    lw = {k: w[k][i] for k in ('gqa_q_gain', 'gqa_k_gain', 'mla_q_a_gain', 'mla_kv_a_gain', 'hy_conv_w', 'hy_conv_b',
                               'hf_w1', 'hf_b1', 'hf_w2', 'hf_b2', 'hf_w3', 'hf_freq', 'hf_log_rate', 'hy_bias',
                               'cv_w', 'cv_b', 'cv_ln_g', 'cv_ln_b', 'w_router', 'g_mix', 'g_ffn')}
    lw.update(layer=i, w_in_t=w['w_in_t'], q_b=qb, kv_b_k=kvb_k, kv_b_v=kvb_v,
              w_br=w['w_br'][i].astype(BF16), w_out=w['w_out'][i].astype(BF16))
    return lw


def _mixer_branches(hx, keys_extra, tabs, lw, dft):
    b, l, d = hx.shape
    hx2 = hx.reshape(b * l, d)
    proj = lambda row0, n, tn: _matmul_nt(hx2, lw['w_in_t'], lw['layer'], row0, n, F32, 1024, tn).reshape(b, l, n)
    px = proj(0, ATT_KV_COLS, ATT_KV_COLS)
    pq = proj(W_IN_Q, ATT_Q_COLS, ATT_Q_COLS)
    kg, vg, km, vm, qg, qm = _attn_prep(px, pq, tabs, lw)
    own = (kg, vg, km, vm)
    if dft is None:
        return None, own
    ek = keys_extra
    o_gqa = _attention(qg, [(kg, vg)] + ([(ek[0], ek[1])] if ek else []), GQA_HEADS, GQA_KV_HEADS, GQA_HEAD_DIM)
    o_mla = _attention(qm, [(km, vm)] + ([(ek[2], ek[3])] if ek else []), MLA_HEADS, MLA_HEADS, MLA_NOPE + MLA_ROPE)
    phy = proj(W_IN_HY, W_IN_CV - W_IN_HY, 768)
    o_hy = _hyena(phy, lw, dft)
    pcv = proj(W_IN_CV, W_IN_GATE - W_IN_CV, 1024)
    o_cv = _dwconv(pcv, lw['cv_w'], lw['cv_b'], 1, glu=True, ln=(lw['cv_ln_g'], lw['cv_ln_b']), out_dtype=BF16)
    flat = lambda a: a.reshape(b * l, -1)
    return [flat(o_hy), flat(o_gqa), flat(o_mla), flat(o_cv)], own


def kernel(x, c, ctx, c_ctx, w_mod, b_mod, g_mix, w_in, gqa_q_gain, gqa_k_gain, mla_q_a_gain, mla_q_b, mla_kv_a_gain, mla_kv_b, hy_conv_w, hy_conv_b, hf_w1, hf_b1, hf_w2, hf_b2, hf_w3, hf_freq, hf_log_rate, hy_bias, cv_w, cv_b, cv_ln_g, cv_ln_b, w_br, w_out, g_ffn, w_router, w1, w3, w2, g_final):
    w = dict(g_mix=g_mix, w_in_t=jnp.swapaxes(w_in, 1, 2).astype(BF16), gqa_q_gain=gqa_q_gain, gqa_k_gain=gqa_k_gain, mla_q_a_gain=mla_q_a_gain,
             mla_q_b=mla_q_b, mla_kv_a_gain=mla_kv_a_gain, mla_kv_b=mla_kv_b, hy_conv_w=hy_conv_w, hy_conv_b=hy_conv_b,
             hf_w1=hf_w1, hf_b1=hf_b1, hf_w2=hf_w2, hf_b2=hf_b2, hf_w3=hf_w3, hf_freq=hf_freq, hf_log_rate=hf_log_rate,
             hy_bias=hy_bias, cv_w=cv_w, cv_b=cv_b, cv_ln_g=cv_ln_g, cv_ln_b=cv_ln_b, w_br=w_br, w_out=w_out,
             g_ffn=g_ffn, w_router=w_router)
    bsz, s, d = x.shape
    lc = ctx.shape[1]
    depth = w_mod.shape[0]
    assert bsz < 8
    cvec = jnp.concatenate([c, c_ctx[None, :], jnp.zeros((8 - bsz - 1, d), F32)], axis=0)
    mod = _mod_all(cvec, w_mod, b_mod)
    tabs = _rope_tables(s)
    p_x, pt_x = _dft_tables(s)
    p_c, pt_c = _dft_tables(lc)
    cap_x = CAPACITY_FACTOR * s // N_EXPERTS
    cap_c = CAPACITY_FACTOR * lc // N_EXPERTS
    xc = ctx
    for i in range(depth):
        last = i == depth - 1
        lw = _layer_weights(i, w)
        mx = [mod[i, :bsz, j * d:(j + 1) * d].reshape(bsz, 1, d) for j in range(6)]
        mc = [jnp.broadcast_to(mod[i, bsz:bsz + 1, j * d:(j + 1) * d].reshape(1, 1, d), (bsz, 1, d)) for j in range(6)]

        hc = _norm_mod(xc, lw['g_mix'], mc[0], mc[1])
        dft_c = None if last else (p_c, pt_c, _hy_spectrum(p_c, _hy_filters(lc, lw)))
        br_c, keys_c = _mixer_branches(hc, None, None, lw, dft_c)
        hx = _norm_mod(x, lw['g_mix'], mx[0], mx[1])
        dft_x = (p_x, pt_x, _hy_spectrum(p_x, _hy_filters(s, lw)))
        br_x, _ = _mixer_branches(hx, keys_c, tabs, lw, dft_x)

        merged = _merge(hx.reshape(bsz * s, d), lw['w_in_t'], i, W_IN_GATE, br_x, lw['w_br']).reshape(bsz, s, d)
        x = _matmul_residual(merged, lw['w_out'], x, mx[2])
        hf, aff = _norm_mod(x, lw['g_ffn'], mx[3], mx[4], lw['w_router'])
        sp, spt, gslot = _topk(aff, cap_x)
        xs, gs = [_gather(sp, hf, cap_x)], [gslot]
        if not last:
            merged_c = _merge(hc.reshape(bsz * lc, d), lw['w_in_t'], i, W_IN_GATE, br_c, lw['w_br']).reshape(bsz, lc, d)
            xc = _matmul_residual(merged_c, lw['w_out'], xc, mc[2])
            hfc, aff_c = _norm_mod(xc, lw['g_ffn'], mc[3], mc[4], lw['w_router'])
            sp_c, spt_c, gslot_c = _topk(aff_c, cap_c)
            xs.append(_gather(sp_c, hfc, cap_c))
            gs.append(gslot_c)
        ys = _expert_ffn(xs, gs, w1, w3, w2, i)
        x = _scatter(spt, ys[0].reshape(bsz, N_EXPERTS * cap_x, d), x, mx[5], cap_x)
        if not last:
            xc = _scatter(spt_c, ys[1].reshape(bsz, N_EXPERTS * cap_c, d), xc, mc[5], cap_c)
    return _final_norm(x, g_final)
```

```python
import functools
import math

import numpy as np
import jax
import jax.numpy as jnp
from jax import lax
from jax.experimental import pallas as pl
from jax.experimental.pallas import tpu as pltpu

F32 = jnp.float32
BF16 = jnp.bfloat16
HI = lax.Precision.HIGHEST

GRID_W = 64
EPS = 1e-6
ROPE_BASE = 10000.0
GQA_HEADS, GQA_KV_HEADS, GQA_HEAD_DIM = 4, 2, 128
MLA_HEADS, MLA_Q_LORA, MLA_KV_LORA, MLA_NOPE, MLA_ROPE, MLA_V = 4, 384, 256, 64, 32, 128
HY_WIDTH, HY_ORDER, HY_BANDS, HY_FILTER_HIDDEN = 512, 2, 16, 64
CV_WIDTH, CV_KERNEL = 512, 31
N_BRANCH, BRANCH_WIDTH = 4, 512
N_EXPERTS, EXPERT_FF, CAPACITY_FACTOR = 16, 1024, 2
LANES = 128
SUBLANES = 8
BF16_ROWS = 16
CONV_PAD = 16
VMEM_LIMIT = 56 * 1024 * 1024


def _cp(sem, vmem=VMEM_LIMIT):
    return pltpu.CompilerParams(dimension_semantics=sem, vmem_limit_bytes=vmem)


def _t(n, pref):
    return pref if n % pref == 0 else n


def _dot(a, b):
    return jnp.dot(a, b, preferred_element_type=F32)


def _mod_kernel(c_ref, w_ref, b_ref, o_ref):
    c = c_ref[...]
    a = (c * jax.nn.sigmoid(c)).astype(BF16)
    o_ref[0] = _dot(a, w_ref[0].astype(BF16)) + b_ref[0]


def _mod_all(cvec, w_mod, b_mod):
    depth, d, n6 = w_mod.shape
    tn = _t(n6, 1024)
    return pl.pallas_call(
        _mod_kernel, grid=(depth, n6 // tn),
        in_specs=[pl.BlockSpec((8, d), lambda l, j: (0, 0)),
                  pl.BlockSpec((1, d, tn), lambda l, j: (l, 0, j)),
                  pl.BlockSpec((1, 1, tn), lambda l, j: (l, 0, j))],
        out_specs=pl.BlockSpec((1, 8, tn), lambda l, j: (l, 0, j)),
        out_shape=jax.ShapeDtypeStruct((depth, 8, n6), F32),
        compiler_params=_cp(("arbitrary", "arbitrary")), name="mod",
    )(cvec, w_mod, b_mod.reshape(depth, 1, n6))


def _norm_h(x_ref, g_ref, sh_ref, sc_ref):
    x = x_ref[0]
    y = x * lax.rsqrt(jnp.mean(x * x, axis=-1, keepdims=True) + EPS)
    return (y * g_ref[...]) * (1.0 + sc_ref[0]) + sh_ref[0]


def _norm_mod_kernel(x_ref, g_ref, sh_ref, sc_ref, o_ref):
    o_ref[0] = _norm_h(x_ref, g_ref, sh_ref, sc_ref).astype(o_ref.dtype)


def _norm_router_kernel(x_ref, g_ref, sh_ref, sc_ref, wr_ref, o_ref, aff_ref):
    h = _norm_h(x_ref, g_ref, sh_ref, sc_ref)
    o_ref[0] = h.astype(o_ref.dtype)
    logits = jnp.dot(h, wr_ref[...], preferred_element_type=F32, precision=HI)
    lane = lax.broadcasted_iota(jnp.int32, logits.shape, 1)
    logits = jnp.where(lane < N_EXPERTS, logits, -1e30)
    e = jnp.exp(logits - jnp.max(logits, axis=-1, keepdims=True))
    aff_ref[0] = e / jnp.sum(e, axis=-1, keepdims=True)


def _norm_mod(x, g, sh, sc, w_router=None):
    b, l, d = x.shape
    tm = _t(l, 256)
    specs = [pl.BlockSpec((1, tm, d), lambda i, j: (i, j, 0)),
             pl.BlockSpec((1, d), lambda i, j: (0, 0)),
             pl.BlockSpec((1, 1, d), lambda i, j: (i, 0, 0)),
             pl.BlockSpec((1, 1, d), lambda i, j: (i, 0, 0))]
    h_spec = pl.BlockSpec((1, tm, d), lambda i, j: (i, j, 0))
    h_shape = jax.ShapeDtypeStruct((b, l, d), BF16)
    if w_router is None:
        return pl.pallas_call(_norm_mod_kernel, grid=(b, l // tm), in_specs=specs, out_specs=h_spec,
                              out_shape=h_shape, compiler_params=_cp(("arbitrary", "arbitrary")),
                              name="norm_mod")(x, g.reshape(1, d), sh, sc)
    wr = jnp.pad(w_router, ((0, 0), (0, LANES - w_router.shape[1])))
    return pl.pallas_call(
        _norm_router_kernel, grid=(b, l // tm),
        in_specs=specs + [pl.BlockSpec((d, LANES), lambda i, j: (0, 0))],
        out_specs=[h_spec, pl.BlockSpec((1, tm, LANES), lambda i, j: (i, j, 0))],
        out_shape=[h_shape, jax.ShapeDtypeStruct((b, l, LANES), F32)],
        compiler_params=_cp(("arbitrary", "arbitrary")), name="norm_router",
    )(x, g.reshape(1, d), sh, sc, wr)


def _final_norm_kernel(x_ref, g_ref, o_ref):
    x = x_ref[0]
    o_ref[0] = x * lax.rsqrt(jnp.mean(x * x, axis=-1, keepdims=True) + EPS) * g_ref[...]


def _final_norm(x, g):
    b, l, d = x.shape
    tm = _t(l, 256)
    return pl.pallas_call(
        _final_norm_kernel, grid=(b, l // tm),
        in_specs=[pl.BlockSpec((1, tm, d), lambda i, j: (i, j, 0)), pl.BlockSpec((1, d), lambda i, j: (0, 0))],
        out_specs=pl.BlockSpec((1, tm, d), lambda i, j: (i, j, 0)),
        out_shape=jax.ShapeDtypeStruct((b, l, d), F32),
        compiler_params=_cp(("arbitrary", "arbitrary")), name="final_norm",
    )(x, g.reshape(1, d))


def _dot_nt(a, w):
    return lax.dot_general(a, w, (((1,), (1,)), ((), ())), preferred_element_type=F32)


def _mm_kernel(a_ref, w_ref, o_ref, wb_ref):
    @pl.when(pl.program_id(1) == 0)
    def _():
        wb_ref[...] = w_ref[0].astype(wb_ref.dtype)

    o_ref[...] = _dot_nt(a_ref[...], wb_ref[...]).astype(o_ref.dtype)


def _matmul_nt(a, w, layer, row0, n, out_dtype, tm, tn):
    m, k = a.shape
    tm, tn = _t(m, tm), _t(n, tn)
    return pl.pallas_call(
        _mm_kernel, grid=(n // tn, m // tm),
        in_specs=[pl.BlockSpec((tm, k), lambda j, i: (i, 0)),
                  pl.BlockSpec((pl.Element(1), pl.Element(tn), pl.Element(k)),
                               lambda j, i: (layer, pl.multiple_of(row0 + j * tn, SUBLANES), 0))],
        out_specs=pl.BlockSpec((tm, tn), lambda j, i: (i, j)),
        out_shape=jax.ShapeDtypeStruct((m, n), out_dtype),
        scratch_shapes=[pltpu.VMEM((tn, k), BF16)],
        compiler_params=_cp(("arbitrary", "arbitrary")), name="matmul",
    )(a, w)


def _mm_res_kernel(a_ref, w_ref, x_ref, g_ref, o_ref, wb_ref):
    @pl.when((pl.program_id(1) == 0) & (pl.program_id(2) == 0))
    def _():
        wb_ref[...] = w_ref[0].astype(wb_ref.dtype)

    o_ref[0] = x_ref[0] + g_ref[0] * _dot(a_ref[0], wb_ref[...])


def _matmul_residual(a, w, layer, x, g):
    b, l, k = a.shape
    n = w.shape[2]
    tm, tn = _t(l, 1024), _t(n, 512)
    return pl.pallas_call(
        _mm_res_kernel, grid=(n // tn, b, l // tm),
        in_specs=[pl.BlockSpec((1, tm, k), lambda j, bi, i: (bi, i, 0)),
                  pl.BlockSpec((1, k, tn), lambda j, bi, i: (layer, 0, j)),
                  pl.BlockSpec((1, tm, tn), lambda j, bi, i: (bi, i, j)),
                  pl.BlockSpec((1, 1, tn), lambda j, bi, i: (bi, 0, j))],
        out_specs=pl.BlockSpec((1, tm, tn), lambda j, bi, i: (bi, i, j)),
        out_shape=jax.ShapeDtypeStruct((b, l, n), F32),
        scratch_shapes=[pltpu.VMEM((k, tn), BF16)],
        compiler_params=_cp(("arbitrary", "arbitrary", "arbitrary")), name="matmul_residual",
    )(a, w, x, g)


ATT_K, ATT_V, ATT_KVA, ATT_KPE, ATT_KV_COLS = 0, 256, 512, 768, 896
ATT_Q, ATT_QA, ATT_Q_COLS = 0, 512, 896


def _rope(xh, tab_ref, half):
    lane = lax.broadcasted_iota(jnp.int32, xh.shape, 1)
    partner = jnp.where((lane % (2 * half)) < half, pltpu.roll(xh, LANES - half, 1), pltpu.roll(xh, half, 1))
    return xh * tab_ref[0] + partner * tab_ref[1]


def _rms(x, gain):
    return x * lax.rsqrt(jnp.mean(x * x, axis=-1, keepdims=True) + EPS) * gain


def _prep_kernel(*refs, use_pos):
    if use_pos:
        px_ref, pq_ref, tg_ref, tm_ref = refs[:4]
        refs = refs[4:]
    else:
        px_ref, pq_ref = refs[:2]
        refs = refs[2:]
    gq_ref, gk_ref, gqa_ref, gkva_ref, qb_ref, kvbk_ref, kvbv_ref, kg_ref, vg_ref, km_ref, vm_ref, qg_ref, qm_ref = refs
    p = px_ref[0]
    pq = pq_ref[0]
    hd = GQA_HEAD_DIM
    for h in range(GQA_KV_HEADS):
        kh = _rms(p[:, ATT_K + h * hd:ATT_K + (h + 1) * hd], gk_ref[...])
        if use_pos:
            kh = _rope(kh, tg_ref, 32)
        kg_ref[0, :, h * hd:(h + 1) * hd] = kh.astype(BF16)
    vg_ref[0] = p[:, ATT_V:ATT_V + GQA_KV_HEADS * hd].astype(BF16)
    for h in range(GQA_HEADS):
        qh = _rms(pq[:, ATT_Q + h * hd:ATT_Q + (h + 1) * hd], gq_ref[...])
        if use_pos:
            qh = _rope(qh, tg_ref, 32)
        qg_ref[0, :, h * hd:(h + 1) * hd] = qh.astype(BF16)
    kvn = _rms(p[:, ATT_KVA:ATT_KVA + MLA_KV_LORA], gkva_ref[...]).astype(BF16)
    knope = _dot(kvn, kvbk_ref[...].astype(BF16))
    vm_ref[0] = _dot(kvn, kvbv_ref[...].astype(BF16)).astype(BF16)
    kpe = p[:, ATT_KPE:ATT_KPE + LANES]
    lane = lax.broadcasted_iota(jnp.int32, kpe.shape, 1)
    kpe = pltpu.roll(jnp.where(lane < MLA_ROPE, kpe, 0.0), MLA_NOPE, 1)
    if use_pos:
        kpe = _rope(kpe, tm_ref, 8)
    qan = _rms(pq[:, ATT_QA:ATT_QA + MLA_Q_LORA], gqa_ref[...]).astype(BF16)
    qmf = _dot(qan, qb_ref[...].astype(BF16))
    for h in range(MLA_HEADS):
        km_ref[0, :, h * LANES:(h + 1) * LANES] = (knope[:, h * LANES:(h + 1) * LANES] + kpe).astype(BF16)
        qh = qmf[:, h * LANES:(h + 1) * LANES]
        if use_pos:
            qh = _rope(qh, tm_ref, 8)
        qm_ref[0, :, h * LANES:(h + 1) * LANES] = qh.astype(BF16)


def _attn_prep(px, pq, tabs, lw):
    b, l, _ = px.shape
    tm = _t(l, 256)
    use_pos = tabs is not None
    full = lambda shape: pl.BlockSpec(shape, lambda i, j: (0,) * len(shape))
    in_specs = [pl.BlockSpec((1, tm, ATT_KV_COLS), lambda i, j: (i, j, 0)),
                pl.BlockSpec((1, tm, ATT_Q_COLS), lambda i, j: (i, j, 0))]
    args = [px, pq]
    if use_pos:
        in_specs += [pl.BlockSpec((2, tm, LANES), lambda i, j: (0, j, 0))] * 2
        args += list(tabs)
    small = [lw['gqa_q_gain'].reshape(1, -1), lw['gqa_k_gain'].reshape(1, -1), lw['mla_q_a_gain'].reshape(1, -1),
             lw['mla_kv_a_gain'].reshape(1, -1), lw['q_b'], lw['kv_b_k'], lw['kv_b_v']]
    in_specs += [full(a.shape) for a in small]
    widths = [256, 256, 512, 512, 512, 512]
    return pl.pallas_call(
        functools.partial(_prep_kernel, use_pos=use_pos), grid=(b, l // tm), in_specs=in_specs,
        out_specs=[pl.BlockSpec((1, tm, w), lambda i, j: (i, j, 0)) for w in widths],
        out_shape=[jax.ShapeDtypeStruct((b, l, w), BF16) for w in widths],
        compiler_params=_cp(("arbitrary", "arbitrary")), name="attn_prep",
    )(*args, *small)


def _attn_kernel(*refs, n_src, scale, nsplit):
    q_ref, o_ref = refs[0], refs[-1]
    rows = q_ref.shape[1] // nsplit
    c2 = scale * math.log2(math.e)
    for c in range(nsplit):
        q = q_ref[0, c * rows:(c + 1) * rows, :]
        ss = [lax.dot_general(q, refs[1 + 2 * j][0], (((1,), (1,)), ((), ())), preferred_element_type=F32)
              for j in range(n_src)]
        m = jnp.max(ss[0], axis=-1, keepdims=True)
        for s in ss[1:]:
            m = jnp.maximum(m, jnp.max(s, axis=-1, keepdims=True))
        acc, den = None, None
        for j, s in enumerate(ss):
            p = jnp.exp2((s - m) * c2)
            d = jnp.sum(p, axis=-1, keepdims=True)
            a = _dot(p.astype(BF16), refs[2 + 2 * j][0])
            acc = a if acc is None else acc + a
            den = d if den is None else den + d
        o_ref[0, c * rows:(c + 1) * rows, :] = (acc / den).astype(o_ref.dtype)


def _attention(q, srcs, heads, kv_heads, dk):
    b, s, _ = q.shape
    r = heads // kv_heads
    tq = _t(s, 1024)
    in_specs = [pl.BlockSpec((1, tq, LANES), lambda bi, h, i: (bi, i, h))]
    args = [q]
    for k, v in srcs:
        lk = k.shape[1]
        in_specs += [pl.BlockSpec((1, lk, LANES), lambda bi, h, i: (bi, 0, h // r))] * 2
        args += [k, v]
    return pl.pallas_call(
        functools.partial(_attn_kernel, n_src=len(srcs), scale=float(dk) ** -0.5, nsplit=tq // 256 if tq % 256 == 0 else 1),
        grid=(b, heads, s // tq), in_specs=in_specs,
        out_specs=pl.BlockSpec((1, tq, LANES), lambda bi, h, i: (bi, i, h)),
        out_shape=jax.ShapeDtypeStruct((b, s, heads * LANES), BF16),
        compiler_params=_cp(("arbitrary", "arbitrary", "arbitrary")), name="attention",
    )(*args)


def _dwconv_kernel(*refs, taps, glu, post_ln, chunk, sub):
    if glu:
        ap, ac, an, bp, bc, bn, w_ref, b_ref, lg_ref, lb_ref, o_ref, scr = refs
        load = lambda a, g: a[0] * jax.nn.sigmoid(g[0])
        prev, cur, nxt = load(ap, bp), load(ac, bc), load(an, bn)
    else:
        ap, ac, an, w_ref, b_ref, o_ref, scr = refs
        prev, cur, nxt = ap[0], ac[0], an[0]
    i = pl.program_id(2)
    last = pl.num_programs(2) - 1
    pad = CONV_PAD
    scr[0, 0:pad, :] = jnp.where(i > 0, prev, 0.0)
    scr[0, pad:pad + chunk, :] = cur
    scr[0, pad + chunk:2 * pad + chunk, :] = jnp.where(i < last, nxt, 0.0)
    lo = (taps - 1) // 2
    span = chunk + 2 * pad - SUBLANES
    for r in sorted({(pad - lo + j) % SUBLANES for j in range(taps)} - {0}):
        scr[r, 0:span, :] = scr[0, pl.ds(r, span), :]
    for c in range(chunk // sub):
        acc = None
        for j in range(taps):
            o = c * sub + pad - lo + j
            term = w_ref[j:j + 1, :] * scr[o % SUBLANES, pl.ds(o - o % SUBLANES, sub), :]
            acc = term if acc is None else acc + term
        y = acc + b_ref[...]
        if post_ln:
            mu = jnp.mean(y, axis=-1, keepdims=True)
            yc = y - mu
            var = jnp.mean(yc * yc, axis=-1, keepdims=True)
            y = yc * lax.rsqrt(var + EPS) * lg_ref[...] + lb_ref[...]
            y = y * jax.nn.sigmoid(y)
        o_ref[0, c * sub:(c + 1) * sub, :] = y.astype(o_ref.dtype)


def _dwconv(x, w, bias, ncol, *, glu=False, ln=None, out_dtype=F32, chunk=256):
    b, l, _ = x.shape
    taps = w.shape[0]
    cw = 512
    chunk = _t(l, chunk)
    nblk = l // chunk
    per = chunk // CONV_PAD
    nsmall = l // CONV_PAD

    def views(coff):
        return [pl.BlockSpec((1, CONV_PAD, cw), lambda bi, c, i: (bi, jnp.maximum(i * per - 1, 0), c + coff)),
                pl.BlockSpec((1, chunk, cw), lambda bi, c, i: (bi, i, c + coff)),
                pl.BlockSpec((1, CONV_PAD, cw), lambda bi, c, i: (bi, jnp.minimum((i + 1) * per, nsmall - 1), c + coff))]

    in_specs = views(0)
    args = [x, x, x]
    if glu:
        in_specs += views(ncol)
        args += [x, x, x]
    vec = lambda: pl.BlockSpec((1, cw), lambda bi, c, i: (0, c))
    in_specs += [pl.BlockSpec((taps, cw), lambda bi, c, i: (0, c)), vec()]
    args += [w, bias.reshape(1, -1)]
    if ln is not None:
        in_specs += [vec(), vec()]
        args += [ln[0].reshape(1, -1), ln[1].reshape(1, -1)]
    return pl.pallas_call(
        functools.partial(_dwconv_kernel, taps=taps, glu=glu, post_ln=ln is not None, chunk=chunk, sub=32),
        grid=(b, ncol, nblk), in_specs=in_specs,
        out_specs=pl.BlockSpec((1, chunk, cw), lambda bi, c, i: (bi, i, c)),
        out_shape=jax.ShapeDtypeStruct((b, l, ncol * cw), out_dtype),
        scratch_shapes=[pltpu.VMEM((SUBLANES, chunk + 2 * CONV_PAD, cw), F32)],
        compiler_params=_cp(("arbitrary", "arbitrary", "arbitrary")), name="dwconv",
    )(*args)


def _dft_tables(l):
    n2 = 2 * l
    k = jnp.arange(l, dtype=jnp.int32)
    m = (k[:, None] * k[None, :]) & (n2 - 1)
    ang = m.astype(F32) * (2.0 * math.pi / n2)
    c = jnp.cos(ang)
    s = -jnp.sin(ang)
    nyq = jnp.where(k % 2 == 0, 1.0, -1.0).astype(F32)
    s = jnp.where((k == 0)[:, None], nyq[None, :], s)
    p = jnp.stack([c, s]).astype(BF16)
    return p, jnp.swapaxes(p, 1, 2)


def _hy_feats(l):
    pos = jnp.arange(l, dtype=F32)
    t01 = pos / (l - 1)
    bands = jnp.linspace(1e-4, HY_BANDS - 1, HY_BANDS, dtype=F32)
    ang = (2.0 * math.pi / l) * pos[:, None] * bands[None, :]
    feats = jnp.concatenate([t01[:, None], jnp.cos(ang), -jnp.sin(ang)], axis=-1)
    return jnp.pad(feats, ((0, 0), (0, LANES - feats.shape[1])))


def _hyfilt_kernel(f_ref, w1_ref, b1_ref, fr_ref, w2_ref, b2_ref, w3_ref, lr_ref, o_ref, *, tl):
    feats = f_ref[...]
    fr = fr_ref[...]
    hdot = lambda a, b: jnp.dot(a, b, preferred_element_type=F32, precision=HI)
    h = jnp.sin(fr * (hdot(feats, w1_ref[...]) + b1_ref[...]))
    h = jnp.sin(fr * (hdot(h, w2_ref[...]) + b2_ref[...]))
    h = hdot(h, w3_ref[...])
    h = h * jnp.exp(-feats[:, 0:1] * jnp.exp(lr_ref[...]))
    row = pl.program_id(0) * tl + lax.broadcasted_iota(jnp.int32, h.shape, 0)
    col = lax.broadcasted_iota(jnp.int32, h.shape, 1)
    is_bwd = ((col // HY_WIDTH) % 2) == 1
    o_ref[...] = jnp.where((row == 0) & is_bwd, 0.0, h).astype(o_ref.dtype)


def _hy_filters(l, lw):
    feats = _hy_feats(l)
    hid = HY_FILTER_HIDDEN
    padc = lambda a: jnp.pad(a, ((0, 0), (0, LANES - a.shape[1])))
    w1 = jnp.pad(lw['hf_w1'], ((0, LANES - lw['hf_w1'].shape[0]), (0, LANES - hid)))
    w2 = jnp.pad(lw['hf_w2'], ((0, LANES - hid), (0, LANES - hid)))
    w3 = jnp.pad(lw['hf_w3'], ((0, LANES - hid), (0, 0)))
    b1, b2, fr = padc(lw['hf_b1'].reshape(1, -1)), padc(lw['hf_b2'].reshape(1, -1)), padc(lw['hf_freq'].reshape(1, -1))
    lr = lw['hf_log_rate'].reshape(1, -1)
    nc = w3.shape[1]
    tl = _t(l, 256)
    full = lambda a: pl.BlockSpec(a.shape, lambda i: (0, 0))
    return pl.pallas_call(
        functools.partial(_hyfilt_kernel, tl=tl), grid=(l // tl,),
        in_specs=[pl.BlockSpec((tl, LANES), lambda i: (i, 0)), full(w1), full(b1), full(fr), full(w2), full(b2),
                  full(w3), full(lr)],
        out_specs=pl.BlockSpec((tl, nc), lambda i: (i, 0)),
        out_shape=jax.ShapeDtypeStruct((l, nc), BF16),
        compiler_params=_cp(("arbitrary",)), name="hy_filter",
    )(feats, w1, b1, fr, w2, b2, w3, lr)


def _hyspec_kernel(p_ref, h_ref, o_ref, *, tk, scale):
    hm = h_ref[...]
    ar = _dot(p_ref[0], hm)
    ai = _dot(p_ref[1], hm)
    c = HY_WIDTH
    row0 = (pl.program_id(0) * tk + lax.broadcasted_iota(jnp.int32, (tk, c), 0)) == 0
    s = jnp.where(row0, 0.5 * scale, scale)
    for n in range(HY_ORDER):
        o = 2 * n * c
        o_ref[n, 0] = (ar[:, o:o + c] + ar[:, o + c:o + 2 * c]) * s
        fi, bi = ai[:, o:o + c], ai[:, o + c:o + 2 * c]
        o_ref[n, 1] = jnp.where(row0, fi + bi, fi - bi) * s


def _hy_spectrum(p, filt):
    l = p.shape[1]
    tk = _t(l, 256)
    return pl.pallas_call(
        functools.partial(_hyspec_kernel, tk=tk, scale=1.0 / l), grid=(l // tk,),
        in_specs=[pl.BlockSpec((2, tk, l), lambda i: (0, i, 0)), pl.BlockSpec(filt.shape, lambda i: (0, 0))],
        out_specs=pl.BlockSpec((HY_ORDER, 2, tk, HY_WIDTH), lambda i: (0, 0, i, 0)),
        out_shape=jax.ShapeDtypeStruct((HY_ORDER, 2, l, HY_WIDTH), F32),
        compiler_params=_cp(("arbitrary",)), name="hy_spectrum",
    )(p, filt)


def _hyfwd_kernel(p_ref, z_ref, k_ref, y_ref, *, tk):
    z = z_ref[0].astype(BF16)
    xr = _dot(p_ref[0], z)
    xi = _dot(p_ref[1], z)
    kr, ki = k_ref[0, 0], k_ref[0, 1]
    row0 = (pl.program_id(0) * tk + lax.broadcasted_iota(jnp.int32, xr.shape, 0)) == 0
    xiki = xi * ki
    y_ref[0, 0] = (xr * kr - jnp.where(row0, 0.0, xiki)).astype(y_ref.dtype)
    y_ref[0, 1] = jnp.where(row0, xiki, xr * ki + xi * kr).astype(y_ref.dtype)


def _hy_forward(p, z, zcol, kf, order):
    b, l, _ = z.shape
    c = HY_WIDTH
    tk = _t(l, 1024)
    return pl.pallas_call(
        functools.partial(_hyfwd_kernel, tk=tk), grid=(l // tk, b),
        in_specs=[pl.BlockSpec((2, tk, l), lambda i, bi: (0, i, 0)),
                  pl.BlockSpec((1, l, c), lambda i, bi: (bi, 0, zcol)),
                  pl.BlockSpec((1, 2, tk, c), lambda i, bi: (order, 0, i, 0))],
        out_specs=pl.BlockSpec((1, 2, tk, c), lambda i, bi: (bi, 0, i, 0)),
        out_shape=jax.ShapeDtypeStruct((b, 2, l, c), BF16),
        compiler_params=_cp(("arbitrary", "arbitrary")), name="hy_forward",
    )(p, z, kf)


def _hyinv_kernel(pt_ref, y_ref, g_ref, z_ref, bias_ref, o_ref):
    conv = _dot(pt_ref[0], y_ref[0, 0]) + _dot(pt_ref[1], y_ref[0, 1])
    o_ref[0] = (g_ref[0] * (conv + bias_ref[...] * z_ref[0])).astype(o_ref.dtype)


def _hy_inverse(pt, y, u, gcol, z, zcol, bias, out_dtype):
    b, _, l, c = y.shape
    tn = _t(l, 1024)
    return pl.pallas_call(
        _hyinv_kernel, grid=(l // tn, b),
        in_specs=[pl.BlockSpec((2, tn, l), lambda i, bi: (0, i, 0)),
                  pl.BlockSpec((1, 2, l, c), lambda i, bi: (bi, 0, 0, 0)),
                  pl.BlockSpec((1, tn, c), lambda i, bi: (bi, i, gcol)),
                  pl.BlockSpec((1, tn, c), lambda i, bi: (bi, i, zcol)),
                  pl.BlockSpec((1, c), lambda i, bi: (0, 0))],
        out_specs=pl.BlockSpec((1, tn, c), lambda i, bi: (bi, i, 0)),
        out_shape=jax.ShapeDtypeStruct((b, l, c), out_dtype),
        compiler_params=_cp(("arbitrary", "arbitrary")), name="hy_inverse",
    )(pt, y, u, z, bias.reshape(1, c))


def _hyena(phy, lw, dft):
    p, pt, kf = dft
    u = _dwconv(phy, lw['hy_conv_w'], lw['hy_conv_b'], HY_ORDER + 1, chunk=1024)
    y = _hy_forward(p, u, 0, kf, 0)
    z = _hy_inverse(pt, y, u, 1, u, 0, lw['hy_bias'][0], F32)
    y = _hy_forward(p, z, 0, kf, 1)
    return _hy_inverse(pt, y, u, 2, z, 0, lw['hy_bias'][1], BF16)


def _merge_kernel(hx_ref, g0, g1, g2, g3, b0, b1, b2, b3, wbr_ref, o_ref, gb_ref, wb_ref):
    @pl.when(pl.program_id(1) == 0)
    def _():
        for n, wg in enumerate((g0, g1, g2, g3)):
            gb_ref[n] = wg[0].astype(gb_ref.dtype)
        wb_ref[...] = wbr_ref[0].astype(wb_ref.dtype)

    hx = hx_ref[...]
    acc = None
    for n, br in enumerate((b0, b1, b2, b3)):
        gate = jax.nn.sigmoid(_dot_nt(hx, gb_ref[n]))
        term = gate * _dot(br[...], wb_ref[n])
        acc = term if acc is None else acc + term
    o_ref[...] = acc.astype(o_ref.dtype)


def _merge(hx, w_gate, layer, row0, branches, w_br):
    m, d = hx.shape
    tm, tn = _t(m, 512), _t(d, 512)
    nj = d // tn
    bw = BRANCH_WIDTH
    in_specs = [pl.BlockSpec((tm, d), lambda j, i: (i, 0))]
    in_specs += [pl.BlockSpec((pl.Element(1), pl.Element(tn), pl.Element(d)),
                              functools.partial(lambda j, i, n: (layer, pl.multiple_of(row0 + n * d + j * tn, SUBLANES), 0), n=n),
                              pipeline_mode=pl.Buffered(1))
                 for n in range(N_BRANCH)]
    in_specs += [pl.BlockSpec((tm, bw), lambda j, i: (i, 0))] * N_BRANCH
    in_specs += [pl.BlockSpec((1, N_BRANCH, bw, tn), lambda j, i: (layer, 0, 0, j))]
    return pl.pallas_call(
        _merge_kernel, grid=(nj, m // tm), in_specs=in_specs,
        out_specs=pl.BlockSpec((tm, tn), lambda j, i: (i, j)),
        out_shape=jax.ShapeDtypeStruct((m, d), BF16),
        scratch_shapes=[pltpu.VMEM((N_BRANCH, tn, d), BF16), pltpu.VMEM((N_BRANCH, bw, tn), BF16)],
        compiler_params=_cp(("arbitrary", "arbitrary")), name="merge",
    )(hx, w_gate, w_gate, w_gate, w_gate, *branches, w_br)


def _topk_kernel(aff_ref, tri_ref, sp_ref, spt_ref, g_ref, *, cap):
    afft = aff_ref[0].T[:N_EXPERTS, :]
    bits = lax.bitcast_convert_type(afft, jnp.int32)

    def body(i, prefix):
        cand = prefix | jnp.left_shift(jnp.int32(1), 30 - i)
        cnt = jnp.sum((bits >= cand).astype(F32), axis=1, keepdims=True)
        return jnp.where(cnt >= cap, cand, prefix)

    thr = lax.fori_loop(0, 31, body, jnp.zeros((N_EXPERTS, 1), jnp.int32))
    gt = bits > thr
    eq = bits == thr
    need = cap - jnp.sum(gt.astype(F32), axis=1, keepdims=True)
    tri = tri_ref[...]
    rank_eq = _dot(eq.astype(BF16), tri)
    sel = gt | (eq & (rank_eq <= need))
    pos = _dot(sel.astype(BF16), tri) - 1.0
    sp = jnp.where(sel, pos, -1.0)
    sp_ref[0] = sp.astype(jnp.int32)
    n = sp.shape[1]
    padded = jnp.concatenate([sp, jnp.full((LANES - N_EXPERTS, n), -1.0, F32)], axis=0)
    spt_ref[0] = padded.T
    icap = g_ref.shape[2]
    slot = lax.broadcasted_iota(jnp.int32, (icap, n), 0).astype(F32)
    for e in range(N_EXPERTS):
        hit = slot == sp[e:e + 1, :]
        g_ref[0, e] = jnp.sum(jnp.where(hit, afft[e:e + 1, :], 0.0), axis=1, keepdims=True)


def _topk(aff, cap):
    b, n, _ = aff.shape
    t = jnp.arange(n, dtype=jnp.int32)
    tri = (t[:, None] <= t[None, :]).astype(BF16)
    return pl.pallas_call(
        functools.partial(_topk_kernel, cap=float(cap)), grid=(b,),
        in_specs=[pl.BlockSpec((1, n, LANES), lambda i: (i, 0, 0)), pl.BlockSpec((n, n), lambda i: (0, 0))],
        out_specs=[pl.BlockSpec((1, N_EXPERTS, n), lambda i: (i, 0, 0)), pl.BlockSpec((1, n, LANES), lambda i: (i, 0, 0)),
                   pl.BlockSpec((1, N_EXPERTS, cap, 1), lambda i: (i, 0, 0, 0))],
        out_shape=[jax.ShapeDtypeStruct((b, N_EXPERTS, n), jnp.int32), jax.ShapeDtypeStruct((b, n, LANES), F32),
                   jax.ShapeDtypeStruct((b, N_EXPERTS, cap, 1), F32)],
        compiler_params=_cp(("arbitrary",)), name="topk",
    )(aff, tri)


def _gather_kernel(sp_ref, h_ref, o_ref, *, cap):
    sp = sp_ref[0, 0]
    slot = lax.broadcasted_iota(jnp.int32, (cap, sp.shape[1]), 0)
    onehot = (slot == sp).astype(BF16)
    o_ref[0] = _dot(onehot, h_ref[0]).astype(o_ref.dtype)


def _gather(sp, h, cap):
    b, n, d = h.shape
    return pl.pallas_call(
        functools.partial(_gather_kernel, cap=cap), grid=(b, N_EXPERTS),
        in_specs=[pl.BlockSpec((1, 1, 1, n), lambda bi, e: (bi, e, 0, 0)), pl.BlockSpec((1, n, d), lambda bi, e: (bi, 0, 0))],
        out_specs=pl.BlockSpec((1, cap, d), lambda bi, e: (e, bi, 0)),
        out_shape=jax.ShapeDtypeStruct((N_EXPERTS, b * cap, d), BF16),
        compiler_params=_cp(("arbitrary", "arbitrary")), name="moe_gather",
    )(sp.reshape(b, N_EXPERTS, 1, n), h)


def _ffn_kernel(*refs, n_x):
    xs = refs[:n_x]
    gs = refs[n_x:2 * n_x]
    w1_ref, w3_ref, w2_ref = refs[2 * n_x:2 * n_x + 3]
    ys = refs[2 * n_x + 3:3 * n_x + 3]
    accs = refs[3 * n_x + 3:]
    f = pl.program_id(1)
    last = pl.num_programs(1) - 1
    w1, w3, w2 = w1_ref[0, 0].astype(BF16), w3_ref[0, 0].astype(BF16), w2_ref[0, 0].astype(BF16)
    for x_ref, g_ref, y_ref, acc in zip(xs, gs, ys, accs):
        rows = x_ref.shape[1]
        grp = 512 if rows % 512 == 0 else rows
        parts = []
        for r in range(rows // grp):
            x = x_ref[0, r * grp:(r + 1) * grp, :]
            a = _dot(x, w1)
            hid = (a * jax.nn.sigmoid(a) * _dot(x, w3)).astype(BF16)
            parts.append(_dot(hid, w2))

        @pl.when(f == 0)
        def _():
            for r, part in enumerate(parts):
                acc[r * grp:(r + 1) * grp, :] = part

        @pl.when(f > 0)
        def _():
            for r, part in enumerate(parts):
                acc[r * grp:(r + 1) * grp, :] += part

        @pl.when(f == last)
        def _():
            nb, _, cap, _ = y_ref.shape
            for bi in range(nb):
                y_ref[bi, 0] = (acc[bi * cap:(bi + 1) * cap, :] * g_ref[bi, 0]).astype(y_ref.dtype)


def _expert_ffn(xs, gs, w1, w3, w2, layer):
    _, e, d, ff = w1.shape
    tf = _t(ff, 256)
    in_specs = [pl.BlockSpec((1, x.shape[1], d), lambda ei, f: (ei, 0, 0)) for x in xs]
    in_specs += [pl.BlockSpec((g.shape[0], 1, g.shape[2], 1), lambda ei, f: (0, ei, 0, 0)) for g in gs]
    in_specs += [pl.BlockSpec((1, 1, d, tf), lambda ei, f: (layer, ei, 0, f))] * 2
    in_specs += [pl.BlockSpec((1, 1, tf, d), lambda ei, f: (layer, ei, f, 0))]
    return pl.pallas_call(
        functools.partial(_ffn_kernel, n_x=len(xs)), grid=(e, ff // tf), in_specs=in_specs,
        out_specs=[pl.BlockSpec((g.shape[0], 1, g.shape[2], d), lambda ei, f: (0, ei, 0, 0)) for g in gs],
        out_shape=[jax.ShapeDtypeStruct((g.shape[0], e, g.shape[2], d), BF16) for g in gs],
        scratch_shapes=[pltpu.VMEM((x.shape[1], d), F32) for x in xs],
        compiler_params=_cp(("arbitrary", "arbitrary")), name="expert_ffn",
    )(*xs, *gs, w1, w3, w2)


def _scatter_kernel(spt_ref, y_ref, x_ref, g_ref, o_ref, acc_ref, *, cap, kc):
    k = pl.program_id(2)
    tt = spt_ref.shape[1]
    blk = y_ref.shape[1]
    spt = spt_ref[0].astype(BF16)
    term = None
    for j in range(blk // kc):
        first = (k * blk + j * kc) // cap
        col = lax.broadcasted_iota(jnp.int32, (LANES, kc), 1)
        row = lax.broadcasted_iota(jnp.int32, (LANES, kc), 0)
        rep = (row == first + col // cap).astype(BF16)
        mine = _dot(spt, rep)
        slot = (lax.broadcasted_iota(jnp.int32, (tt, kc), 1) % cap).astype(F32)
        part = _dot((mine == slot).astype(BF16), y_ref[0, j * kc:(j + 1) * kc, :])
        term = part if term is None else term + part

    @pl.when(k == 0)
    def _():
        acc_ref[...] = term

    @pl.when(k > 0)
    def _():
        acc_ref[...] += term

    @pl.when(k == pl.num_programs(2) - 1)
    def _():
        o_ref[0] = x_ref[0] + g_ref[0] * acc_ref[...]


def _scatter(spt, y, x, g, cap):
    b, n, d = x.shape
    tt = _t(n, 512)
    kc = _t(N_EXPERTS * cap, 1024)
    blk = _t(N_EXPERTS * cap, 2 * kc)
    assert kc % cap == 0 and cap <= 256
    return pl.pallas_call(
        functools.partial(_scatter_kernel, cap=cap, kc=kc), grid=(b, n // tt, N_EXPERTS * cap // blk),
        in_specs=[pl.BlockSpec((1, tt, LANES), lambda bi, i, k: (bi, i, 0)),
                  pl.BlockSpec((1, blk, d), lambda bi, i, k: (bi, k, 0)),
                  pl.BlockSpec((1, tt, d), lambda bi, i, k: (bi, i, 0)),
                  pl.BlockSpec((1, 1, d), lambda bi, i, k: (bi, 0, 0))],
        out_specs=pl.BlockSpec((1, tt, d), lambda bi, i, k: (bi, i, 0)),
        out_shape=jax.ShapeDtypeStruct((b, n, d), F32),
        scratch_shapes=[pltpu.VMEM((tt, d), F32)],
        compiler_params=_cp(("arbitrary", "arbitrary", "arbitrary")), name="moe_scatter",
    )(spt, y, x, g)


def _rope_tables(s):
    rows = jnp.repeat(jnp.arange(s // GRID_W, dtype=jnp.int32), GRID_W).astype(F32)
    cols = jnp.tile(jnp.arange(GRID_W, dtype=jnp.int32), s // GRID_W).astype(F32)

    def half_tables(m):
        inv = ROPE_BASE ** (-jnp.arange(0, m, 2, dtype=F32) / m)
        out = []
        for pos in (rows, cols):
            ang = pos[:, None] * inv[None, :]
            c, sn = jnp.cos(ang), jnp.sin(ang)
            out.append((jnp.concatenate([c, c], axis=1), jnp.concatenate([-sn, sn], axis=1)))
        return (jnp.concatenate([out[0][0], out[1][0]], axis=1), jnp.concatenate([out[0][1], out[1][1]], axis=1))

    cg, sg = half_tables(GQA_HEAD_DIM // 2)
    cm, sm = half_tables(MLA_ROPE // 2)
    padm = lambda a, fill: jnp.concatenate(
        [jnp.full((s, MLA_NOPE), fill, F32), a, jnp.full((s, LANES - MLA_NOPE - MLA_ROPE), fill, F32)], axis=1)
    return jnp.stack([cg, sg]), jnp.stack([padm(cm, 1.0), padm(sm, 0.0)])


W_IN_Q = 2 * GQA_KV_HEADS * GQA_HEAD_DIM + MLA_KV_LORA + MLA_ROPE
W_IN_HY = W_IN_Q + GQA_HEADS * GQA_HEAD_DIM + MLA_Q_LORA
W_IN_CV = W_IN_HY + (HY_ORDER + 1) * HY_WIDTH
W_IN_GATE = W_IN_CV + 2 * CV_WIDTH


def _layer_weights(i, w):
    assert W_IN_Q + LANES - MLA_ROPE == ATT_KV_COLS and W_IN_HY - W_IN_Q == ATT_Q_COLS
    qb = w['mla_q_b'][i].reshape(MLA_Q_LORA, MLA_HEADS, MLA_NOPE + MLA_ROPE)
    qb = jnp.pad(qb, ((0, 0), (0, 0), (0, LANES - MLA_NOPE - MLA_ROPE))).reshape(MLA_Q_LORA, MLA_HEADS * LANES)
    kvb = w['mla_kv_b'][i].reshape(MLA_KV_LORA, MLA_HEADS, MLA_NOPE + MLA_V)
    kvb_k = jnp.pad(kvb[:, :, :MLA_NOPE], ((0, 0), (0, 0), (0, LANES - MLA_NOPE))).reshape(MLA_KV_LORA, MLA_HEADS * LANES)
    kvb_v = kvb[:, :, MLA_NOPE:].reshape(MLA_KV_LORA, MLA_HEADS * MLA_V)
    lw = {k: w[k][i] for k in ('gqa_q_gain', 'gqa_k_gain', 'mla_q_a_gain', 'mla_kv_a_gain', 'hy_conv_w', 'hy_conv_b',
                               'hf_w1', 'hf_b1', 'hf_w2', 'hf_b2', 'hf_w3', 'hf_freq', 'hf_log_rate', 'hy_bias',
                               'cv_w', 'cv_b', 'cv_ln_g', 'cv_ln_b', 'w_router', 'g_mix', 'g_ffn')}
    lw.update(layer=i, w_in_t=w['w_in_t'], q_b=qb, kv_b_k=kvb_k, kv_b_v=kvb_v, w_br=w['w_br'], w_out=w['w_out'])
    return lw


def _mixer_branches(hx, keys_extra, tabs, lw, dft):
    b, l, d = hx.shape
    hx2 = hx.reshape(b * l, d)
    proj = lambda row0, n, tn: _matmul_nt(hx2, lw['w_in_t'], lw['layer'], row0, n, F32, 1024, tn).reshape(b, l, n)
    px = proj(0, ATT_KV_COLS, ATT_KV_COLS)
    pq = proj(W_IN_Q, ATT_Q_COLS, ATT_Q_COLS)
    kg, vg, km, vm, qg, qm = _attn_prep(px, pq, tabs, lw)
    own = (kg, vg, km, vm)
    if dft is None:
        return None, own
    ek = keys_extra
    o_gqa = _attention(qg, [(kg, vg)] + ([(ek[0], ek[1])] if ek else []), GQA_HEADS, GQA_KV_HEADS, GQA_HEAD_DIM)
    o_mla = _attention(qm, [(km, vm)] + ([(ek[2], ek[3])] if ek else []), MLA_HEADS, MLA_HEADS, MLA_NOPE + MLA_ROPE)
    phy = proj(W_IN_HY, W_IN_CV - W_IN_HY, 768)
    o_hy = _hyena(phy, lw, dft)
    pcv = proj(W_IN_CV, W_IN_GATE - W_IN_CV, 1024)
    o_cv = _dwconv(pcv, lw['cv_w'], lw['cv_b'], 1, glu=True, ln=(lw['cv_ln_g'], lw['cv_ln_b']), out_dtype=BF16)
    flat = lambda a: a.reshape(b * l, -1)
    return [flat(o_hy), flat(o_gqa), flat(o_mla), flat(o_cv)], own


def kernel(x, c, ctx, c_ctx, w_mod, b_mod, g_mix, w_in, gqa_q_gain, gqa_k_gain, mla_q_a_gain, mla_q_b, mla_kv_a_gain, mla_kv_b, hy_conv_w, hy_conv_b, hf_w1, hf_b1, hf_w2, hf_b2, hf_w3, hf_freq, hf_log_rate, hy_bias, cv_w, cv_b, cv_ln_g, cv_ln_b, w_br, w_out, g_ffn, w_router, w1, w3, w2, g_final):
    w = dict(g_mix=g_mix, w_in_t=jnp.swapaxes(w_in, 1, 2), gqa_q_gain=gqa_q_gain, gqa_k_gain=gqa_k_gain, mla_q_a_gain=mla_q_a_gain,
             mla_q_b=mla_q_b, mla_kv_a_gain=mla_kv_a_gain, mla_kv_b=mla_kv_b, hy_conv_w=hy_conv_w, hy_conv_b=hy_conv_b,
             hf_w1=hf_w1, hf_b1=hf_b1, hf_w2=hf_w2, hf_b2=hf_b2, hf_w3=hf_w3, hf_freq=hf_freq, hf_log_rate=hf_log_rate,
             hy_bias=hy_bias, cv_w=cv_w, cv_b=cv_b, cv_ln_g=cv_ln_g, cv_ln_b=cv_ln_b, w_br=w_br, w_out=w_out,
             g_ffn=g_ffn, w_router=w_router)
    bsz, s, d = x.shape
    lc = ctx.shape[1]
    depth = w_mod.shape[0]
    assert bsz < 8
    cvec = jnp.concatenate([c, c_ctx[None, :], jnp.zeros((8 - bsz - 1, d), F32)], axis=0)
    mod = _mod_all(cvec, w_mod, b_mod)
    tabs = _rope_tables(s)
    p_x, pt_x = _dft_tables(s)
    p_c, pt_c = _dft_tables(lc)
    cap_x = CAPACITY_FACTOR * s // N_EXPERTS
    cap_c = CAPACITY_FACTOR * lc // N_EXPERTS
    xc = ctx
    for i in range(depth):
        last = i == depth - 1
        lw = _layer_weights(i, w)
        mx = [mod[i, :bsz, j * d:(j + 1) * d].reshape(bsz, 1, d) for j in range(6)]
        mc = [jnp.broadcast_to(mod[i, bsz:bsz + 1, j * d:(j + 1) * d].reshape(1, 1, d), (bsz, 1, d)) for j in range(6)]

        hc = _norm_mod(xc, lw['g_mix'], mc[0], mc[1])
        dft_c = None if last else (p_c, pt_c, _hy_spectrum(p_c, _hy_filters(lc, lw)))
        br_c, keys_c = _mixer_branches(hc, None, None, lw, dft_c)
        hx = _norm_mod(x, lw['g_mix'], mx[0], mx[1])
        dft_x = (p_x, pt_x, _hy_spectrum(p_x, _hy_filters(s, lw)))
        br_x, _ = _mixer_branches(hx, keys_c, tabs, lw, dft_x)

        merged = _merge(hx.reshape(bsz * s, d), lw['w_in_t'], i, W_IN_GATE, br_x, lw['w_br']).reshape(bsz, s, d)
        x = _matmul_residual(merged, lw['w_out'], i, x, mx[2])
        hf, aff = _norm_mod(x, lw['g_ffn'], mx[3], mx[4], lw['w_router'])
        sp, spt, gslot = _topk(aff, cap_x)
        xs, gs = [_gather(sp, hf, cap_x)], [gslot]
        if not last:
            merged_c = _merge(hc.reshape(bsz * lc, d), lw['w_in_t'], i, W_IN_GATE, br_c, lw['w_br']).reshape(bsz, lc, d)
            xc = _matmul_residual(merged_c, lw['w_out'], i, xc, mc[2])
            hfc, aff_c = _norm_mod(xc, lw['g_ffn'], mc[3], mc[4], lw['w_router'])
            sp_c, spt_c, gslot_c = _topk(aff_c, cap_c)
            xs.append(_gather(sp_c, hfc, cap_c))
            gs.append(gslot_c)
        ys = _expert_ffn(xs, gs, w1, w3, w2, i)
        x = _scatter(spt, ys[0].reshape(bsz, N_EXPERTS * cap_x, d), x, mx[5], cap_x)
        if not last:
            xc = _scatter(spt_c, ys[1].reshape(bsz, N_EXPERTS * cap_c, d), xc, mc[5], cap_c)
    return _final_norm(x, g_final)
```

```python
import functools
import math

import numpy as np
import jax
import jax.numpy as jnp
from jax import lax
from jax.experimental import pallas as pl
from jax.experimental.pallas import tpu as pltpu

F32 = jnp.float32
BF16 = jnp.bfloat16
HI = lax.Precision.HIGHEST

GRID_W = 64
EPS = 1e-6
ROPE_BASE = 10000.0
GQA_HEADS, GQA_KV_HEADS, GQA_HEAD_DIM = 4, 2, 128
MLA_HEADS, MLA_Q_LORA, MLA_KV_LORA, MLA_NOPE, MLA_ROPE, MLA_V = 4, 384, 256, 64, 32, 128
HY_WIDTH, HY_ORDER, HY_BANDS, HY_FILTER_HIDDEN = 512, 2, 16, 64
CV_WIDTH, CV_KERNEL = 512, 31
N_BRANCH, BRANCH_WIDTH = 4, 512
N_EXPERTS, EXPERT_FF, CAPACITY_FACTOR = 16, 1024, 2
LANES = 128
SUBLANES = 8
BF16_ROWS = 16
CONV_PAD = 16
VMEM_LIMIT = 56 * 1024 * 1024


def _cp(sem, vmem=VMEM_LIMIT):
    return pltpu.CompilerParams(dimension_semantics=sem, vmem_limit_bytes=vmem)


def _t(n, pref):
    return pref if n % pref == 0 else n


def _dot(a, b):
    return jnp.dot(a, b, preferred_element_type=F32)


def _mod_kernel(c_ref, w_ref, b_ref, o_ref):
    c = c_ref[...]
    a = (c * jax.nn.sigmoid(c)).astype(BF16)
    o_ref[0] = _dot(a, w_ref[0].astype(BF16)) + b_ref[0]


def _mod_all(cvec, w_mod, b_mod):
    depth, d, n6 = w_mod.shape
    tn = _t(n6, 1024)
    return pl.pallas_call(
        _mod_kernel, grid=(depth, n6 // tn),
        in_specs=[pl.BlockSpec((8, d), lambda l, j: (0, 0)),
                  pl.BlockSpec((1, d, tn), lambda l, j: (l, 0, j)),
                  pl.BlockSpec((1, 1, tn), lambda l, j: (l, 0, j))],
        out_specs=pl.BlockSpec((1, 8, tn), lambda l, j: (l, 0, j)),
        out_shape=jax.ShapeDtypeStruct((depth, 8, n6), F32),
        compiler_params=_cp(("arbitrary", "arbitrary")), name="mod",
    )(cvec, w_mod, b_mod.reshape(depth, 1, n6))


def _norm_h(x_ref, g_ref, sh_ref, sc_ref):
    x = x_ref[0]
    y = x * lax.rsqrt(jnp.mean(x * x, axis=-1, keepdims=True) + EPS)
    return (y * g_ref[...]) * (1.0 + sc_ref[0]) + sh_ref[0]


def _norm_mod_kernel(x_ref, g_ref, sh_ref, sc_ref, o_ref):
    o_ref[0] = _norm_h(x_ref, g_ref, sh_ref, sc_ref).astype(o_ref.dtype)


def _norm_router_kernel(x_ref, g_ref, sh_ref, sc_ref, wr_ref, o_ref, aff_ref):
    h = _norm_h(x_ref, g_ref, sh_ref, sc_ref)
    h_hi = h.astype(BF16)
    o_ref[0] = h_hi.astype(o_ref.dtype)
    w = wr_ref[...]
    w_hi = w.astype(BF16)
    h_lo = (h - h_hi.astype(F32)).astype(BF16)
    w_lo = (w - w_hi.astype(F32)).astype(BF16)
    logits = _dot(h_hi, w_hi) + (_dot(h_lo, w_hi) + _dot(h_hi, w_lo))
    lane = lax.broadcasted_iota(jnp.int32, logits.shape, 1)
    logits = jnp.where(lane < N_EXPERTS, logits, -1e30)
    e = jnp.exp(logits - jnp.max(logits, axis=-1, keepdims=True))
    aff_ref[0] = e / jnp.sum(e, axis=-1, keepdims=True)


def _norm_mod(x, g, sh, sc, w_router=None):
    b, l, d = x.shape
    tm = _t(l, 256)
    specs = [pl.BlockSpec((1, tm, d), lambda i, j: (i, j, 0)),
             pl.BlockSpec((1, d), lambda i, j: (0, 0)),
             pl.BlockSpec((1, 1, d), lambda i, j: (i, 0, 0)),
             pl.BlockSpec((1, 1, d), lambda i, j: (i, 0, 0))]
    h_spec = pl.BlockSpec((1, tm, d), lambda i, j: (i, j, 0))
    h_shape = jax.ShapeDtypeStruct((b, l, d), BF16)
    if w_router is None:
        return pl.pallas_call(_norm_mod_kernel, grid=(b, l // tm), in_specs=specs, out_specs=h_spec,
                              out_shape=h_shape, compiler_params=_cp(("arbitrary", "arbitrary")),
                              name="norm_mod")(x, g.reshape(1, d), sh, sc)
    wr = jnp.pad(w_router, ((0, 0), (0, LANES - w_router.shape[1])))
    return pl.pallas_call(
        _norm_router_kernel, grid=(b, l // tm),
        in_specs=specs + [pl.BlockSpec((d, LANES), lambda i, j: (0, 0))],
        out_specs=[h_spec, pl.BlockSpec((1, tm, LANES), lambda i, j: (i, j, 0))],
        out_shape=[h_shape, jax.ShapeDtypeStruct((b, l, LANES), F32)],
        compiler_params=_cp(("arbitrary", "arbitrary")), name="norm_router",
    )(x, g.reshape(1, d), sh, sc, wr)


def _final_norm_kernel(x_ref, g_ref, o_ref):
    x = x_ref[0]
    o_ref[0] = x * lax.rsqrt(jnp.mean(x * x, axis=-1, keepdims=True) + EPS) * g_ref[...]


def _final_norm(x, g):
    b, l, d = x.shape
    tm = _t(l, 256)
    return pl.pallas_call(
        _final_norm_kernel, grid=(b, l // tm),
        in_specs=[pl.BlockSpec((1, tm, d), lambda i, j: (i, j, 0)), pl.BlockSpec((1, d), lambda i, j: (0, 0))],
        out_specs=pl.BlockSpec((1, tm, d), lambda i, j: (i, j, 0)),
        out_shape=jax.ShapeDtypeStruct((b, l, d), F32),
        compiler_params=_cp(("arbitrary", "arbitrary")), name="final_norm",
    )(x, g.reshape(1, d))


def _dot_nt(a, w):
    return lax.dot_general(a, w, (((1,), (1,)), ((), ())), preferred_element_type=F32)


def _mm_kernel(a_ref, w_ref, o_ref, wb_ref):
    @pl.when(pl.program_id(1) == 0)
    def _():
        wb_ref[...] = w_ref[0].astype(wb_ref.dtype)

    o_ref[...] = _dot_nt(a_ref[...], wb_ref[...]).astype(o_ref.dtype)


def _matmul_nt(a, w, layer, row0, n, out_dtype, tm, tn):
    m, k = a.shape
    tm, tn = _t(m, tm), _t(n, tn)
    return pl.pallas_call(
        _mm_kernel, grid=(n // tn, m // tm),
        in_specs=[pl.BlockSpec((tm, k), lambda j, i: (i, 0)),
                  pl.BlockSpec((pl.Element(1), pl.Element(tn), pl.Element(k)),
                               lambda j, i: (layer, pl.multiple_of(row0 + j * tn, SUBLANES), 0))],
        out_specs=pl.BlockSpec((tm, tn), lambda j, i: (i, j)),
        out_shape=jax.ShapeDtypeStruct((m, n), out_dtype),
        scratch_shapes=[pltpu.VMEM((tn, k), BF16)],
        compiler_params=_cp(("arbitrary", "arbitrary")), name="matmul",
    )(a, w)


def _mm_res_kernel(a_ref, w_ref, x_ref, g_ref, o_ref, wb_ref):
    @pl.when((pl.program_id(1) == 0) & (pl.program_id(2) == 0))
    def _():
        wb_ref[...] = w_ref[0].astype(wb_ref.dtype)

    o_ref[0] = x_ref[0] + g_ref[0] * _dot(a_ref[0], wb_ref[...])


def _matmul_residual(a, w, layer, x, g):
    b, l, k = a.shape
    n = w.shape[2]
    tm, tn = _t(l, 1024), _t(n, 512)
    return pl.pallas_call(
        _mm_res_kernel, grid=(n // tn, b, l // tm),
        in_specs=[pl.BlockSpec((1, tm, k), lambda j, bi, i: (bi, i, 0)),
                  pl.BlockSpec((1, k, tn), lambda j, bi, i: (layer, 0, j)),
                  pl.BlockSpec((1, tm, tn), lambda j, bi, i: (bi, i, j)),
                  pl.BlockSpec((1, 1, tn), lambda j, bi, i: (bi, 0, j))],
        out_specs=pl.BlockSpec((1, tm, tn), lambda j, bi, i: (bi, i, j)),
        out_shape=jax.ShapeDtypeStruct((b, l, n), F32),
        scratch_shapes=[pltpu.VMEM((k, tn), BF16)],
        compiler_params=_cp(("arbitrary", "arbitrary", "arbitrary")), name="matmul_residual",
    )(a, w, x, g)


ATT_K, ATT_V, ATT_KVA, ATT_KPE, ATT_KV_COLS = 0, 256, 512, 768, 896
ATT_Q, ATT_QA, ATT_Q_COLS = 0, 512, 896


def _rope(xh, tab_ref, half):
    lane = lax.broadcasted_iota(jnp.int32, xh.shape, 1)
    partner = jnp.where((lane % (2 * half)) < half, pltpu.roll(xh, LANES - half, 1), pltpu.roll(xh, half, 1))
    return xh * tab_ref[0] + partner * tab_ref[1]


def _rms(x, gain):
    return x * lax.rsqrt(jnp.mean(x * x, axis=-1, keepdims=True) + EPS) * gain


def _prep_kernel(*refs, use_pos):
    hx_ref, wkv_ref, wq_ref = refs[:3]
    refs = refs[3:]
    if use_pos:
        tg_ref, tm_ref = refs[:2]
        refs = refs[2:]
    (gq_ref, gk_ref, gqa_ref, gkva_ref, qb_ref, kvbk_ref, kvbv_ref, kg_ref, vg_ref, km_ref, vm_ref, qg_ref, qm_ref,
     wkvb_ref, wqb_ref) = refs

    @pl.when((pl.program_id(0) == 0) & (pl.program_id(1) == 0))
    def _():
        wkvb_ref[...] = wkv_ref[0].astype(wkvb_ref.dtype)
        wqb_ref[...] = wq_ref[0].astype(wqb_ref.dtype)

    hx = hx_ref[0]
    p = _dot_nt(hx, wkvb_ref[...])
    pq = _dot_nt(hx, wqb_ref[...])
    hd = GQA_HEAD_DIM
    for h in range(GQA_KV_HEADS):
        kh = _rms(p[:, ATT_K + h * hd:ATT_K + (h + 1) * hd], gk_ref[...])
        if use_pos:
            kh = _rope(kh, tg_ref, 32)
        kg_ref[0, :, h * hd:(h + 1) * hd] = kh.astype(BF16)
    vg_ref[0] = p[:, ATT_V:ATT_V + GQA_KV_HEADS * hd].astype(BF16)
    for h in range(GQA_HEADS):
        qh = _rms(pq[:, ATT_Q + h * hd:ATT_Q + (h + 1) * hd], gq_ref[...])
        if use_pos:
            qh = _rope(qh, tg_ref, 32)
        qg_ref[0, :, h * hd:(h + 1) * hd] = qh.astype(BF16)
    kvn = _rms(p[:, ATT_KVA:ATT_KVA + MLA_KV_LORA], gkva_ref[...]).astype(BF16)
    knope = _dot(kvn, kvbk_ref[...].astype(BF16))
    vm_ref[0] = _dot(kvn, kvbv_ref[...].astype(BF16)).astype(BF16)
    kpe = p[:, ATT_KPE:ATT_KPE + LANES]
    lane = lax.broadcasted_iota(jnp.int32, kpe.shape, 1)
    kpe = pltpu.roll(jnp.where(lane < MLA_ROPE, kpe, 0.0), MLA_NOPE, 1)
    if use_pos:
        kpe = _rope(kpe, tm_ref, 8)
    qan = _rms(pq[:, ATT_QA:ATT_QA + MLA_Q_LORA], gqa_ref[...]).astype(BF16)
    qmf = _dot(qan, qb_ref[...].astype(BF16))
    for h in range(MLA_HEADS):
        km_ref[0, :, h * LANES:(h + 1) * LANES] = (knope[:, h * LANES:(h + 1) * LANES] + kpe).astype(BF16)
        qh = qmf[:, h * LANES:(h + 1) * LANES]
        if use_pos:
            qh = _rope(qh, tm_ref, 8)
        qm_ref[0, :, h * LANES:(h + 1) * LANES] = qh.astype(BF16)


def _attn_prep(hx, tabs, lw):
    b, l, d = hx.shape
    tm = _t(l, 512)
    use_pos = tabs is not None
    full = lambda shape: pl.BlockSpec(shape, lambda i, j: (0,) * len(shape))
    layer = lw['layer']
    window = lambda row0, n: pl.BlockSpec((pl.Element(1), pl.Element(n), pl.Element(d)),
                                          lambda i, j: (layer, row0, 0), pipeline_mode=pl.Buffered(1))
    in_specs = [pl.BlockSpec((1, tm, d), lambda i, j: (i, j, 0)), window(0, ATT_KV_COLS), window(W_IN_Q, ATT_Q_COLS)]
    args = [hx, lw['w_in_t'], lw['w_in_t']]
    if use_pos:
        in_specs += [pl.BlockSpec((2, tm, LANES), lambda i, j: (0, j, 0))] * 2
        args += list(tabs)
    small = [lw['gqa_q_gain'].reshape(1, -1), lw['gqa_k_gain'].reshape(1, -1), lw['mla_q_a_gain'].reshape(1, -1),
             lw['mla_kv_a_gain'].reshape(1, -1), lw['q_b'], lw['kv_b_k'], lw['kv_b_v']]
    in_specs += [full(a.shape) for a in small]
    widths = [256, 256, 512, 512, 512, 512]
    return pl.pallas_call(
        functools.partial(_prep_kernel, use_pos=use_pos), grid=(b, l // tm), in_specs=in_specs,
        out_specs=[pl.BlockSpec((1, tm, w), lambda i, j: (i, j, 0)) for w in widths],
        out_shape=[jax.ShapeDtypeStruct((b, l, w), BF16) for w in widths],
        scratch_shapes=[pltpu.VMEM((ATT_KV_COLS, d), BF16), pltpu.VMEM((ATT_Q_COLS, d), BF16)],
        compiler_params=_cp(("arbitrary", "arbitrary")), name="attn_prep",
    )(*args, *small)


def _attn_kernel(*refs, n_src, scale, nsplit):
    q_ref, o_ref = refs[0], refs[-1]
    rows = q_ref.shape[1] // nsplit
    c2 = scale * math.log2(math.e)
    for c in range(nsplit):
        q = q_ref[0, c * rows:(c + 1) * rows, :]
        ss = [lax.dot_general(q, refs[1 + 2 * j][0], (((1,), (1,)), ((), ())), preferred_element_type=F32)
              for j in range(n_src)]
        m = jnp.max(ss[0], axis=-1, keepdims=True)
        for s in ss[1:]:
            m = jnp.maximum(m, jnp.max(s, axis=-1, keepdims=True))
        acc, den = None, None
        for j, s in enumerate(ss):
            p = jnp.exp2((s - m) * c2)
            d = jnp.sum(p, axis=-1, keepdims=True)
            a = _dot(p.astype(BF16), refs[2 + 2 * j][0])
            acc = a if acc is None else acc + a
            den = d if den is None else den + d
        o_ref[0, c * rows:(c + 1) * rows, :] = (acc / den).astype(o_ref.dtype)


def _attention(q, srcs, heads, kv_heads, dk):
    b, s, _ = q.shape
    r = heads // kv_heads
    tq = _t(s, 1024)
    in_specs = [pl.BlockSpec((1, tq, LANES), lambda bi, h, i: (bi, i, h))]
    args = [q]
    for k, v in srcs:
        lk = k.shape[1]
        in_specs += [pl.BlockSpec((1, lk, LANES), lambda bi, h, i: (bi, 0, h // r))] * 2
        args += [k, v]
    return pl.pallas_call(
        functools.partial(_attn_kernel, n_src=len(srcs), scale=float(dk) ** -0.5, nsplit=tq // 256 if tq % 256 == 0 else 1),
        grid=(b, heads, s // tq), in_specs=in_specs,
        out_specs=pl.BlockSpec((1, tq, LANES), lambda bi, h, i: (bi, i, h)),
        out_shape=jax.ShapeDtypeStruct((b, s, heads * LANES), BF16),
        compiler_params=_cp(("arbitrary", "arbitrary", "arbitrary")), name="attention",
    )(*args)


def _dwconv_kernel(*refs, taps, glu, post_ln, chunk, sub):
    if glu:
        ap, ac, an, bp, bc, bn, w_ref, b_ref, lg_ref, lb_ref, o_ref, scr = refs
        load = lambda a, g: a[0] * jax.nn.sigmoid(g[0])
        prev, cur, nxt = load(ap, bp), load(ac, bc), load(an, bn)
    else:
        ap, ac, an, w_ref, b_ref, o_ref, scr = refs
        prev, cur, nxt = ap[0], ac[0], an[0]
    i = pl.program_id(2)
    last = pl.num_programs(2) - 1
    pad = CONV_PAD
    scr[0, 0:pad, :] = jnp.where(i > 0, prev, 0.0)
    scr[0, pad:pad + chunk, :] = cur
    scr[0, pad + chunk:2 * pad + chunk, :] = jnp.where(i < last, nxt, 0.0)
    lo = (taps - 1) // 2
    span = chunk + 2 * pad - SUBLANES
    for r in sorted({(pad - lo + j) % SUBLANES for j in range(taps)} - {0}):
        scr[r, 0:span, :] = scr[0, pl.ds(r, span), :]
    for c in range(chunk // sub):
        acc = None
        for j in range(taps):
            o = c * sub + pad - lo + j
            term = w_ref[j:j + 1, :] * scr[o % SUBLANES, pl.ds(o - o % SUBLANES, sub), :]
            acc = term if acc is None else acc + term
        y = acc + b_ref[...]
        if post_ln:
            mu = jnp.mean(y, axis=-1, keepdims=True)
            yc = y - mu
            var = jnp.mean(yc * yc, axis=-1, keepdims=True)
            y = yc * lax.rsqrt(var + EPS) * lg_ref[...] + lb_ref[...]
            y = y * jax.nn.sigmoid(y)
        o_ref[0, c * sub:(c + 1) * sub, :] = y.astype(o_ref.dtype)


def _dwconv(x, w, bias, ncol, *, glu=False, ln=None, out_dtype=F32, chunk=256):
    b, l, _ = x.shape
    taps = w.shape[0]
    cw = 512
    chunk = _t(l, chunk)
    nblk = l // chunk
    per = chunk // CONV_PAD
    nsmall = l // CONV_PAD

    def views(coff):
        return [pl.BlockSpec((1, CONV_PAD, cw), lambda bi, c, i: (bi, jnp.maximum(i * per - 1, 0), c + coff)),
                pl.BlockSpec((1, chunk, cw), lambda bi, c, i: (bi, i, c + coff)),
                pl.BlockSpec((1, CONV_PAD, cw), lambda bi, c, i: (bi, jnp.minimum((i + 1) * per, nsmall - 1), c + coff))]

    in_specs = views(0)
    args = [x, x, x]
    if glu:
        in_specs += views(ncol)
        args += [x, x, x]
    vec = lambda: pl.BlockSpec((1, cw), lambda bi, c, i: (0, c))
    in_specs += [pl.BlockSpec((taps, cw), lambda bi, c, i: (0, c)), vec()]
    args += [w, bias.reshape(1, -1)]
    if ln is not None:
        in_specs += [vec(), vec()]
        args += [ln[0].reshape(1, -1), ln[1].reshape(1, -1)]
    return pl.pallas_call(
        functools.partial(_dwconv_kernel, taps=taps, glu=glu, post_ln=ln is not None, chunk=chunk, sub=32),
        grid=(b, ncol, nblk), in_specs=in_specs,
        out_specs=pl.BlockSpec((1, chunk, cw), lambda bi, c, i: (bi, i, c)),
        out_shape=jax.ShapeDtypeStruct((b, l, ncol * cw), out_dtype),
        scratch_shapes=[pltpu.VMEM((SUBLANES, chunk + 2 * CONV_PAD, cw), F32)],
        compiler_params=_cp(("arbitrary", "arbitrary", "arbitrary")), name="dwconv",
    )(*args)


def _dft_tables(l):
    n2 = 2 * l
    k = jnp.arange(l, dtype=jnp.int32)
    m = (k[:, None] * k[None, :]) & (n2 - 1)
    ang = m.astype(F32) * (2.0 * math.pi / n2)
    c = jnp.cos(ang)
    s = -jnp.sin(ang)
    nyq = jnp.where(k % 2 == 0, 1.0, -1.0).astype(F32)
    s = jnp.where((k == 0)[:, None], nyq[None, :], s)
    p = jnp.stack([c, s]).astype(BF16)
    return p, jnp.swapaxes(p, 1, 2)


def _hy_feats(l):
    pos = jnp.arange(l, dtype=F32)
    t01 = pos / (l - 1)
    bands = jnp.linspace(1e-4, HY_BANDS - 1, HY_BANDS, dtype=F32)
    ang = (2.0 * math.pi / l) * pos[:, None] * bands[None, :]
    feats = jnp.concatenate([t01[:, None], jnp.cos(ang), -jnp.sin(ang)], axis=-1)
    return jnp.pad(feats, ((0, 0), (0, LANES - feats.shape[1])))


def _hyfilt_kernel(f_ref, w1_ref, b1_ref, fr_ref, w2_ref, b2_ref, w3_ref, lr_ref, o_ref, *, tl):
    feats = f_ref[...]
    fr = fr_ref[...]
    hdot = lambda a, b: jnp.dot(a, b, preferred_element_type=F32, precision=HI)
    h = jnp.sin(fr * (hdot(feats, w1_ref[...]) + b1_ref[...]))
    h = jnp.sin(fr * (hdot(h, w2_ref[...]) + b2_ref[...]))
    h = hdot(h, w3_ref[...])
    h = h * jnp.exp(-feats[:, 0:1] * jnp.exp(lr_ref[...]))
    row = pl.program_id(0) * tl + lax.broadcasted_iota(jnp.int32, h.shape, 0)
    col = lax.broadcasted_iota(jnp.int32, h.shape, 1)
    is_bwd = ((col // HY_WIDTH) % 2) == 1
    o_ref[...] = jnp.where((row == 0) & is_bwd, 0.0, h).astype(o_ref.dtype)


def _hy_filters(l, lw):
    feats = _hy_feats(l)
    hid = HY_FILTER_HIDDEN
    padc = lambda a: jnp.pad(a, ((0, 0), (0, LANES - a.shape[1])))
    w1 = jnp.pad(lw['hf_w1'], ((0, LANES - lw['hf_w1'].shape[0]), (0, LANES - hid)))
    w2 = jnp.pad(lw['hf_w2'], ((0, LANES - hid), (0, LANES - hid)))
    w3 = jnp.pad(lw['hf_w3'], ((0, LANES - hid), (0, 0)))
    b1, b2, fr = padc(lw['hf_b1'].reshape(1, -1)), padc(lw['hf_b2'].reshape(1, -1)), padc(lw['hf_freq'].reshape(1, -1))
    lr = lw['hf_log_rate'].reshape(1, -1)
    nc = w3.shape[1]
    tl = _t(l, 256)
    full = lambda a: pl.BlockSpec(a.shape, lambda i: (0, 0))
    return pl.pallas_call(
        functools.partial(_hyfilt_kernel, tl=tl), grid=(l // tl,),
        in_specs=[pl.BlockSpec((tl, LANES), lambda i: (i, 0)), full(w1), full(b1), full(fr), full(w2), full(b2),
                  full(w3), full(lr)],
        out_specs=pl.BlockSpec((tl, nc), lambda i: (i, 0)),
        out_shape=jax.ShapeDtypeStruct((l, nc), BF16),
        compiler_params=_cp(("arbitrary",)), name="hy_filter",
    )(feats, w1, b1, fr, w2, b2, w3, lr)


def _hyspec_kernel(p_ref, h_ref, o_ref, *, tk, scale):
    hm = h_ref[...]
    ar = _dot(p_ref[0], hm)
    ai = _dot(p_ref[1], hm)
    c = HY_WIDTH
    row0 = (pl.program_id(0) * tk + lax.broadcasted_iota(jnp.int32, (tk, c), 0)) == 0
    s = jnp.where(row0, 0.5 * scale, scale)
    for n in range(HY_ORDER):
        o = 2 * n * c
        o_ref[n, 0] = (ar[:, o:o + c] + ar[:, o + c:o + 2 * c]) * s
        fi, bi = ai[:, o:o + c], ai[:, o + c:o + 2 * c]
        o_ref[n, 1] = jnp.where(row0, fi + bi, fi - bi) * s


def _hy_spectrum(p, filt):
    l = p.shape[1]
    tk = _t(l, 256)
    return pl.pallas_call(
        functools.partial(_hyspec_kernel, tk=tk, scale=1.0 / l), grid=(l // tk,),
        in_specs=[pl.BlockSpec((2, tk, l), lambda i: (0, i, 0)), pl.BlockSpec(filt.shape, lambda i: (0, 0))],
        out_specs=pl.BlockSpec((HY_ORDER, 2, tk, HY_WIDTH), lambda i: (0, 0, i, 0)),
        out_shape=jax.ShapeDtypeStruct((HY_ORDER, 2, l, HY_WIDTH), F32),
        compiler_params=_cp(("arbitrary",)), name="hy_spectrum",
    )(p, filt)


def _hyfwd_kernel(p_ref, z_ref, k_ref, y_ref, *, tk):
    z = z_ref[0].astype(BF16)
    xr = _dot(p_ref[0], z)
    xi = _dot(p_ref[1], z)
    kr, ki = k_ref[0, 0], k_ref[0, 1]
    row0 = (pl.program_id(0) * tk + lax.broadcasted_iota(jnp.int32, xr.shape, 0)) == 0
    xiki = xi * ki
    y_ref[0, 0] = (xr * kr - jnp.where(row0, 0.0, xiki)).astype(y_ref.dtype)
    y_ref[0, 1] = jnp.where(row0, xiki, xr * ki + xi * kr).astype(y_ref.dtype)


def _hy_forward(p, z, zcol, kf, order):
    b, l, _ = z.shape
    c = HY_WIDTH
    tk = _t(l, 1024)
    return pl.pallas_call(
        functools.partial(_hyfwd_kernel, tk=tk), grid=(l // tk, b),
        in_specs=[pl.BlockSpec((2, tk, l), lambda i, bi: (0, i, 0)),
                  pl.BlockSpec((1, l, c), lambda i, bi: (bi, 0, zcol)),
                  pl.BlockSpec((1, 2, tk, c), lambda i, bi: (order, 0, i, 0))],
        out_specs=pl.BlockSpec((1, 2, tk, c), lambda i, bi: (bi, 0, i, 0)),
        out_shape=jax.ShapeDtypeStruct((b, 2, l, c), BF16),
        compiler_params=_cp(("arbitrary", "arbitrary")), name="hy_forward",
    )(p, z, kf)


def _hyinv_kernel(pt_ref, y_ref, g_ref, z_ref, bias_ref, o_ref):
    conv = _dot(pt_ref[0], y_ref[0, 0]) + _dot(pt_ref[1], y_ref[0, 1])
    o_ref[0] = (g_ref[0] * (conv + bias_ref[...] * z_ref[0])).astype(o_ref.dtype)


def _hy_inverse(pt, y, u, gcol, z, zcol, bias, out_dtype):
    b, _, l, c = y.shape
    tn = _t(l, 1024)
    return pl.pallas_call(
        _hyinv_kernel, grid=(l // tn, b),
        in_specs=[pl.BlockSpec((2, tn, l), lambda i, bi: (0, i, 0)),
                  pl.BlockSpec((1, 2, l, c), lambda i, bi: (bi, 0, 0, 0)),
                  pl.BlockSpec((1, tn, c), lambda i, bi: (bi, i, gcol)),
                  pl.BlockSpec((1, tn, c), lambda i, bi: (bi, i, zcol)),
                  pl.BlockSpec((1, c), lambda i, bi: (0, 0))],
        out_specs=pl.BlockSpec((1, tn, c), lambda i, bi: (bi, i, 0)),
        out_shape=jax.ShapeDtypeStruct((b, l, c), out_dtype),
        compiler_params=_cp(("arbitrary", "arbitrary")), name="hy_inverse",
    )(pt, y, u, z, bias.reshape(1, c))


def _hyena(phy, lw, dft):
    p, pt, kf = dft
    u = _dwconv(phy, lw['hy_conv_w'], lw['hy_conv_b'], HY_ORDER + 1, chunk=1024)
    y = _hy_forward(p, u, 0, kf, 0)
    z = _hy_inverse(pt, y, u, 1, u, 0, lw['hy_bias'][0], F32)
    y = _hy_forward(p, z, 0, kf, 1)
    return _hy_inverse(pt, y, u, 2, z, 0, lw['hy_bias'][1], BF16)


def _merge_kernel(hx_ref, g0, g1, g2, g3, b0, b1, b2, b3, wbr_ref, o_ref, gb_ref, wb_ref):
    @pl.when(pl.program_id(1) == 0)
    def _():
        for n, wg in enumerate((g0, g1, g2, g3)):
            gb_ref[n] = wg[0].astype(gb_ref.dtype)
        wb_ref[...] = wbr_ref[0].astype(wb_ref.dtype)

    hx = hx_ref[...]
    acc = None
    for n, br in enumerate((b0, b1, b2, b3)):
        gate = jax.nn.sigmoid(_dot_nt(hx, gb_ref[n]))
        term = gate * _dot(br[...], wb_ref[n])
        acc = term if acc is None else acc + term
    o_ref[...] = acc.astype(o_ref.dtype)


def _merge(hx, w_gate, layer, row0, branches, w_br):
    m, d = hx.shape
    tm, tn = _t(m, 512), _t(d, 512)
    nj = d // tn
    bw = BRANCH_WIDTH
    in_specs = [pl.BlockSpec((tm, d), lambda j, i: (i, 0))]
    in_specs += [pl.BlockSpec((pl.Element(1), pl.Element(tn), pl.Element(d)),
                              functools.partial(lambda j, i, n: (layer, pl.multiple_of(row0 + n * d + j * tn, SUBLANES), 0), n=n),
                              pipeline_mode=pl.Buffered(1))
                 for n in range(N_BRANCH)]
    in_specs += [pl.BlockSpec((tm, bw), lambda j, i: (i, 0))] * N_BRANCH
    in_specs += [pl.BlockSpec((1, N_BRANCH, bw, tn), lambda j, i: (layer, 0, 0, j))]
    return pl.pallas_call(
        _merge_kernel, grid=(nj, m // tm), in_specs=in_specs,
        out_specs=pl.BlockSpec((tm, tn), lambda j, i: (i, j)),
        out_shape=jax.ShapeDtypeStruct((m, d), BF16),
        scratch_shapes=[pltpu.VMEM((N_BRANCH, tn, d), BF16), pltpu.VMEM((N_BRANCH, bw, tn), BF16)],
        compiler_params=_cp(("arbitrary", "arbitrary")), name="merge",
    )(hx, w_gate, w_gate, w_gate, w_gate, *branches, w_br)


def _topk_kernel(aff_ref, tri_ref, sp_ref, spt_ref, g_ref, *, cap):
    afft = aff_ref[0].T[:N_EXPERTS, :]
    bits = lax.bitcast_convert_type(afft, jnp.int32)

    def body(i, prefix):
        cand = prefix | jnp.left_shift(jnp.int32(1), 30 - i)
        cnt = jnp.sum((bits >= cand).astype(F32), axis=1, keepdims=True)
        return jnp.where(cnt >= cap, cand, prefix)

    thr = lax.fori_loop(0, 31, body, jnp.zeros((N_EXPERTS, 1), jnp.int32))
    gt = bits > thr
    eq = bits == thr
    need = cap - jnp.sum(gt.astype(F32), axis=1, keepdims=True)
    tri = tri_ref[...]
    rank_eq = _dot(eq.astype(BF16), tri)
    sel = gt | (eq & (rank_eq <= need))
    pos = _dot(sel.astype(BF16), tri) - 1.0
    sp = jnp.where(sel, pos, -1.0)
    sp_ref[0] = sp.astype(jnp.int32)
    n = sp.shape[1]
    padded = jnp.concatenate([sp, jnp.full((LANES - N_EXPERTS, n), -1.0, F32)], axis=0)
    spt_ref[0] = padded.T
    icap = g_ref.shape[2]
    slot = lax.broadcasted_iota(jnp.int32, (icap, n), 0).astype(F32)
    for e in range(N_EXPERTS):
        hit = slot == sp[e:e + 1, :]
        g_ref[0, e] = jnp.sum(jnp.where(hit, afft[e:e + 1, :], 0.0), axis=1, keepdims=True)


def _topk(aff, cap):
    b, n, _ = aff.shape
    t = jnp.arange(n, dtype=jnp.int32)
    tri = (t[:, None] <= t[None, :]).astype(BF16)
    return pl.pallas_call(
        functools.partial(_topk_kernel, cap=float(cap)), grid=(b,),
        in_specs=[pl.BlockSpec((1, n, LANES), lambda i: (i, 0, 0)), pl.BlockSpec((n, n), lambda i: (0, 0))],
        out_specs=[pl.BlockSpec((1, N_EXPERTS, n), lambda i: (i, 0, 0)), pl.BlockSpec((1, n, LANES), lambda i: (i, 0, 0)),
                   pl.BlockSpec((1, N_EXPERTS, cap, 1), lambda i: (i, 0, 0, 0))],
        out_shape=[jax.ShapeDtypeStruct((b, N_EXPERTS, n), jnp.int32), jax.ShapeDtypeStruct((b, n, LANES), F32),
                   jax.ShapeDtypeStruct((b, N_EXPERTS, cap, 1), F32)],
        compiler_params=_cp(("arbitrary",)), name="topk",
    )(aff, tri)


def _gather_kernel(sp_ref, h_ref, o_ref, *, cap):
    sp = sp_ref[0, 0]
    slot = lax.broadcasted_iota(jnp.int32, (cap, sp.shape[1]), 0)
    onehot = (slot == sp).astype(BF16)
    o_ref[0] = _dot(onehot, h_ref[0]).astype(o_ref.dtype)


def _gather(sp, h, cap):
    b, n, d = h.shape
    return pl.pallas_call(
        functools.partial(_gather_kernel, cap=cap), grid=(b, N_EXPERTS),
        in_specs=[pl.BlockSpec((1, 1, 1, n), lambda bi, e: (bi, e, 0, 0)), pl.BlockSpec((1, n, d), lambda bi, e: (bi, 0, 0))],
        out_specs=pl.BlockSpec((1, cap, d), lambda bi, e: (e, bi, 0)),
        out_shape=jax.ShapeDtypeStruct((N_EXPERTS, b * cap, d), BF16),
        compiler_params=_cp(("arbitrary", "arbitrary")), name="moe_gather",
    )(sp.reshape(b, N_EXPERTS, 1, n), h)


def _ffn_kernel(*refs, n_x):
    xs = refs[:n_x]
    gs = refs[n_x:2 * n_x]
    w1_ref, w3_ref, w2_ref = refs[2 * n_x:2 * n_x + 3]
    ys = refs[2 * n_x + 3:3 * n_x + 3]
    accs = refs[3 * n_x + 3:]
    f = pl.program_id(1)
    last = pl.num_programs(1) - 1
    w1, w3, w2 = w1_ref[0, 0].astype(BF16), w3_ref[0, 0].astype(BF16), w2_ref[0, 0].astype(BF16)
    for x_ref, g_ref, y_ref, acc in zip(xs, gs, ys, accs):
        rows = x_ref.shape[1]
        grp = 512 if rows % 512 == 0 else rows
        parts = []
        for r in range(rows // grp):
            x = x_ref[0, r * grp:(r + 1) * grp, :]
            a = _dot(x, w1)
            hid = (a * jax.nn.sigmoid(a) * _dot(x, w3)).astype(BF16)
            parts.append(_dot(hid, w2))

        @pl.when(f == 0)
        def _():
            for r, part in enumerate(parts):
                acc[r * grp:(r + 1) * grp, :] = part

        @pl.when(f > 0)
        def _():
            for r, part in enumerate(parts):
                acc[r * grp:(r + 1) * grp, :] += part

        @pl.when(f == last)
        def _():
            nb, _, cap, _ = y_ref.shape
            for bi in range(nb):
                y_ref[bi, 0] = (acc[bi * cap:(bi + 1) * cap, :] * g_ref[bi, 0]).astype(y_ref.dtype)


def _expert_ffn(xs, gs, w1, w3, w2, layer):
    _, e, d, ff = w1.shape
    tf = _t(ff, 256)
    in_specs = [pl.BlockSpec((1, x.shape[1], d), lambda ei, f: (ei, 0, 0)) for x in xs]
    in_specs += [pl.BlockSpec((g.shape[0], 1, g.shape[2], 1), lambda ei, f: (0, ei, 0, 0)) for g in gs]
    in_specs += [pl.BlockSpec((1, 1, d, tf), lambda ei, f: (layer, ei, 0, f))] * 2
    in_specs += [pl.BlockSpec((1, 1, tf, d), lambda ei, f: (layer, ei, f, 0))]
    return pl.pallas_call(
        functools.partial(_ffn_kernel, n_x=len(xs)), grid=(e, ff // tf), in_specs=in_specs,
        out_specs=[pl.BlockSpec((g.shape[0], 1, g.shape[2], d), lambda ei, f: (0, ei, 0, 0)) for g in gs],
        out_shape=[jax.ShapeDtypeStruct((g.shape[0], e, g.shape[2], d), BF16) for g in gs],
        scratch_shapes=[pltpu.VMEM((x.shape[1], d), F32) for x in xs],
        compiler_params=_cp(("arbitrary", "arbitrary")), name="expert_ffn",
    )(*xs, *gs, w1, w3, w2)


def _scatter_kernel(spt_ref, y_ref, x_ref, g_ref, o_ref, acc_ref, *, cap, kc):
    k = pl.program_id(2)
    tt = spt_ref.shape[1]
    blk = y_ref.shape[1]
    spt = spt_ref[0].astype(BF16)
    term = None
    for j in range(blk // kc):
        first = (k * blk + j * kc) // cap
        col = lax.broadcasted_iota(jnp.int32, (LANES, kc), 1)
        row = lax.broadcasted_iota(jnp.int32, (LANES, kc), 0)
        rep = (row == first + col // cap).astype(BF16)
        mine = _dot(spt, rep)
        slot = (lax.broadcasted_iota(jnp.int32, (tt, kc), 1) % cap).astype(F32)
        part = _dot((mine == slot).astype(BF16), y_ref[0, j * kc:(j + 1) * kc, :])
        term = part if term is None else term + part

    @pl.when(k == 0)
    def _():
        acc_ref[...] = term

    @pl.when(k > 0)
    def _():
        acc_ref[...] += term

    @pl.when(k == pl.num_programs(2) - 1)
    def _():
        o_ref[0] = x_ref[0] + g_ref[0] * acc_ref[...]


def _scatter(spt, y, x, g, cap):
    b, n, d = x.shape
    tt = _t(n, 512)
    kc = _t(N_EXPERTS * cap, 1024)
    blk = _t(N_EXPERTS * cap, 2 * kc)
    assert kc % cap == 0 and cap <= 256
    return pl.pallas_call(
        functools.partial(_scatter_kernel, cap=cap, kc=kc), grid=(b, n // tt, N_EXPERTS * cap // blk),
        in_specs=[pl.BlockSpec((1, tt, LANES), lambda bi, i, k: (bi, i, 0)),
                  pl.BlockSpec((1, blk, d), lambda bi, i, k: (bi, k, 0)),
                  pl.BlockSpec((1, tt, d), lambda bi, i, k: (bi, i, 0)),
                  pl.BlockSpec((1, 1, d), lambda bi, i, k: (bi, 0, 0))],
        out_specs=pl.BlockSpec((1, tt, d), lambda bi, i, k: (bi, i, 0)),
        out_shape=jax.ShapeDtypeStruct((b, n, d), F32),
        scratch_shapes=[pltpu.VMEM((tt, d), F32)],
        compiler_params=_cp(("arbitrary", "arbitrary", "arbitrary")), name="moe_scatter",
    )(spt, y, x, g)


def _rope_tables(s):
    rows = jnp.repeat(jnp.arange(s // GRID_W, dtype=jnp.int32), GRID_W).astype(F32)
    cols = jnp.tile(jnp.arange(GRID_W, dtype=jnp.int32), s // GRID_W).astype(F32)

    def half_tables(m):
        inv = ROPE_BASE ** (-jnp.arange(0, m, 2, dtype=F32) / m)
        out = []
        for pos in (rows, cols):
            ang = pos[:, None] * inv[None, :]
            c, sn = jnp.cos(ang), jnp.sin(ang)
            out.append((jnp.concatenate([c, c], axis=1), jnp.concatenate([-sn, sn], axis=1)))
        return (jnp.concatenate([out[0][0], out[1][0]], axis=1), jnp.concatenate([out[0][1], out[1][1]], axis=1))

    cg, sg = half_tables(GQA_HEAD_DIM // 2)
    cm, sm = half_tables(MLA_ROPE // 2)
    padm = lambda a, fill: jnp.concatenate(
        [jnp.full((s, MLA_NOPE), fill, F32), a, jnp.full((s, LANES - MLA_NOPE - MLA_ROPE), fill, F32)], axis=1)
    return jnp.stack([cg, sg]), jnp.stack([padm(cm, 1.0), padm(sm, 0.0)])


W_IN_Q = 2 * GQA_KV_HEADS * GQA_HEAD_DIM + MLA_KV_LORA + MLA_ROPE
W_IN_HY = W_IN_Q + GQA_HEADS * GQA_HEAD_DIM + MLA_Q_LORA
W_IN_CV = W_IN_HY + (HY_ORDER + 1) * HY_WIDTH
W_IN_GATE = W_IN_CV + 2 * CV_WIDTH


def _layer_weights(i, w):
    assert W_IN_Q + LANES - MLA_ROPE == ATT_KV_COLS and W_IN_HY - W_IN_Q == ATT_Q_COLS
    qb = w['mla_q_b'][i].reshape(MLA_Q_LORA, MLA_HEADS, MLA_NOPE + MLA_ROPE)
    qb = jnp.pad(qb, ((0, 0), (0, 0), (0, LANES - MLA_NOPE - MLA_ROPE))).reshape(MLA_Q_LORA, MLA_HEADS * LANES)
    kvb = w['mla_kv_b'][i].reshape(MLA_KV_LORA, MLA_HEADS, MLA_NOPE + MLA_V)
    kvb_k = jnp.pad(kvb[:, :, :MLA_NOPE], ((0, 0), (0, 0), (0, LANES - MLA_NOPE))).reshape(MLA_KV_LORA, MLA_HEADS * LANES)
    kvb_v = kvb[:, :, MLA_NOPE:].reshape(MLA_KV_LORA, MLA_HEADS * MLA_V)
    lw = {k: w[k][i] for k in ('gqa_q_gain', 'gqa_k_gain', 'mla_q_a_gain', 'mla_kv_a_gain', 'hy_conv_w', 'hy_conv_b',
                               'hf_w1', 'hf_b1', 'hf_w2', 'hf_b2', 'hf_w3', 'hf_freq', 'hf_log_rate', 'hy_bias',
                               'cv_w', 'cv_b', 'cv_ln_g', 'cv_ln_b', 'w_router', 'g_mix', 'g_ffn')}
    lw.update(layer=i, w_in_t=w['w_in_t'], q_b=qb, kv_b_k=kvb_k, kv_b_v=kvb_v, w_br=w['w_br'], w_out=w['w_out'])
    return lw


def _mixer_branches(hx, keys_extra, tabs, lw, dft):
    b, l, d = hx.shape
    hx2 = hx.reshape(b * l, d)
    proj = lambda row0, n, tn: _matmul_nt(hx2, lw['w_in_t'], lw['layer'], row0, n, F32, 1024, tn).reshape(b, l, n)
    kg, vg, km, vm, qg, qm = _attn_prep(hx, tabs, lw)
    own = (kg, vg, km, vm)
    if dft is None:
        return None, own
    ek = keys_extra
    o_gqa = _attention(qg, [(kg, vg)] + ([(ek[0], ek[1])] if ek else []), GQA_HEADS, GQA_KV_HEADS, GQA_HEAD_DIM)
    o_mla = _attention(qm, [(km, vm)] + ([(ek[2], ek[3])] if ek else []), MLA_HEADS, MLA_HEADS, MLA_NOPE + MLA_ROPE)
    phy = proj(W_IN_HY, W_IN_CV - W_IN_HY, 768)
    o_hy = _hyena(phy, lw, dft)
    pcv = proj(W_IN_CV, W_IN_GATE - W_IN_CV, 1024)
    o_cv = _dwconv(pcv, lw['cv_w'], lw['cv_b'], 1, glu=True, ln=(lw['cv_ln_g'], lw['cv_ln_b']), out_dtype=BF16)
    flat = lambda a: a.reshape(b * l, -1)
    return [flat(o_hy), flat(o_gqa), flat(o_mla), flat(o_cv)], own


def kernel(x, c, ctx, c_ctx, w_mod, b_mod, g_mix, w_in, gqa_q_gain, gqa_k_gain, mla_q_a_gain, mla_q_b, mla_kv_a_gain, mla_kv_b, hy_conv_w, hy_conv_b, hf_w1, hf_b1, hf_w2, hf_b2, hf_w3, hf_freq, hf_log_rate, hy_bias, cv_w, cv_b, cv_ln_g, cv_ln_b, w_br, w_out, g_ffn, w_router, w1, w3, w2, g_final):
    w = dict(g_mix=g_mix, w_in_t=jnp.swapaxes(w_in, 1, 2), gqa_q_gain=gqa_q_gain, gqa_k_gain=gqa_k_gain, mla_q_a_gain=mla_q_a_gain,
             mla_q_b=mla_q_b, mla_kv_a_gain=mla_kv_a_gain, mla_kv_b=mla_kv_b, hy_conv_w=hy_conv_w, hy_conv_b=hy_conv_b,
             hf_w1=hf_w1, hf_b1=hf_b1, hf_w2=hf_w2, hf_b2=hf_b2, hf_w3=hf_w3, hf_freq=hf_freq, hf_log_rate=hf_log_rate,
             hy_bias=hy_bias, cv_w=cv_w, cv_b=cv_b, cv_ln_g=cv_ln_g, cv_ln_b=cv_ln_b, w_br=w_br, w_out=w_out,
             g_ffn=g_ffn, w_router=w_router)
    bsz, s, d = x.shape
    lc = ctx.shape[1]
    depth = w_mod.shape[0]
    assert bsz < 8
    cvec = jnp.concatenate([c, c_ctx[None, :], jnp.zeros((8 - bsz - 1, d), F32)], axis=0)
    mod = _mod_all(cvec, w_mod, b_mod)
    tabs = _rope_tables(s)
    p_x, pt_x = _dft_tables(s)
    p_c, pt_c = _dft_tables(lc)
    cap_x = CAPACITY_FACTOR * s // N_EXPERTS
    cap_c = CAPACITY_FACTOR * lc // N_EXPERTS
    xc = ctx
    for i in range(depth):
        last = i == depth - 1
        lw = _layer_weights(i, w)
        mx = [mod[i, :bsz, j * d:(j + 1) * d].reshape(bsz, 1, d) for j in range(6)]
        mc = [jnp.broadcast_to(mod[i, bsz:bsz + 1, j * d:(j + 1) * d].reshape(1, 1, d), (bsz, 1, d)) for j in range(6)]

        hc = _norm_mod(xc, lw['g_mix'], mc[0], mc[1])
        dft_c = None if last else (p_c, pt_c, _hy_spectrum(p_c, _hy_filters(lc, lw)))
        br_c, keys_c = _mixer_branches(hc, None, None, lw, dft_c)
        hx = _norm_mod(x, lw['g_mix'], mx[0], mx[1])
        dft_x = (p_x, pt_x, _hy_spectrum(p_x, _hy_filters(s, lw)))
        br_x, _ = _mixer_branches(hx, keys_c, tabs, lw, dft_x)

        merged = _merge(hx.reshape(bsz * s, d), lw['w_in_t'], i, W_IN_GATE, br_x, lw['w_br']).reshape(bsz, s, d)
        x = _matmul_residual(merged, lw['w_out'], i, x, mx[2])
        hf, aff = _norm_mod(x, lw['g_ffn'], mx[3], mx[4], lw['w_router'])
        sp, spt, gslot = _topk(aff, cap_x)
        xs, gs = [_gather(sp, hf, cap_x)], [gslot]
        if not last:
            merged_c = _merge(hc.reshape(bsz * lc, d), lw['w_in_t'], i, W_IN_GATE, br_c, lw['w_br']).reshape(bsz, lc, d)
            xc = _matmul_residual(merged_c, lw['w_out'], i, xc, mc[2])
            hfc, aff_c = _norm_mod(xc, lw['g_ffn'], mc[3], mc[4], lw['w_router'])
            sp_c, spt_c, gslot_c = _topk(aff_c, cap_c)
            xs.append(_gather(sp_c, hfc, cap_c))
            gs.append(gslot_c)
        ys = _expert_ffn(xs, gs, w1, w3, w2, i)
        x = _scatter(spt, ys[0].reshape(bsz, N_EXPERTS * cap_x, d), x, mx[5], cap_x)
        if not last:
            xc = _scatter(spt_c, ys[1].reshape(bsz, N_EXPERTS * cap_c, d), xc, mc[5], cap_c)
    return _final_norm(x, g_final)
```

```python
import functools
import math

import numpy as np
import jax
import jax.numpy as jnp
from jax import lax
from jax.experimental import pallas as pl
from jax.experimental.pallas import tpu as pltpu

F32 = jnp.float32
BF16 = jnp.bfloat16
HI = lax.Precision.HIGHEST

GRID_W = 64
EPS = 1e-6
ROPE_BASE = 10000.0
GQA_HEADS, GQA_KV_HEADS, GQA_HEAD_DIM = 4, 2, 128
MLA_HEADS, MLA_Q_LORA, MLA_KV_LORA, MLA_NOPE, MLA_ROPE, MLA_V = 4, 384, 256, 64, 32, 128
HY_WIDTH, HY_ORDER, HY_BANDS, HY_FILTER_HIDDEN = 512, 2, 16, 64
CV_WIDTH, CV_KERNEL = 512, 31
N_BRANCH, BRANCH_WIDTH = 4, 512
N_EXPERTS, EXPERT_FF, CAPACITY_FACTOR = 16, 1024, 2
LANES = 128
SUBLANES = 8
BF16_ROWS = 16
CONV_PAD = 16
VMEM_LIMIT = 56 * 1024 * 1024


def _cp(sem, vmem=VMEM_LIMIT):
    return pltpu.CompilerParams(dimension_semantics=sem, vmem_limit_bytes=vmem)


def _t(n, pref):
    return pref if n % pref == 0 else n


def _dot(a, b):
    return jnp.dot(a, b, preferred_element_type=F32)


def _mod_kernel(c_ref, w_ref, b_ref, o_ref):
    c = c_ref[...]
    a = (c * jax.nn.sigmoid(c)).astype(BF16)
    o_ref[0] = _dot(a, w_ref[0].astype(BF16)) + b_ref[0]


def _mod_all(cvec, w_mod, b_mod):
    depth, d, n6 = w_mod.shape
    tn = _t(n6, 1024)
    return pl.pallas_call(
        _mod_kernel, grid=(depth, n6 // tn),
        in_specs=[pl.BlockSpec((8, d), lambda l, j: (0, 0)),
                  pl.BlockSpec((1, d, tn), lambda l, j: (l, 0, j)),
                  pl.BlockSpec((1, 1, tn), lambda l, j: (l, 0, j))],
        out_specs=pl.BlockSpec((1, 8, tn), lambda l, j: (l, 0, j)),
        out_shape=jax.ShapeDtypeStruct((depth, 8, n6), F32),
        compiler_params=_cp(("arbitrary", "arbitrary")), name="mod",
    )(cvec, w_mod, b_mod.reshape(depth, 1, n6))


def _norm_h(x_ref, g_ref, sh_ref, sc_ref):
    x = x_ref[0]
    y = x * lax.rsqrt(jnp.mean(x * x, axis=-1, keepdims=True) + EPS)
    return (y * g_ref[...]) * (1.0 + sc_ref[0]) + sh_ref[0]


def _norm_mod_kernel(x_ref, g_ref, sh_ref, sc_ref, o_ref):
    o_ref[0] = _norm_h(x_ref, g_ref, sh_ref, sc_ref).astype(o_ref.dtype)


def _norm_router_kernel(x_ref, g_ref, sh_ref, sc_ref, wr_ref, o_ref, aff_ref):
    h = _norm_h(x_ref, g_ref, sh_ref, sc_ref)
    h_hi = h.astype(BF16)
    o_ref[0] = h_hi.astype(o_ref.dtype)
    w = wr_ref[...]
    w_hi = w.astype(BF16)
    h_lo = (h - h_hi.astype(F32)).astype(BF16)
    w_lo = (w - w_hi.astype(F32)).astype(BF16)
    logits = _dot(h_hi, w_hi) + (_dot(h_lo, w_hi) + _dot(h_hi, w_lo))
    lane = lax.broadcasted_iota(jnp.int32, logits.shape, 1)
    logits = jnp.where(lane < N_EXPERTS, logits, -1e30)
    e = jnp.exp(logits - jnp.max(logits, axis=-1, keepdims=True))
    aff_ref[0] = e / jnp.sum(e, axis=-1, keepdims=True)


def _norm_mod(x, g, sh, sc, w_router=None):
    b, l, d = x.shape
    tm = _t(l, 256)
    specs = [pl.BlockSpec((1, tm, d), lambda i, j: (i, j, 0)),
             pl.BlockSpec((1, d), lambda i, j: (0, 0)),
             pl.BlockSpec((1, 1, d), lambda i, j: (i, 0, 0)),
             pl.BlockSpec((1, 1, d), lambda i, j: (i, 0, 0))]
    h_spec = pl.BlockSpec((1, tm, d), lambda i, j: (i, j, 0))
    h_shape = jax.ShapeDtypeStruct((b, l, d), BF16)
    if w_router is None:
        return pl.pallas_call(_norm_mod_kernel, grid=(b, l // tm), in_specs=specs, out_specs=h_spec,
                              out_shape=h_shape, compiler_params=_cp(("arbitrary", "arbitrary")),
                              name="norm_mod")(x, g.reshape(1, d), sh, sc)
    wr = jnp.pad(w_router, ((0, 0), (0, LANES - w_router.shape[1])))
    return pl.pallas_call(
        _norm_router_kernel, grid=(b, l // tm),
        in_specs=specs + [pl.BlockSpec((d, LANES), lambda i, j: (0, 0))],
        out_specs=[h_spec, pl.BlockSpec((1, tm, LANES), lambda i, j: (i, j, 0))],
        out_shape=[h_shape, jax.ShapeDtypeStruct((b, l, LANES), F32)],
        compiler_params=_cp(("arbitrary", "arbitrary")), name="norm_router",
    )(x, g.reshape(1, d), sh, sc, wr)


def _dot_nt(a, w):
    return lax.dot_general(a, w, (((1,), (1,)), ((), ())), preferred_element_type=F32)


def _mm_kernel(a_ref, w_ref, o_ref, wb_ref):
    @pl.when(pl.program_id(1) == 0)
    def _():
        wb_ref[...] = w_ref[0].astype(wb_ref.dtype)

    o_ref[...] = _dot_nt(a_ref[...], wb_ref[...]).astype(o_ref.dtype)


def _matmul_nt(a, w, layer, row0, n, out_dtype, tm, tn):
    m, k = a.shape
    tm, tn = _t(m, tm), _t(n, tn)
    return pl.pallas_call(
        _mm_kernel, grid=(n // tn, m // tm),
        in_specs=[pl.BlockSpec((tm, k), lambda j, i: (i, 0)),
                  pl.BlockSpec((pl.Element(1), pl.Element(tn), pl.Element(k)),
                               lambda j, i: (layer, pl.multiple_of(row0 + j * tn, SUBLANES), 0))],
        out_specs=pl.BlockSpec((tm, tn), lambda j, i: (i, j)),
        out_shape=jax.ShapeDtypeStruct((m, n), out_dtype),
        scratch_shapes=[pltpu.VMEM((tn, k), BF16)],
        compiler_params=_cp(("arbitrary", "arbitrary")), name="matmul",
    )(a, w)


def _mm_res_kernel(a_ref, w_ref, x_ref, g_ref, o_ref, wb_ref):
    @pl.when((pl.program_id(1) == 0) & (pl.program_id(2) == 0))
    def _():
        wb_ref[...] = w_ref[0].astype(wb_ref.dtype)

    o_ref[0] = x_ref[0] + g_ref[0] * _dot(a_ref[0], wb_ref[...])


def _matmul_residual(a, w, layer, x, g):
    b, l, k = a.shape
    n = w.shape[2]
    tm, tn = _t(l, 1024), _t(n, 512)
    return pl.pallas_call(
        _mm_res_kernel, grid=(n // tn, b, l // tm),
        in_specs=[pl.BlockSpec((1, tm, k), lambda j, bi, i: (bi, i, 0)),
                  pl.BlockSpec((1, k, tn), lambda j, bi, i: (layer, 0, j)),
                  pl.BlockSpec((1, tm, tn), lambda j, bi, i: (bi, i, j)),
                  pl.BlockSpec((1, 1, tn), lambda j, bi, i: (bi, 0, j))],
        out_specs=pl.BlockSpec((1, tm, tn), lambda j, bi, i: (bi, i, j)),
        out_shape=jax.ShapeDtypeStruct((b, l, n), F32),
        scratch_shapes=[pltpu.VMEM((k, tn), BF16)],
        compiler_params=_cp(("arbitrary", "arbitrary", "arbitrary")), name="matmul_residual",
    )(a, w, x, g)


ATT_K, ATT_V, ATT_KVA, ATT_KPE, ATT_KV_COLS = 0, 256, 512, 768, 896
ATT_Q, ATT_QA, ATT_Q_COLS = 0, 512, 896


def _rope(xh, tab_ref, half):
    lane = lax.broadcasted_iota(jnp.int32, xh.shape, 1)
    partner = jnp.where((lane % (2 * half)) < half, pltpu.roll(xh, LANES - half, 1), pltpu.roll(xh, half, 1))
    return xh * tab_ref[0] + partner * tab_ref[1]


def _rms(x, gain):
    return x * lax.rsqrt(jnp.mean(x * x, axis=-1, keepdims=True) + EPS) * gain


def _prep_kernel(*refs, use_pos):
    hx_ref, wkv_ref, wq_ref = refs[:3]
    refs = refs[3:]
    if use_pos:
        tg_ref, tm_ref = refs[:2]
        refs = refs[2:]
    (gq_ref, gk_ref, gqa_ref, gkva_ref, qb_ref, kvbk_ref, kvbv_ref, kg_ref, vg_ref, km_ref, vm_ref, qg_ref, qm_ref,
     wkvb_ref, wqb_ref) = refs

    @pl.when((pl.program_id(0) == 0) & (pl.program_id(1) == 0))
    def _():
        wkvb_ref[...] = wkv_ref[0].astype(wkvb_ref.dtype)
        wqb_ref[...] = wq_ref[0].astype(wqb_ref.dtype)

    hx = hx_ref[0]
    p = _dot_nt(hx, wkvb_ref[...])
    pq = _dot_nt(hx, wqb_ref[...])
    hd = GQA_HEAD_DIM
    for h in range(GQA_KV_HEADS):
        kh = _rms(p[:, ATT_K + h * hd:ATT_K + (h + 1) * hd], gk_ref[...])
        if use_pos:
            kh = _rope(kh, tg_ref, 32)
        kg_ref[0, :, h * hd:(h + 1) * hd] = kh.astype(BF16)
    vg_ref[0] = p[:, ATT_V:ATT_V + GQA_KV_HEADS * hd].astype(BF16)
    for h in range(GQA_HEADS):
        qh = _rms(pq[:, ATT_Q + h * hd:ATT_Q + (h + 1) * hd], gq_ref[...])
        if use_pos:
            qh = _rope(qh, tg_ref, 32)
        qg_ref[0, :, h * hd:(h + 1) * hd] = qh.astype(BF16)
    kvn = _rms(p[:, ATT_KVA:ATT_KVA + MLA_KV_LORA], gkva_ref[...]).astype(BF16)
    knope = _dot(kvn, kvbk_ref[...].astype(BF16))
    vm_ref[0] = _dot(kvn, kvbv_ref[...].astype(BF16)).astype(BF16)
    kpe = p[:, ATT_KPE:ATT_KPE + LANES]
    lane = lax.broadcasted_iota(jnp.int32, kpe.shape, 1)
    kpe = pltpu.roll(jnp.where(lane < MLA_ROPE, kpe, 0.0), MLA_NOPE, 1)
    if use_pos:
        kpe = _rope(kpe, tm_ref, 8)
    qan = _rms(pq[:, ATT_QA:ATT_QA + MLA_Q_LORA], gqa_ref[...]).astype(BF16)
    qmf = _dot(qan, qb_ref[...].astype(BF16))
    for h in range(MLA_HEADS):
        km_ref[0, :, h * LANES:(h + 1) * LANES] = (knope[:, h * LANES:(h + 1) * LANES] + kpe).astype(BF16)
        qh = qmf[:, h * LANES:(h + 1) * LANES]
        if use_pos:
            qh = _rope(qh, tm_ref, 8)
        qm_ref[0, :, h * LANES:(h + 1) * LANES] = qh.astype(BF16)


def _attn_prep(hx, tabs, lw):
    b, l, d = hx.shape
    tm = _t(l, 512)
    use_pos = tabs is not None
    full = lambda shape: pl.BlockSpec(shape, lambda i, j: (0,) * len(shape))
    layer = lw['layer']
    window = lambda row0, n: pl.BlockSpec((pl.Element(1), pl.Element(n), pl.Element(d)),
                                          lambda i, j: (layer, row0, 0), pipeline_mode=pl.Buffered(1))
    in_specs = [pl.BlockSpec((1, tm, d), lambda i, j: (i, j, 0)), window(0, ATT_KV_COLS), window(W_IN_Q, ATT_Q_COLS)]
    args = [hx, lw['w_in_t'], lw['w_in_t']]
    if use_pos:
        in_specs += [pl.BlockSpec((2, tm, LANES), lambda i, j: (0, j, 0))] * 2
        args += list(tabs)
    small = [lw['gqa_q_gain'].reshape(1, -1), lw['gqa_k_gain'].reshape(1, -1), lw['mla_q_a_gain'].reshape(1, -1),
             lw['mla_kv_a_gain'].reshape(1, -1), lw['q_b'], lw['kv_b_k'], lw['kv_b_v']]
    in_specs += [full(a.shape) for a in small]
    widths = [256, 256, 512, 512, 512, 512]
    return pl.pallas_call(
        functools.partial(_prep_kernel, use_pos=use_pos), grid=(b, l // tm), in_specs=in_specs,
        out_specs=[pl.BlockSpec((1, tm, w), lambda i, j: (i, j, 0)) for w in widths],
        out_shape=[jax.ShapeDtypeStruct((b, l, w), BF16) for w in widths],
        scratch_shapes=[pltpu.VMEM((ATT_KV_COLS, d), BF16), pltpu.VMEM((ATT_Q_COLS, d), BF16)],
        compiler_params=_cp(("arbitrary", "arbitrary")), name="attn_prep",
    )(*args, *small)


def _attn_kernel(*refs, n_src, scale, nsplit):
    q_ref, o_ref = refs[0], refs[-1]
    rows = q_ref.shape[1] // nsplit
    c2 = scale * math.log2(math.e)
    for c in range(nsplit):
        q = q_ref[0, c * rows:(c + 1) * rows, :]
        ss = [lax.dot_general(q, refs[1 + 2 * j][0], (((1,), (1,)), ((), ())), preferred_element_type=F32)
              for j in range(n_src)]
        m = jnp.max(ss[0], axis=-1, keepdims=True)
        for s in ss[1:]:
            m = jnp.maximum(m, jnp.max(s, axis=-1, keepdims=True))
        acc, den = None, None
        for j, s in enumerate(ss):
            p = jnp.exp2((s - m) * c2)
            d = jnp.sum(p, axis=-1, keepdims=True)
            a = _dot(p.astype(BF16), refs[2 + 2 * j][0])
            acc = a if acc is None else acc + a
            den = d if den is None else den + d
        o_ref[0, c * rows:(c + 1) * rows, :] = (acc / den).astype(o_ref.dtype)


def _attention(q, srcs, heads, kv_heads, dk):
    b, s, _ = q.shape
    r = heads // kv_heads
    tq = _t(s, 1024)
    in_specs = [pl.BlockSpec((1, tq, LANES), lambda bi, h, i: (bi, i, h))]
    args = [q]
    for k, v in srcs:
        lk = k.shape[1]
        in_specs += [pl.BlockSpec((1, lk, LANES), lambda bi, h, i: (bi, 0, h // r))] * 2
        args += [k, v]
    return pl.pallas_call(
        functools.partial(_attn_kernel, n_src=len(srcs), scale=float(dk) ** -0.5, nsplit=tq // 256 if tq % 256 == 0 else 1),
        grid=(b, heads, s // tq), in_specs=in_specs,
        out_specs=pl.BlockSpec((1, tq, LANES), lambda bi, h, i: (bi, i, h)),
        out_shape=jax.ShapeDtypeStruct((b, s, heads * LANES), BF16),
        compiler_params=_cp(("arbitrary", "arbitrary", "arbitrary")), name="attention",
    )(*args)


def _dwconv_kernel(*refs, taps, glu, post_ln, chunk, sub):
    if glu:
        ap, ac, an, bp, bc, bn, w_ref, b_ref, lg_ref, lb_ref, o_ref, scr = refs
        load = lambda a, g: a[0] * jax.nn.sigmoid(g[0])
        prev, cur, nxt = load(ap, bp), load(ac, bc), load(an, bn)
    else:
        ap, ac, an, w_ref, b_ref, o_ref, scr = refs
        prev, cur, nxt = ap[0], ac[0], an[0]
    i = pl.program_id(2)
    last = pl.num_programs(2) - 1
    pad = CONV_PAD
    scr[0, 0:pad, :] = jnp.where(i > 0, prev, 0.0)
    scr[0, pad:pad + chunk, :] = cur
    scr[0, pad + chunk:2 * pad + chunk, :] = jnp.where(i < last, nxt, 0.0)
    lo = (taps - 1) // 2
    span = chunk + 2 * pad - SUBLANES
    for r in sorted({(pad - lo + j) % SUBLANES for j in range(taps)} - {0}):
        scr[r, 0:span, :] = scr[0, pl.ds(r, span), :]
    for c in range(chunk // sub):
        acc = None
        for j in range(taps):
            o = c * sub + pad - lo + j
            term = w_ref[j:j + 1, :] * scr[o % SUBLANES, pl.ds(o - o % SUBLANES, sub), :]
            acc = term if acc is None else acc + term
        y = acc + b_ref[...]
        if post_ln:
            mu = jnp.mean(y, axis=-1, keepdims=True)
            yc = y - mu
            var = jnp.mean(yc * yc, axis=-1, keepdims=True)
            y = yc * lax.rsqrt(var + EPS) * lg_ref[...] + lb_ref[...]
            y = y * jax.nn.sigmoid(y)
        o_ref[0, c * sub:(c + 1) * sub, :] = y.astype(o_ref.dtype)


def _dwconv(x, w, bias, ncol, *, glu=False, ln=None, out_dtype=F32, chunk=256):
    b, l, _ = x.shape
    taps = w.shape[0]
    cw = 512
    chunk = _t(l, chunk)
    nblk = l // chunk
    per = chunk // CONV_PAD
    nsmall = l // CONV_PAD

    def views(coff):
        return [pl.BlockSpec((1, CONV_PAD, cw), lambda bi, c, i: (bi, jnp.maximum(i * per - 1, 0), c + coff)),
                pl.BlockSpec((1, chunk, cw), lambda bi, c, i: (bi, i, c + coff)),
                pl.BlockSpec((1, CONV_PAD, cw), lambda bi, c, i: (bi, jnp.minimum((i + 1) * per, nsmall - 1), c + coff))]

    in_specs = views(0)
    args = [x, x, x]
    if glu:
        in_specs += views(ncol)
        args += [x, x, x]
    vec = lambda: pl.BlockSpec((1, cw), lambda bi, c, i: (0, c))
    in_specs += [pl.BlockSpec((taps, cw), lambda bi, c, i: (0, c)), vec()]
    args += [w, bias.reshape(1, -1)]
    if ln is not None:
        in_specs += [vec(), vec()]
        args += [ln[0].reshape(1, -1), ln[1].reshape(1, -1)]
    return pl.pallas_call(
        functools.partial(_dwconv_kernel, taps=taps, glu=glu, post_ln=ln is not None, chunk=chunk, sub=32),
        grid=(b, ncol, nblk), in_specs=in_specs,
        out_specs=pl.BlockSpec((1, chunk, cw), lambda bi, c, i: (bi, i, c)),
        out_shape=jax.ShapeDtypeStruct((b, l, ncol * cw), out_dtype),
        scratch_shapes=[pltpu.VMEM((SUBLANES, chunk + 2 * CONV_PAD, cw), F32)],
        compiler_params=_cp(("arbitrary", "arbitrary", "arbitrary")), name="dwconv",
    )(*args)


def _dft_tables(l):
    n2 = 2 * l
    k = jnp.arange(l, dtype=jnp.int32)
    m = (k[:, None] * k[None, :]) & (n2 - 1)
    ang = m.astype(F32) * (2.0 * math.pi / n2)
    c = jnp.cos(ang)
    s = -jnp.sin(ang)
    nyq = jnp.where(k % 2 == 0, 1.0, -1.0).astype(F32)
    s = jnp.where((k == 0)[:, None], nyq[None, :], s)
    p = jnp.stack([c, s]).astype(BF16)
    return p, jnp.swapaxes(p, 1, 2)


def _hy_feats(l):
    pos = jnp.arange(l, dtype=F32)
    t01 = pos / (l - 1)
    bands = jnp.linspace(1e-4, HY_BANDS - 1, HY_BANDS, dtype=F32)
    ang = (2.0 * math.pi / l) * pos[:, None] * bands[None, :]
    feats = jnp.concatenate([t01[:, None], jnp.cos(ang), -jnp.sin(ang)], axis=-1)
    return jnp.pad(feats, ((0, 0), (0, LANES - feats.shape[1])))


def _hyfilt_kernel(f_ref, w1_ref, b1_ref, fr_ref, w2_ref, b2_ref, w3_ref, lr_ref, o_ref, *, tl):
    feats = f_ref[...]
    fr = fr_ref[...]
    hdot = lambda a, b: jnp.dot(a, b, preferred_element_type=F32, precision=HI)
    h = jnp.sin(fr * (hdot(feats, w1_ref[...]) + b1_ref[...]))
    h = jnp.sin(fr * (hdot(h, w2_ref[...]) + b2_ref[...]))
    h = hdot(h, w3_ref[...])
    h = h * jnp.exp(-feats[:, 0:1] * jnp.exp(lr_ref[...]))
    row = pl.program_id(0) * tl + lax.broadcasted_iota(jnp.int32, h.shape, 0)
    col = lax.broadcasted_iota(jnp.int32, h.shape, 1)
    is_bwd = ((col // HY_WIDTH) % 2) == 1
    o_ref[...] = jnp.where((row == 0) & is_bwd, 0.0, h).astype(o_ref.dtype)


def _hy_filters(l, lw):
    feats = _hy_feats(l)
    hid = HY_FILTER_HIDDEN
    padc = lambda a: jnp.pad(a, ((0, 0), (0, LANES - a.shape[1])))
    w1 = jnp.pad(lw['hf_w1'], ((0, LANES - lw['hf_w1'].shape[0]), (0, LANES - hid)))
    w2 = jnp.pad(lw['hf_w2'], ((0, LANES - hid), (0, LANES - hid)))
    w3 = jnp.pad(lw['hf_w3'], ((0, LANES - hid), (0, 0)))
    b1, b2, fr = padc(lw['hf_b1'].reshape(1, -1)), padc(lw['hf_b2'].reshape(1, -1)), padc(lw['hf_freq'].reshape(1, -1))
    lr = lw['hf_log_rate'].reshape(1, -1)
    nc = w3.shape[1]
    tl = _t(l, 256)
    full = lambda a: pl.BlockSpec(a.shape, lambda i: (0, 0))
    return pl.pallas_call(
        functools.partial(_hyfilt_kernel, tl=tl), grid=(l // tl,),
        in_specs=[pl.BlockSpec((tl, LANES), lambda i: (i, 0)), full(w1), full(b1), full(fr), full(w2), full(b2),
                  full(w3), full(lr)],
        out_specs=pl.BlockSpec((tl, nc), lambda i: (i, 0)),
        out_shape=jax.ShapeDtypeStruct((l, nc), BF16),
        compiler_params=_cp(("arbitrary",)), name="hy_filter",
    )(feats, w1, b1, fr, w2, b2, w3, lr)


def _hyspec_kernel(p_ref, h_ref, o_ref, *, tk, scale):
    hm = h_ref[...]
    ar = _dot(p_ref[0], hm)
    ai = _dot(p_ref[1], hm)
    c = HY_WIDTH
    row0 = (pl.program_id(0) * tk + lax.broadcasted_iota(jnp.int32, (tk, c), 0)) == 0
    s = jnp.where(row0, 0.5 * scale, scale)
    for n in range(HY_ORDER):
        o = 2 * n * c
        o_ref[n, 0] = (ar[:, o:o + c] + ar[:, o + c:o + 2 * c]) * s
        fi, bi = ai[:, o:o + c], ai[:, o + c:o + 2 * c]
        o_ref[n, 1] = jnp.where(row0, fi + bi, fi - bi) * s


def _hy_spectrum(p, filt):
    l = p.shape[1]
    tk = _t(l, 256)
    return pl.pallas_call(
        functools.partial(_hyspec_kernel, tk=tk, scale=1.0 / l), grid=(l // tk,),
        in_specs=[pl.BlockSpec((2, tk, l), lambda i: (0, i, 0)), pl.BlockSpec(filt.shape, lambda i: (0, 0))],
        out_specs=pl.BlockSpec((HY_ORDER, 2, tk, HY_WIDTH), lambda i: (0, 0, i, 0)),
        out_shape=jax.ShapeDtypeStruct((HY_ORDER, 2, l, HY_WIDTH), F32),
        compiler_params=_cp(("arbitrary",)), name="hy_spectrum",
    )(p, filt)


def _hyfwd_kernel(p_ref, z_ref, k_ref, y_ref, *, tk):
    z = z_ref[0].astype(BF16)
    xr = _dot(p_ref[0], z)
    xi = _dot(p_ref[1], z)
    kr, ki = k_ref[0, 0], k_ref[0, 1]
    row0 = (pl.program_id(0) * tk + lax.broadcasted_iota(jnp.int32, xr.shape, 0)) == 0
    xiki = xi * ki
    y_ref[0, 0] = (xr * kr - jnp.where(row0, 0.0, xiki)).astype(y_ref.dtype)
    y_ref[0, 1] = jnp.where(row0, xiki, xr * ki + xi * kr).astype(y_ref.dtype)


def _hy_forward(p, z, zcol, kf, order):
    b, l, _ = z.shape
    c = HY_WIDTH
    tk = _t(l, 1024)
    return pl.pallas_call(
        functools.partial(_hyfwd_kernel, tk=tk), grid=(l // tk, b),
        in_specs=[pl.BlockSpec((2, tk, l), lambda i, bi: (0, i, 0)),
                  pl.BlockSpec((1, l, c), lambda i, bi: (bi, 0, zcol)),
                  pl.BlockSpec((1, 2, tk, c), lambda i, bi: (order, 0, i, 0))],
        out_specs=pl.BlockSpec((1, 2, tk, c), lambda i, bi: (bi, 0, i, 0)),
        out_shape=jax.ShapeDtypeStruct((b, 2, l, c), BF16),
        compiler_params=_cp(("arbitrary", "arbitrary")), name="hy_forward",
    )(p, z, kf)


def _hyinv_kernel(pt_ref, y_ref, g_ref, z_ref, bias_ref, o_ref):
    conv = _dot(pt_ref[0], y_ref[0, 0]) + _dot(pt_ref[1], y_ref[0, 1])
    o_ref[0] = (g_ref[0] * (conv + bias_ref[...] * z_ref[0])).astype(o_ref.dtype)


def _hy_inverse(pt, y, u, gcol, z, zcol, bias, out_dtype):
    b, _, l, c = y.shape
    tn = _t(l, 1024)
    return pl.pallas_call(
        _hyinv_kernel, grid=(l // tn, b),
        in_specs=[pl.BlockSpec((2, tn, l), lambda i, bi: (0, i, 0)),
                  pl.BlockSpec((1, 2, l, c), lambda i, bi: (bi, 0, 0, 0)),
                  pl.BlockSpec((1, tn, c), lambda i, bi: (bi, i, gcol)),
                  pl.BlockSpec((1, tn, c), lambda i, bi: (bi, i, zcol)),
                  pl.BlockSpec((1, c), lambda i, bi: (0, 0))],
        out_specs=pl.BlockSpec((1, tn, c), lambda i, bi: (bi, i, 0)),
        out_shape=jax.ShapeDtypeStruct((b, l, c), out_dtype),
        compiler_params=_cp(("arbitrary", "arbitrary")), name="hy_inverse",
    )(pt, y, u, z, bias.reshape(1, c))


def _hyena(phy, lw, dft):
    p, pt, kf = dft
    u = _dwconv(phy, lw['hy_conv_w'], lw['hy_conv_b'], HY_ORDER + 1, chunk=1024)
    y = _hy_forward(p, u, 0, kf, 0)
    z = _hy_inverse(pt, y, u, 1, u, 0, lw['hy_bias'][0], F32)
    y = _hy_forward(p, z, 0, kf, 1)
    return _hy_inverse(pt, y, u, 2, z, 0, lw['hy_bias'][1], BF16)


def _merge_kernel(hx_ref, g0, g1, g2, g3, b0, b1, b2, b3, wbr_ref, o_ref, gb_ref, wb_ref):
    @pl.when(pl.program_id(1) == 0)
    def _():
        for n, wg in enumerate((g0, g1, g2, g3)):
            gb_ref[n] = wg[0].astype(gb_ref.dtype)
        wb_ref[...] = wbr_ref[0].astype(wb_ref.dtype)

    hx = hx_ref[...]
    acc = None
    for n, br in enumerate((b0, b1, b2, b3)):
        gate = jax.nn.sigmoid(_dot_nt(hx, gb_ref[n]))
        term = gate * _dot(br[...], wb_ref[n])
        acc = term if acc is None else acc + term
    o_ref[...] = acc.astype(o_ref.dtype)


def _merge(hx, w_gate, layer, row0, branches, w_br):
    m, d = hx.shape
    tm, tn = _t(m, 512), _t(d, 512)
    nj = d // tn
    bw = BRANCH_WIDTH
    in_specs = [pl.BlockSpec((tm, d), lambda j, i: (i, 0))]
    in_specs += [pl.BlockSpec((pl.Element(1), pl.Element(tn), pl.Element(d)),
                              functools.partial(lambda j, i, n: (layer, pl.multiple_of(row0 + n * d + j * tn, SUBLANES), 0), n=n),
                              pipeline_mode=pl.Buffered(1))
                 for n in range(N_BRANCH)]
    in_specs += [pl.BlockSpec((tm, bw), lambda j, i: (i, 0))] * N_BRANCH
    in_specs += [pl.BlockSpec((1, N_BRANCH, bw, tn), lambda j, i: (layer, 0, 0, j))]
    return pl.pallas_call(
        _merge_kernel, grid=(nj, m // tm), in_specs=in_specs,
        out_specs=pl.BlockSpec((tm, tn), lambda j, i: (i, j)),
        out_shape=jax.ShapeDtypeStruct((m, d), BF16),
        scratch_shapes=[pltpu.VMEM((N_BRANCH, tn, d), BF16), pltpu.VMEM((N_BRANCH, bw, tn), BF16)],
        compiler_params=_cp(("arbitrary", "arbitrary")), name="merge",
    )(hx, w_gate, w_gate, w_gate, w_gate, *branches, w_br)


def _topk_kernel(aff_ref, tri_ref, sp_ref, spt_ref, g_ref, *, cap):
    afft = aff_ref[0].T[:N_EXPERTS, :]
    bits = lax.bitcast_convert_type(afft, jnp.int32)

    def body(i, prefix):
        cand = prefix | jnp.left_shift(jnp.int32(1), 30 - i)
        cnt = jnp.sum((bits >= cand).astype(F32), axis=1, keepdims=True)
        return jnp.where(cnt >= cap, cand, prefix)

    thr = lax.fori_loop(0, 31, body, jnp.zeros((N_EXPERTS, 1), jnp.int32))
    gt = bits > thr
    eq = bits == thr
    need = cap - jnp.sum(gt.astype(F32), axis=1, keepdims=True)
    tri = tri_ref[...]
    rank_eq = _dot(eq.astype(BF16), tri)
    sel = gt | (eq & (rank_eq <= need))
    pos = _dot(sel.astype(BF16), tri) - 1.0
    sp = jnp.where(sel, pos, -1.0)
    sp_ref[0] = sp.astype(jnp.int32)
    n = sp.shape[1]
    padded = jnp.concatenate([sp, jnp.full((LANES - N_EXPERTS, n), -1.0, F32)], axis=0)
    spt_ref[0] = padded.T
    icap = g_ref.shape[2]
    slot = lax.broadcasted_iota(jnp.int32, (icap, n), 0).astype(F32)
    for e in range(N_EXPERTS):
        hit = slot == sp[e:e + 1, :]
        g_ref[0, e] = jnp.sum(jnp.where(hit, afft[e:e + 1, :], 0.0), axis=1, keepdims=True)


def _topk(aff, cap):
    b, n, _ = aff.shape
    t = jnp.arange(n, dtype=jnp.int32)
    tri = (t[:, None] <= t[None, :]).astype(BF16)
    return pl.pallas_call(
        functools.partial(_topk_kernel, cap=float(cap)), grid=(b,),
        in_specs=[pl.BlockSpec((1, n, LANES), lambda i: (i, 0, 0)), pl.BlockSpec((n, n), lambda i: (0, 0))],
        out_specs=[pl.BlockSpec((1, N_EXPERTS, n), lambda i: (i, 0, 0)), pl.BlockSpec((1, n, LANES), lambda i: (i, 0, 0)),
                   pl.BlockSpec((1, N_EXPERTS, cap, 1), lambda i: (i, 0, 0, 0))],
        out_shape=[jax.ShapeDtypeStruct((b, N_EXPERTS, n), jnp.int32), jax.ShapeDtypeStruct((b, n, LANES), F32),
                   jax.ShapeDtypeStruct((b, N_EXPERTS, cap, 1), F32)],
        compiler_params=_cp(("arbitrary",)), name="topk",
    )(aff, tri)


def _gather_kernel(sp_ref, h_ref, o_ref, *, cap):
    sp = sp_ref[0, 0]
    slot = lax.broadcasted_iota(jnp.int32, (cap, sp.shape[1]), 0)
    onehot = (slot == sp).astype(BF16)
    o_ref[0] = _dot(onehot, h_ref[0]).astype(o_ref.dtype)


def _gather(sp, h, cap):
    b, n, d = h.shape
    return pl.pallas_call(
        functools.partial(_gather_kernel, cap=cap), grid=(b, N_EXPERTS),
        in_specs=[pl.BlockSpec((1, 1, 1, n), lambda bi, e: (bi, e, 0, 0)), pl.BlockSpec((1, n, d), lambda bi, e: (bi, 0, 0))],
        out_specs=pl.BlockSpec((1, cap, d), lambda bi, e: (e, bi, 0)),
        out_shape=jax.ShapeDtypeStruct((N_EXPERTS, b * cap, d), BF16),
        compiler_params=_cp(("arbitrary", "arbitrary")), name="moe_gather",
    )(sp.reshape(b, N_EXPERTS, 1, n), h)


def _ffn_kernel(*refs, n_x):
    xs = refs[:n_x]
    gs = refs[n_x:2 * n_x]
    w1_ref, w3_ref, w2_ref = refs[2 * n_x:2 * n_x + 3]
    ys = refs[2 * n_x + 3:3 * n_x + 3]
    accs = refs[3 * n_x + 3:]
    f = pl.program_id(1)
    last = pl.num_programs(1) - 1
    w1, w3, w2 = w1_ref[0, 0].astype(BF16), w3_ref[0, 0].astype(BF16), w2_ref[0, 0].astype(BF16)

    @pl.when((pl.program_id(0) == 0) & (f == 0))
    def _():
        for acc in accs:
            acc[...] = jnp.zeros(acc.shape, acc.dtype)

    for x_ref, g_ref, y_ref, acc in zip(xs, gs, ys, accs):
        rows = x_ref.shape[1]
        grp = 512 if rows % 512 == 0 else rows
        for r in range(rows // grp):
            x = x_ref[0, r * grp:(r + 1) * grp, :]
            a = _dot(x, w1)
            hid = (a * jax.nn.sigmoid(a) * _dot(x, w3)).astype(BF16)
            part = _dot(hid, w2)
            acc[r * grp:(r + 1) * grp, :] = jnp.where(f == 0, part, acc[r * grp:(r + 1) * grp, :] + part)

        @pl.when(f == last)
        def _():
            nb, _, cap, _ = y_ref.shape
            for bi in range(nb):
                y_ref[bi, 0] = (acc[bi * cap:(bi + 1) * cap, :] * g_ref[bi, 0]).astype(y_ref.dtype)


def _expert_ffn(xs, gs, w1, w3, w2, layer):
    _, e, d, ff = w1.shape
    tf = _t(ff, 256)
    in_specs = [pl.BlockSpec((1, x.shape[1], d), lambda ei, f: (ei, 0, 0)) for x in xs]
    in_specs += [pl.BlockSpec((g.shape[0], 1, g.shape[2], 1), lambda ei, f: (0, ei, 0, 0)) for g in gs]
    in_specs += [pl.BlockSpec((1, 1, d, tf), lambda ei, f: (layer, ei, 0, f))] * 2
    in_specs += [pl.BlockSpec((1, 1, tf, d), lambda ei, f: (layer, ei, f, 0))]
    return pl.pallas_call(
        functools.partial(_ffn_kernel, n_x=len(xs)), grid=(e, ff // tf), in_specs=in_specs,
        out_specs=[pl.BlockSpec((g.shape[0], 1, g.shape[2], d), lambda ei, f: (0, ei, 0, 0)) for g in gs],
        out_shape=[jax.ShapeDtypeStruct((g.shape[0], e, g.shape[2], d), BF16) for g in gs],
        scratch_shapes=[pltpu.VMEM((x.shape[1], d), F32) for x in xs],
        compiler_params=_cp(("arbitrary", "arbitrary")), name="expert_ffn",
    )(*xs, *gs, w1, w3, w2)


def _scatter_kernel(spt_ref, y_ref, x_ref, g_ref, ng_ref, nsh_ref, nsc_ref, o_ref, h_ref, acc_ref, *, cap, kc):
    k = pl.program_id(2)
    tt = spt_ref.shape[1]
    blk = y_ref.shape[1]
    spt = spt_ref[0].astype(BF16)
    term = None
    for j in range(blk // kc):
        first = (k * blk + j * kc) // cap
        col = lax.broadcasted_iota(jnp.int32, (LANES, kc), 1)
        row = lax.broadcasted_iota(jnp.int32, (LANES, kc), 0)
        rep = (row == first + col // cap).astype(BF16)
        mine = _dot(spt, rep)
        slot = (lax.broadcasted_iota(jnp.int32, (tt, kc), 1) % cap).astype(F32)
        part = _dot((mine == slot).astype(BF16), y_ref[0, j * kc:(j + 1) * kc, :])
        term = part if term is None else term + part

    @pl.when((pl.program_id(0) == 0) & (pl.program_id(1) == 0) & (k == 0))
    def _():
        acc_ref[...] = jnp.zeros(acc_ref.shape, acc_ref.dtype)

    acc_ref[...] = jnp.where(k == 0, term, acc_ref[...] + term)

    @pl.when(k == pl.num_programs(2) - 1)
    def _():
        xn = x_ref[0] + g_ref[0] * acc_ref[...]
        o_ref[0] = xn
        yn = xn * lax.rsqrt(jnp.mean(xn * xn, axis=-1, keepdims=True) + EPS)
        h_ref[0] = ((yn * ng_ref[...]) * (1.0 + nsc_ref[0]) + nsh_ref[0]).astype(h_ref.dtype)


def _scatter(spt, y, x, g, cap, gain, shift, scale, h_dtype):
    b, n, d = x.shape
    tt = _t(n, 512)
    kc = _t(N_EXPERTS * cap, 1024)
    blk = _t(N_EXPERTS * cap, 2 * kc)
    assert kc % cap == 0 and cap <= 256
    return pl.pallas_call(
        functools.partial(_scatter_kernel, cap=cap, kc=kc), grid=(b, n // tt, N_EXPERTS * cap // blk),
        in_specs=[pl.BlockSpec((1, tt, LANES), lambda bi, i, k: (bi, i, 0)),
                  pl.BlockSpec((1, blk, d), lambda bi, i, k: (bi, k, 0)),
                  pl.BlockSpec((1, tt, d), lambda bi, i, k: (bi, i, 0)),
                  pl.BlockSpec((1, 1, d), lambda bi, i, k: (bi, 0, 0)),
                  pl.BlockSpec((1, d), lambda bi, i, k: (0, 0)),
                  pl.BlockSpec((1, 1, d), lambda bi, i, k: (bi, 0, 0)),
                  pl.BlockSpec((1, 1, d), lambda bi, i, k: (bi, 0, 0))],
        out_specs=[pl.BlockSpec((1, tt, d), lambda bi, i, k: (bi, i, 0))] * 2,
        out_shape=[jax.ShapeDtypeStruct((b, n, d), F32), jax.ShapeDtypeStruct((b, n, d), h_dtype)],
        scratch_shapes=[pltpu.VMEM((tt, d), F32)],
        compiler_params=_cp(("arbitrary", "arbitrary", "arbitrary")), name="moe_scatter",
    )(spt, y, x, g, gain.reshape(1, d), shift, scale)


def _rope_tables(s):
    rows = jnp.repeat(jnp.arange(s // GRID_W, dtype=jnp.int32), GRID_W).astype(F32)
    cols = jnp.tile(jnp.arange(GRID_W, dtype=jnp.int32), s // GRID_W).astype(F32)

    def half_tables(m):
        inv = ROPE_BASE ** (-jnp.arange(0, m, 2, dtype=F32) / m)
        out = []
        for pos in (rows, cols):
            ang = pos[:, None] * inv[None, :]
            c, sn = jnp.cos(ang), jnp.sin(ang)
            out.append((jnp.concatenate([c, c], axis=1), jnp.concatenate([-sn, sn], axis=1)))
        return (jnp.concatenate([out[0][0], out[1][0]], axis=1), jnp.concatenate([out[0][1], out[1][1]], axis=1))

    cg, sg = half_tables(GQA_HEAD_DIM // 2)
    cm, sm = half_tables(MLA_ROPE // 2)
    padm = lambda a, fill: jnp.concatenate(
        [jnp.full((s, MLA_NOPE), fill, F32), a, jnp.full((s, LANES - MLA_NOPE - MLA_ROPE), fill, F32)], axis=1)
    return jnp.stack([cg, sg]), jnp.stack([padm(cm, 1.0), padm(sm, 0.0)])


W_IN_Q = 2 * GQA_KV_HEADS * GQA_HEAD_DIM + MLA_KV_LORA + MLA_ROPE
W_IN_HY = W_IN_Q + GQA_HEADS * GQA_HEAD_DIM + MLA_Q_LORA
W_IN_CV = W_IN_HY + (HY_ORDER + 1) * HY_WIDTH
W_IN_GATE = W_IN_CV + 2 * CV_WIDTH


def _layer_weights(i, w):
    assert W_IN_Q + LANES - MLA_ROPE == ATT_KV_COLS and W_IN_HY - W_IN_Q == ATT_Q_COLS
    qb = w['mla_q_b'][i].reshape(MLA_Q_LORA, MLA_HEADS, MLA_NOPE + MLA_ROPE)
    qb = jnp.pad(qb, ((0, 0), (0, 0), (0, LANES - MLA_NOPE - MLA_ROPE))).reshape(MLA_Q_LORA, MLA_HEADS * LANES)
    kvb = w['mla_kv_b'][i].reshape(MLA_KV_LORA, MLA_HEADS, MLA_NOPE + MLA_V)
    kvb_k = jnp.pad(kvb[:, :, :MLA_NOPE], ((0, 0), (0, 0), (0, LANES - MLA_NOPE))).reshape(MLA_KV_LORA, MLA_HEADS * LANES)
    kvb_v = kvb[:, :, MLA_NOPE:].reshape(MLA_KV_LORA, MLA_HEADS * MLA_V)
    lw = {k: w[k][i] for k in ('gqa_q_gain', 'gqa_k_gain', 'mla_q_a_gain', 'mla_kv_a_gain', 'hy_conv_w', 'hy_conv_b',
                               'hf_w1', 'hf_b1', 'hf_w2', 'hf_b2', 'hf_w3', 'hf_freq', 'hf_log_rate', 'hy_bias',
                               'cv_w', 'cv_b', 'cv_ln_g', 'cv_ln_b', 'w_router', 'g_mix', 'g_ffn')}
    lw.update(layer=i, w_in_t=w['w_in_t'], q_b=qb, kv_b_k=kvb_k, kv_b_v=kvb_v, w_br=w['w_br'], w_out=w['w_out'])
    return lw


def _mixer_branches(hx, keys_extra, tabs, lw, dft):
    b, l, d = hx.shape
    hx2 = hx.reshape(b * l, d)
    proj = lambda row0, n, tn: _matmul_nt(hx2, lw['w_in_t'], lw['layer'], row0, n, F32, 1024, tn).reshape(b, l, n)
    kg, vg, km, vm, qg, qm = _attn_prep(hx, tabs, lw)
    own = (kg, vg, km, vm)
    if dft is None:
        return None, own
    ek = keys_extra
    o_gqa = _attention(qg, [(kg, vg)] + ([(ek[0], ek[1])] if ek else []), GQA_HEADS, GQA_KV_HEADS, GQA_HEAD_DIM)
    o_mla = _attention(qm, [(km, vm)] + ([(ek[2], ek[3])] if ek else []), MLA_HEADS, MLA_HEADS, MLA_NOPE + MLA_ROPE)
    phy = proj(W_IN_HY, W_IN_CV - W_IN_HY, 768)
    o_hy = _hyena(phy, lw, dft)
    pcv = proj(W_IN_CV, W_IN_GATE - W_IN_CV, 1024)
    o_cv = _dwconv(pcv, lw['cv_w'], lw['cv_b'], 1, glu=True, ln=(lw['cv_ln_g'], lw['cv_ln_b']), out_dtype=BF16)
    flat = lambda a: a.reshape(b * l, -1)
    return [flat(o_hy), flat(o_gqa), flat(o_mla), flat(o_cv)], own


def kernel(x, c, ctx, c_ctx, w_mod, b_mod, g_mix, w_in, gqa_q_gain, gqa_k_gain, mla_q_a_gain, mla_q_b, mla_kv_a_gain, mla_kv_b, hy_conv_w, hy_conv_b, hf_w1, hf_b1, hf_w2, hf_b2, hf_w3, hf_freq, hf_log_rate, hy_bias, cv_w, cv_b, cv_ln_g, cv_ln_b, w_br, w_out, g_ffn, w_router, w1, w3, w2, g_final):
    w = dict(g_mix=g_mix, w_in_t=jnp.swapaxes(w_in, 1, 2), gqa_q_gain=gqa_q_gain, gqa_k_gain=gqa_k_gain, mla_q_a_gain=mla_q_a_gain,
             mla_q_b=mla_q_b, mla_kv_a_gain=mla_kv_a_gain, mla_kv_b=mla_kv_b, hy_conv_w=hy_conv_w, hy_conv_b=hy_conv_b,
             hf_w1=hf_w1, hf_b1=hf_b1, hf_w2=hf_w2, hf_b2=hf_b2, hf_w3=hf_w3, hf_freq=hf_freq, hf_log_rate=hf_log_rate,
             hy_bias=hy_bias, cv_w=cv_w, cv_b=cv_b, cv_ln_g=cv_ln_g, cv_ln_b=cv_ln_b, w_br=w_br, w_out=w_out,
             g_ffn=g_ffn, w_router=w_router)
    bsz, s, d = x.shape
    lc = ctx.shape[1]
    depth = w_mod.shape[0]
    assert bsz < 8
    cvec = jnp.concatenate([c, c_ctx[None, :], jnp.zeros((8 - bsz - 1, d), F32)], axis=0)
    mod = _mod_all(cvec, w_mod, b_mod)
    tabs = _rope_tables(s)
    p_x, pt_x = _dft_tables(s)
    p_c, pt_c = _dft_tables(lc)
    cap_x = CAPACITY_FACTOR * s // N_EXPERTS
    cap_c = CAPACITY_FACTOR * lc // N_EXPERTS
    xc = ctx
    mods_x = [[mod[i, :bsz, j * d:(j + 1) * d].reshape(bsz, 1, d) for j in range(6)] for i in range(depth)]
    mods_c = [[jnp.broadcast_to(mod[i, bsz:bsz + 1, j * d:(j + 1) * d].reshape(1, 1, d), (bsz, 1, d)) for j in range(6)]
              for i in range(depth)]
    zero = jnp.zeros((bsz, 1, d), F32)
    hx = _norm_mod(x, g_mix[0], mods_x[0][0], mods_x[0][1])
    hc = _norm_mod(xc, g_mix[0], mods_c[0][0], mods_c[0][1])
    for i in range(depth):
        last = i == depth - 1
        lw = _layer_weights(i, w)
        mx, mc = mods_x[i], mods_c[i]
        nxt_x = (g_final, zero, zero, F32) if last else (g_mix[i + 1], mods_x[i + 1][0], mods_x[i + 1][1], BF16)

        dft_c = None if last else (p_c, pt_c, _hy_spectrum(p_c, _hy_filters(lc, lw)))
        br_c, keys_c = _mixer_branches(hc, None, None, lw, dft_c)
        dft_x = (p_x, pt_x, _hy_spectrum(p_x, _hy_filters(s, lw)))
        br_x, _ = _mixer_branches(hx, keys_c, tabs, lw, dft_x)

        merged = _merge(hx.reshape(bsz * s, d), lw['w_in_t'], i, W_IN_GATE, br_x, lw['w_br']).reshape(bsz, s, d)
        x = _matmul_residual(merged, lw['w_out'], i, x, mx[2])
        hf, aff = _norm_mod(x, lw['g_ffn'], mx[3], mx[4], lw['w_router'])
        sp, spt, gslot = _topk(aff, cap_x)
        xs, gs = [_gather(sp, hf, cap_x)], [gslot]
        if not last:
            merged_c = _merge(hc.reshape(bsz * lc, d), lw['w_in_t'], i, W_IN_GATE, br_c, lw['w_br']).reshape(bsz, lc, d)
            xc = _matmul_residual(merged_c, lw['w_out'], i, xc, mc[2])
            hfc, aff_c = _norm_mod(xc, lw['g_ffn'], mc[3], mc[4], lw['w_router'])
            sp_c, spt_c, gslot_c = _topk(aff_c, cap_c)
            xs.append(_gather(sp_c, hfc, cap_c))
            gs.append(gslot_c)
        ys = _expert_ffn(xs, gs, w1, w3, w2, i)
        x, hx = _scatter(spt, ys[0].reshape(bsz, N_EXPERTS * cap_x, d), x, mx[5], cap_x, *nxt_x)
        if not last:
            xc, hc = _scatter(spt_c, ys[1].reshape(bsz, N_EXPERTS * cap_c, d), xc, mc[5], cap_c,
                              g_mix[i + 1], mods_c[i + 1][0], mods_c[i + 1][1], BF16)
    return hx
```

```python
import functools
import math

import numpy as np
import jax
import jax.numpy as jnp
from jax import lax
from jax.experimental import pallas as pl
from jax.experimental.pallas import tpu as pltpu

F32 = jnp.float32
BF16 = jnp.bfloat16
HI = lax.Precision.HIGHEST

GRID_W = 64
EPS = 1e-6
ROPE_BASE = 10000.0
GQA_HEADS, GQA_KV_HEADS, GQA_HEAD_DIM = 4, 2, 128
MLA_HEADS, MLA_Q_LORA, MLA_KV_LORA, MLA_NOPE, MLA_ROPE, MLA_V = 4, 384, 256, 64, 32, 128
HY_WIDTH, HY_ORDER, HY_BANDS, HY_FILTER_HIDDEN = 512, 2, 16, 64
CV_WIDTH, CV_KERNEL = 512, 31
N_BRANCH, BRANCH_WIDTH = 4, 512
N_EXPERTS, EXPERT_FF, CAPACITY_FACTOR = 16, 1024, 2
LANES = 128
SUBLANES = 8
BF16_ROWS = 16
CONV_PAD = 16
VMEM_LIMIT = 56 * 1024 * 1024


def _cp(sem, vmem=VMEM_LIMIT):
    return pltpu.CompilerParams(dimension_semantics=sem, vmem_limit_bytes=vmem)


def _t(n, pref):
    return pref if n % pref == 0 else n


def _dot(a, b):
    return jnp.dot(a, b, preferred_element_type=F32)


def _mod_kernel(c_ref, w_ref, b_ref, o_ref):
    c = c_ref[...]
    a = (c * jax.nn.sigmoid(c)).astype(BF16)
    o_ref[0] = _dot(a, w_ref[0].astype(BF16)) + b_ref[0]


def _mod_all(cvec, w_mod, b_mod):
    depth, d, n6 = w_mod.shape
    tn = _t(n6, 1024)
    return pl.pallas_call(
        _mod_kernel, grid=(depth, n6 // tn),
        in_specs=[pl.BlockSpec((8, d), lambda l, j: (0, 0)),
                  pl.BlockSpec((1, d, tn), lambda l, j: (l, 0, j)),
                  pl.BlockSpec((1, 1, tn), lambda l, j: (l, 0, j))],
        out_specs=pl.BlockSpec((1, 8, tn), lambda l, j: (l, 0, j)),
        out_shape=jax.ShapeDtypeStruct((depth, 8, n6), F32),
        compiler_params=_cp(("arbitrary", "arbitrary")), name="mod",
    )(cvec, w_mod, b_mod.reshape(depth, 1, n6))


def _norm_h(x_ref, g_ref, sh_ref, sc_ref):
    x = x_ref[0]
    y = x * lax.rsqrt(jnp.mean(x * x, axis=-1, keepdims=True) + EPS)
    return (y * g_ref[...]) * (1.0 + sc_ref[0]) + sh_ref[0]


def _norm_mod_kernel(x_ref, g_ref, sh_ref, sc_ref, o_ref):
    o_ref[0] = _norm_h(x_ref, g_ref, sh_ref, sc_ref).astype(o_ref.dtype)


def _norm_router_kernel(x_ref, g_ref, sh_ref, sc_ref, wr_ref, o_ref, aff_ref):
    h = _norm_h(x_ref, g_ref, sh_ref, sc_ref)
    h_hi = h.astype(BF16)
    o_ref[0] = h_hi.astype(o_ref.dtype)
    w = wr_ref[...]
    w_hi = w.astype(BF16)
    h_lo = (h - h_hi.astype(F32)).astype(BF16)
    w_lo = (w - w_hi.astype(F32)).astype(BF16)
    logits = _dot(h_hi, w_hi) + (_dot(h_lo, w_hi) + _dot(h_hi, w_lo))
    lane = lax.broadcasted_iota(jnp.int32, logits.shape, 1)
    logits = jnp.where(lane < N_EXPERTS, logits, -1e30)
    e = jnp.exp(logits - jnp.max(logits, axis=-1, keepdims=True))
    aff_ref[0] = e / jnp.sum(e, axis=-1, keepdims=True)


def _norm_mod(x, g, sh, sc, w_router=None):
    b, l, d = x.shape
    tm = _t(l, 256)
    specs = [pl.BlockSpec((1, tm, d), lambda i, j: (i, j, 0)),
             pl.BlockSpec((1, d), lambda i, j: (0, 0)),
             pl.BlockSpec((1, 1, d), lambda i, j: (i, 0, 0)),
             pl.BlockSpec((1, 1, d), lambda i, j: (i, 0, 0))]
    h_spec = pl.BlockSpec((1, tm, d), lambda i, j: (i, j, 0))
    h_shape = jax.ShapeDtypeStruct((b, l, d), BF16)
    if w_router is None:
        return pl.pallas_call(_norm_mod_kernel, grid=(b, l // tm), in_specs=specs, out_specs=h_spec,
                              out_shape=h_shape, compiler_params=_cp(("arbitrary", "arbitrary")),
                              name="norm_mod")(x, g.reshape(1, d), sh, sc)
    wr = jnp.pad(w_router, ((0, 0), (0, LANES - w_router.shape[1])))
    return pl.pallas_call(
        _norm_router_kernel, grid=(b, l // tm),
        in_specs=specs + [pl.BlockSpec((d, LANES), lambda i, j: (0, 0))],
        out_specs=[h_spec, pl.BlockSpec((1, tm, LANES), lambda i, j: (i, j, 0))],
        out_shape=[h_shape, jax.ShapeDtypeStruct((b, l, LANES), F32)],
        compiler_params=_cp(("arbitrary", "arbitrary")), name="norm_router",
    )(x, g.reshape(1, d), sh, sc, wr)


def _dot_nt(a, w):
    return lax.dot_general(a, w, (((1,), (1,)), ((), ())), preferred_element_type=F32)


def _mm_kernel(a_ref, w_ref, o_ref, wb_ref):
    @pl.when(pl.program_id(1) == 0)
    def _():
        wb_ref[...] = w_ref[0].astype(wb_ref.dtype)

    o_ref[...] = _dot_nt(a_ref[...], wb_ref[...]).astype(o_ref.dtype)


def _matmul_nt(a, w, layer, row0, n, out_dtype, tm, tn):
    m, k = a.shape
    tm, tn = _t(m, tm), _t(n, tn)
    return pl.pallas_call(
        _mm_kernel, grid=(n // tn, m // tm),
        in_specs=[pl.BlockSpec((tm, k), lambda j, i: (i, 0)),
                  pl.BlockSpec((pl.Element(1), pl.Element(tn), pl.Element(k)),
                               lambda j, i: (layer, pl.multiple_of(row0 + j * tn, SUBLANES), 0))],
        out_specs=pl.BlockSpec((tm, tn), lambda j, i: (i, j)),
        out_shape=jax.ShapeDtypeStruct((m, n), out_dtype),
        scratch_shapes=[pltpu.VMEM((tn, k), BF16)],
        compiler_params=_cp(("arbitrary", "arbitrary")), name="matmul",
    )(a, w)


def _mm_res_kernel(a_ref, w_ref, x_ref, g_ref, o_ref, wb_ref):
    @pl.when((pl.program_id(1) == 0) & (pl.program_id(2) == 0))
    def _():
        wb_ref[...] = w_ref[0].astype(wb_ref.dtype)

    o_ref[0] = x_ref[0] + g_ref[0] * _dot(a_ref[0], wb_ref[...])


def _matmul_residual(a, w, layer, x, g):
    b, l, k = a.shape
    n = w.shape[2]
    tm, tn = _t(l, 1024), _t(n, 512)
    return pl.pallas_call(
        _mm_res_kernel, grid=(n // tn, b, l // tm),
        in_specs=[pl.BlockSpec((1, tm, k), lambda j, bi, i: (bi, i, 0)),
                  pl.BlockSpec((1, k, tn), lambda j, bi, i: (layer, 0, j)),
                  pl.BlockSpec((1, tm, tn), lambda j, bi, i: (bi, i, j)),
                  pl.BlockSpec((1, 1, tn), lambda j, bi, i: (bi, 0, j))],
        out_specs=pl.BlockSpec((1, tm, tn), lambda j, bi, i: (bi, i, j)),
        out_shape=jax.ShapeDtypeStruct((b, l, n), F32),
        scratch_shapes=[pltpu.VMEM((k, tn), BF16)],
        compiler_params=_cp(("arbitrary", "arbitrary", "arbitrary")), name="matmul_residual",
    )(a, w, x, g)


ATT_K, ATT_V, ATT_KVA, ATT_KPE, ATT_KV_COLS = 0, 256, 512, 768, 896
ATT_Q, ATT_QA, ATT_Q_COLS = 0, 512, 896


def _rope(xh, tab_ref, half):
    lane = lax.broadcasted_iota(jnp.int32, xh.shape, 1)
    partner = jnp.where((lane % (2 * half)) < half, pltpu.roll(xh, LANES - half, 1), pltpu.roll(xh, half, 1))
    return xh * tab_ref[0] + partner * tab_ref[1]


def _rms(x, gain):
    return x * lax.rsqrt(jnp.mean(x * x, axis=-1, keepdims=True) + EPS) * gain


def _prep_kernel(*refs, use_pos):
    hx_ref, wkv_ref, wq_ref = refs[:3]
    refs = refs[3:]
    if use_pos:
        tg_ref, tm_ref = refs[:2]
        refs = refs[2:]
    (gq_ref, gk_ref, gqa_ref, gkva_ref, qb_ref, kvbk_ref, kvbv_ref, kg_ref, vg_ref, km_ref, vm_ref, qg_ref, qm_ref,
     wkvb_ref, wqb_ref) = refs

    @pl.when((pl.program_id(0) == 0) & (pl.program_id(1) == 0))
    def _():
        wkvb_ref[...] = wkv_ref[0].astype(wkvb_ref.dtype)
        wqb_ref[...] = wq_ref[0].astype(wqb_ref.dtype)

    hx = hx_ref[0]
    p = _dot_nt(hx, wkvb_ref[...])
    pq = _dot_nt(hx, wqb_ref[...])
    hd = GQA_HEAD_DIM
    for h in range(GQA_KV_HEADS):
        kh = _rms(p[:, ATT_K + h * hd:ATT_K + (h + 1) * hd], gk_ref[...])
        if use_pos:
            kh = _rope(kh, tg_ref, 32)
        kg_ref[0, :, h * hd:(h + 1) * hd] = kh.astype(BF16)
    vg_ref[0] = p[:, ATT_V:ATT_V + GQA_KV_HEADS * hd].astype(BF16)
    for h in range(GQA_HEADS):
        qh = _rms(pq[:, ATT_Q + h * hd:ATT_Q + (h + 1) * hd], gq_ref[...])
        if use_pos:
            qh = _rope(qh, tg_ref, 32)
        qg_ref[0, :, h * hd:(h + 1) * hd] = qh.astype(BF16)
    kvn = _rms(p[:, ATT_KVA:ATT_KVA + MLA_KV_LORA], gkva_ref[...]).astype(BF16)
    knope = _dot(kvn, kvbk_ref[...].astype(BF16))
    vm_ref[0] = _dot(kvn, kvbv_ref[...].astype(BF16)).astype(BF16)
    kpe = p[:, ATT_KPE:ATT_KPE + LANES]
    lane = lax.broadcasted_iota(jnp.int32, kpe.shape, 1)
    kpe = pltpu.roll(jnp.where(lane < MLA_ROPE, kpe, 0.0), MLA_NOPE, 1)
    if use_pos:
        kpe = _rope(kpe, tm_ref, 8)
    qan = _rms(pq[:, ATT_QA:ATT_QA + MLA_Q_LORA], gqa_ref[...]).astype(BF16)
    qmf = _dot(qan, qb_ref[...].astype(BF16))
    for h in range(MLA_HEADS):
        km_ref[0, :, h * LANES:(h + 1) * LANES] = (knope[:, h * LANES:(h + 1) * LANES] + kpe).astype(BF16)
        qh = qmf[:, h * LANES:(h + 1) * LANES]
        if use_pos:
            qh = _rope(qh, tm_ref, 8)
        qm_ref[0, :, h * LANES:(h + 1) * LANES] = qh.astype(BF16)


def _attn_prep(hx, tabs, lw):
    b, l, d = hx.shape
    tm = _t(l, 512)
    use_pos = tabs is not None
    full = lambda shape: pl.BlockSpec(shape, lambda i, j: (0,) * len(shape))
    layer = lw['layer']
    window = lambda row0, n: pl.BlockSpec((pl.Element(1), pl.Element(n), pl.Element(d)),
                                          lambda i, j: (layer, row0, 0), pipeline_mode=pl.Buffered(1))
    in_specs = [pl.BlockSpec((1, tm, d), lambda i, j: (i, j, 0)), window(0, ATT_KV_COLS), window(W_IN_Q, ATT_Q_COLS)]
    args = [hx, lw['w_in_t'], lw['w_in_t']]
    if use_pos:
        in_specs += [pl.BlockSpec((2, tm, LANES), lambda i, j: (0, j, 0))] * 2
        args += list(tabs)
    small = [lw['gqa_q_gain'].reshape(1, -1), lw['gqa_k_gain'].reshape(1, -1), lw['mla_q_a_gain'].reshape(1, -1),
             lw['mla_kv_a_gain'].reshape(1, -1), lw['q_b'], lw['kv_b_k'], lw['kv_b_v']]
    in_specs += [full(a.shape) for a in small]
    widths = [256, 256, 512, 512, 512, 512]
    return pl.pallas_call(
        functools.partial(_prep_kernel, use_pos=use_pos), grid=(b, l // tm), in_specs=in_specs,
        out_specs=[pl.BlockSpec((1, tm, w), lambda i, j: (i, j, 0)) for w in widths],
        out_shape=[jax.ShapeDtypeStruct((b, l, w), BF16) for w in widths],
        scratch_shapes=[pltpu.VMEM((ATT_KV_COLS, d), BF16), pltpu.VMEM((ATT_Q_COLS, d), BF16)],
        compiler_params=_cp(("arbitrary", "arbitrary")), name="attn_prep",
    )(*args, *small)


def _attn_kernel(*refs, n_src, scale, nsplit):
    q_ref, o_ref = refs[0], refs[-1]
    rows = q_ref.shape[1] // nsplit
    c2 = scale * math.log2(math.e)
    for c in range(nsplit):
        q = q_ref[0, c * rows:(c + 1) * rows, :]
        ss = [lax.dot_general(q, refs[1 + 2 * j][0], (((1,), (1,)), ((), ())), preferred_element_type=F32)
              for j in range(n_src)]
        m = jnp.max(ss[0], axis=-1, keepdims=True)
        for s in ss[1:]:
            m = jnp.maximum(m, jnp.max(s, axis=-1, keepdims=True))
        acc, den = None, None
        for j, s in enumerate(ss):
            p = jnp.exp2((s - m) * c2)
            d = jnp.sum(p, axis=-1, keepdims=True)
            a = _dot(p.astype(BF16), refs[2 + 2 * j][0])
            acc = a if acc is None else acc + a
            den = d if den is None else den + d
        o_ref[0, c * rows:(c + 1) * rows, :] = (acc / den).astype(o_ref.dtype)


def _attention(q, srcs, heads, kv_heads, dk):
    b, s, _ = q.shape
    r = heads // kv_heads
    tq = _t(s, 2048)
    in_specs = [pl.BlockSpec((1, tq, LANES), lambda bi, h, i: (bi, i, h))]
    args = [q]
    for k, v in srcs:
        lk = k.shape[1]
        in_specs += [pl.BlockSpec((1, lk, LANES), lambda bi, h, i: (bi, 0, h // r))] * 2
        args += [k, v]
    return pl.pallas_call(
        functools.partial(_attn_kernel, n_src=len(srcs), scale=float(dk) ** -0.5, nsplit=tq // 256 if tq % 256 == 0 else 1),
        grid=(b, heads, s // tq), in_specs=in_specs,
        out_specs=pl.BlockSpec((1, tq, LANES), lambda bi, h, i: (bi, i, h)),
        out_shape=jax.ShapeDtypeStruct((b, s, heads * LANES), BF16),
        compiler_params=_cp(("arbitrary", "arbitrary", "arbitrary")), name="attention",
    )(*args)


def _dwconv_kernel(*refs, taps, glu, post_ln, chunk, sub):
    if glu:
        ap, ac, an, bp, bc, bn, w_ref, b_ref, lg_ref, lb_ref, o_ref, scr = refs
        load = lambda a, g: a[0] * jax.nn.sigmoid(g[0])
        prev, cur, nxt = load(ap, bp), load(ac, bc), load(an, bn)
    else:
        ap, ac, an, w_ref, b_ref, o_ref, scr = refs
        prev, cur, nxt = ap[0], ac[0], an[0]
    i = pl.program_id(2)
    last = pl.num_programs(2) - 1
    pad = CONV_PAD
    scr[0, 0:pad, :] = jnp.where(i > 0, prev, 0.0)
    scr[0, pad:pad + chunk, :] = cur
    scr[0, pad + chunk:2 * pad + chunk, :] = jnp.where(i < last, nxt, 0.0)
    lo = (taps - 1) // 2
    span = chunk + 2 * pad - SUBLANES
    for r in sorted({(pad - lo + j) % SUBLANES for j in range(taps)} - {0}):
        scr[r, 0:span, :] = scr[0, pl.ds(r, span), :]
    for c in range(chunk // sub):
        acc = None
        for j in range(taps):
            o = c * sub + pad - lo + j
            term = w_ref[j:j + 1, :] * scr[o % SUBLANES, pl.ds(o - o % SUBLANES, sub), :]
            acc = term if acc is None else acc + term
        y = acc + b_ref[...]
        if post_ln:
            mu = jnp.mean(y, axis=-1, keepdims=True)
            yc = y - mu
            var = jnp.mean(yc * yc, axis=-1, keepdims=True)
            y = yc * lax.rsqrt(var + EPS) * lg_ref[...] + lb_ref[...]
            y = y * jax.nn.sigmoid(y)
        o_ref[0, c * sub:(c + 1) * sub, :] = y.astype(o_ref.dtype)


def _dwconv(x, w, bias, ncol, *, glu=False, ln=None, out_dtype=F32, chunk=256):
    b, l, _ = x.shape
    taps = w.shape[0]
    cw = 512
    chunk = _t(l, chunk)
    nblk = l // chunk
    per = chunk // CONV_PAD
    nsmall = l // CONV_PAD

    def views(coff):
        return [pl.BlockSpec((1, CONV_PAD, cw), lambda bi, c, i: (bi, jnp.maximum(i * per - 1, 0), c + coff)),
                pl.BlockSpec((1, chunk, cw), lambda bi, c, i: (bi, i, c + coff)),
                pl.BlockSpec((1, CONV_PAD, cw), lambda bi, c, i: (bi, jnp.minimum((i + 1) * per, nsmall - 1), c + coff))]

    in_specs = views(0)
    args = [x, x, x]
    if glu:
        in_specs += views(ncol)
        args += [x, x, x]
    vec = lambda: pl.BlockSpec((1, cw), lambda bi, c, i: (0, c))
    in_specs += [pl.BlockSpec((taps, cw), lambda bi, c, i: (0, c)), vec()]
    args += [w, bias.reshape(1, -1)]
    if ln is not None:
        in_specs += [vec(), vec()]
        args += [ln[0].reshape(1, -1), ln[1].reshape(1, -1)]
    return pl.pallas_call(
        functools.partial(_dwconv_kernel, taps=taps, glu=glu, post_ln=ln is not None, chunk=chunk, sub=32),
        grid=(b, ncol, nblk), in_specs=in_specs,
        out_specs=pl.BlockSpec((1, chunk, cw), lambda bi, c, i: (bi, i, c)),
        out_shape=jax.ShapeDtypeStruct((b, l, ncol * cw), out_dtype),
        scratch_shapes=[pltpu.VMEM((SUBLANES, chunk + 2 * CONV_PAD, cw), F32)],
        compiler_params=_cp(("arbitrary", "arbitrary", "arbitrary")), name="dwconv",
    )(*args)


DFT_GROUP = 64


def _dft_kernel(ca_ref, sa_ref, cb_ref, sb_ref, p_ref, pt_ref):
    ca, sa = ca_ref[0], sa_ref[0]
    cb, sb = cb_ref[...], sb_ref[...]
    c = ca * cb - sa * sb
    s = -(sa * cb + ca * sb)
    row = pl.program_id(0) * DFT_GROUP + lax.broadcasted_iota(jnp.int32, c.shape, 0)
    col = lax.broadcasted_iota(jnp.int32, c.shape, 1)
    sign = lambda i: jnp.where(i % 2 == 0, 1.0, -1.0)
    p_ref[0] = c.astype(p_ref.dtype)
    p_ref[1] = jnp.where(row == 0, sign(col), s).astype(p_ref.dtype)
    pt_ref[0] = c.astype(pt_ref.dtype)
    pt_ref[1] = jnp.where(col == 0, sign(row), s).astype(pt_ref.dtype)


def _dft_tables(l):
    n2 = 2 * l
    g = DFT_GROUP
    n = jnp.arange(l, dtype=jnp.int32)
    ang = lambda k: ((k[:, None] * n[None, :]) & (n2 - 1)).astype(F32) * (2.0 * math.pi / n2)
    ang_a = ang(g * jnp.arange(l // g, dtype=jnp.int32)).reshape(l // g, 1, l)
    ang_b = ang(jnp.arange(g, dtype=jnp.int32))
    coarse = pl.BlockSpec((1, 1, l), lambda a: (a, 0, 0))
    fine = pl.BlockSpec((g, l), lambda a: (0, 0))
    out = pl.BlockSpec((2, g, l), lambda a: (0, a, 0))
    return pl.pallas_call(
        _dft_kernel, grid=(l // g,), in_specs=[coarse, coarse, fine, fine], out_specs=[out, out],
        out_shape=[jax.ShapeDtypeStruct((2, l, l), BF16)] * 2,
        compiler_params=_cp(("arbitrary",)), name="dft_tables",
    )(jnp.cos(ang_a), jnp.sin(ang_a), jnp.cos(ang_b), jnp.sin(ang_b))


def _hy_feats(l):
    pos = jnp.arange(l, dtype=F32)
    t01 = pos / (l - 1)
    bands = jnp.linspace(1e-4, HY_BANDS - 1, HY_BANDS, dtype=F32)
    ang = (2.0 * math.pi / l) * pos[:, None] * bands[None, :]
    feats = jnp.concatenate([t01[:, None], jnp.cos(ang), -jnp.sin(ang)], axis=-1)
    return jnp.pad(feats, ((0, 0), (0, LANES - feats.shape[1])))


def _hyfilt_kernel(f_ref, w1_ref, b1_ref, fr_ref, w2_ref, b2_ref, w3_ref, lr_ref, o_ref, *, tl):
    feats = f_ref[...]
    fr = fr_ref[...]
    hdot = lambda a, b: jnp.dot(a, b, preferred_element_type=F32, precision=HI)
    h = jnp.sin(fr * (hdot(feats, w1_ref[...]) + b1_ref[...]))
    h = jnp.sin(fr * (hdot(h, w2_ref[...]) + b2_ref[...]))
    h = hdot(h, w3_ref[...])
    h = h * jnp.exp(-feats[:, 0:1] * jnp.exp(lr_ref[...]))
    row = pl.program_id(0) * tl + lax.broadcasted_iota(jnp.int32, h.shape, 0)
    col = lax.broadcasted_iota(jnp.int32, h.shape, 1)
    is_bwd = ((col // HY_WIDTH) % 2) == 1
    o_ref[...] = jnp.where((row == 0) & is_bwd, 0.0, h).astype(o_ref.dtype)


def _hy_filters(l, lw):
    feats = _hy_feats(l)
    hid = HY_FILTER_HIDDEN
    padc = lambda a: jnp.pad(a, ((0, 0), (0, LANES - a.shape[1])))
    w1 = jnp.pad(lw['hf_w1'], ((0, LANES - lw['hf_w1'].shape[0]), (0, LANES - hid)))
    w2 = jnp.pad(lw['hf_w2'], ((0, LANES - hid), (0, LANES - hid)))
    w3 = jnp.pad(lw['hf_w3'], ((0, LANES - hid), (0, 0)))
    b1, b2, fr = padc(lw['hf_b1'].reshape(1, -1)), padc(lw['hf_b2'].reshape(1, -1)), padc(lw['hf_freq'].reshape(1, -1))
    lr = lw['hf_log_rate'].reshape(1, -1)
    nc = w3.shape[1]
    tl = _t(l, 256)
    full = lambda a: pl.BlockSpec(a.shape, lambda i: (0, 0))
    return pl.pallas_call(
        functools.partial(_hyfilt_kernel, tl=tl), grid=(l // tl,),
        in_specs=[pl.BlockSpec((tl, LANES), lambda i: (i, 0)), full(w1), full(b1), full(fr), full(w2), full(b2),
                  full(w3), full(lr)],
        out_specs=pl.BlockSpec((tl, nc), lambda i: (i, 0)),
        out_shape=jax.ShapeDtypeStruct((l, nc), BF16),
        compiler_params=_cp(("arbitrary",)), name="hy_filter",
    )(feats, w1, b1, fr, w2, b2, w3, lr)


def _hyspec_kernel(p_ref, h_ref, o_ref, *, tk, scale):
    hm = h_ref[...]
    ar = _dot(p_ref[0], hm)
    ai = _dot(p_ref[1], hm)
    c = HY_WIDTH
    row0 = (pl.program_id(0) * tk + lax.broadcasted_iota(jnp.int32, (tk, c), 0)) == 0
    s = jnp.where(row0, 0.5 * scale, scale)
    for n in range(HY_ORDER):
        o = 2 * n * c
        o_ref[n, 0] = (ar[:, o:o + c] + ar[:, o + c:o + 2 * c]) * s
        fi, bi = ai[:, o:o + c], ai[:, o + c:o + 2 * c]
        o_ref[n, 1] = jnp.where(row0, fi + bi, fi - bi) * s


def _hy_spectrum(p, filt):
    l = p.shape[1]
    tk = _t(l, 256)
    return pl.pallas_call(
        functools.partial(_hyspec_kernel, tk=tk, scale=1.0 / l), grid=(l // tk,),
        in_specs=[pl.BlockSpec((2, tk, l), lambda i: (0, i, 0)), pl.BlockSpec(filt.shape, lambda i: (0, 0))],
        out_specs=pl.BlockSpec((HY_ORDER, 2, tk, HY_WIDTH), lambda i: (0, 0, i, 0)),
        out_shape=jax.ShapeDtypeStruct((HY_ORDER, 2, l, HY_WIDTH), F32),
        compiler_params=_cp(("arbitrary",)), name="hy_spectrum",
    )(p, filt)


def _hyfwd_kernel(p_ref, z_ref, k_ref, y_ref, *, tk):
    z = z_ref[0].astype(BF16)
    xr = _dot(p_ref[0], z)
    xi = _dot(p_ref[1], z)
    kr, ki = k_ref[0, 0], k_ref[0, 1]
    row0 = (pl.program_id(0) * tk + lax.broadcasted_iota(jnp.int32, xr.shape, 0)) == 0
    xiki = xi * ki
    y_ref[0, 0] = (xr * kr - jnp.where(row0, 0.0, xiki)).astype(y_ref.dtype)
    y_ref[0, 1] = jnp.where(row0, xiki, xr * ki + xi * kr).astype(y_ref.dtype)


def _hy_forward(p, z, zcol, kf, order):
    b, l, _ = z.shape
    c = HY_WIDTH
    tk = _t(l, 1024)
    return pl.pallas_call(
        functools.partial(_hyfwd_kernel, tk=tk), grid=(l // tk, b),
        in_specs=[pl.BlockSpec((2, tk, l), lambda i, bi: (0, i, 0)),
                  pl.BlockSpec((1, l, c), lambda i, bi: (bi, 0, zcol)),
                  pl.BlockSpec((1, 2, tk, c), lambda i, bi: (order, 0, i, 0))],
        out_specs=pl.BlockSpec((1, 2, tk, c), lambda i, bi: (bi, 0, i, 0)),
        out_shape=jax.ShapeDtypeStruct((b, 2, l, c), BF16),
        compiler_params=_cp(("arbitrary", "arbitrary")), name="hy_forward",
    )(p, z, kf)


def _hyinv_kernel(pt_ref, y_ref, g_ref, z_ref, bias_ref, o_ref):
    conv = _dot(pt_ref[0], y_ref[0, 0]) + _dot(pt_ref[1], y_ref[0, 1])
    o_ref[0] = (g_ref[0] * (conv + bias_ref[...] * z_ref[0])).astype(o_ref.dtype)


def _hy_inverse(pt, y, u, gcol, z, zcol, bias, out_dtype):
    b, _, l, c = y.shape
    tn = _t(l, 1024)
    return pl.pallas_call(
        _hyinv_kernel, grid=(l // tn, b),
        in_specs=[pl.BlockSpec((2, tn, l), lambda i, bi: (0, i, 0)),
                  pl.BlockSpec((1, 2, l, c), lambda i, bi: (bi, 0, 0, 0)),
                  pl.BlockSpec((1, tn, c), lambda i, bi: (bi, i, gcol)),
                  pl.BlockSpec((1, tn, c), lambda i, bi: (bi, i, zcol)),
                  pl.BlockSpec((1, c), lambda i, bi: (0, 0))],
        out_specs=pl.BlockSpec((1, tn, c), lambda i, bi: (bi, i, 0)),
        out_shape=jax.ShapeDtypeStruct((b, l, c), out_dtype),
        compiler_params=_cp(("arbitrary", "arbitrary")), name="hy_inverse",
    )(pt, y, u, z, bias.reshape(1, c))


def _hyena(phy, lw, dft):
    p, pt, kf = dft
    u = _dwconv(phy, lw['hy_conv_w'], lw['hy_conv_b'], HY_ORDER + 1, chunk=1024)
    y = _hy_forward(p, u, 0, kf, 0)
    z = _hy_inverse(pt, y, u, 1, u, 0, lw['hy_bias'][0], F32)
    y = _hy_forward(p, z, 0, kf, 1)
    return _hy_inverse(pt, y, u, 2, z, 0, lw['hy_bias'][1], BF16)


def _merge_kernel(hx_ref, g0, g1, g2, g3, b0, b1, b2, b3, wbr_ref, o_ref, gb_ref, wb_ref):
    @pl.when(pl.program_id(1) == 0)
    def _():
        for n, wg in enumerate((g0, g1, g2, g3)):
            gb_ref[n] = wg[0].astype(gb_ref.dtype)
        wb_ref[...] = wbr_ref[0].astype(wb_ref.dtype)

    hx = hx_ref[...]
    acc = None
    for n, br in enumerate((b0, b1, b2, b3)):
        gate = jax.nn.sigmoid(_dot_nt(hx, gb_ref[n]))
        term = gate * _dot(br[...], wb_ref[n])
        acc = term if acc is None else acc + term
    o_ref[...] = acc.astype(o_ref.dtype)


def _merge(hx, w_gate, layer, row0, branches, w_br):
    m, d = hx.shape
    tm, tn = _t(m, 512), _t(d, 512)
    nj = d // tn
    bw = BRANCH_WIDTH
    in_specs = [pl.BlockSpec((tm, d), lambda j, i: (i, 0))]
    in_specs += [pl.BlockSpec((pl.Element(1), pl.Element(tn), pl.Element(d)),
                              functools.partial(lambda j, i, n: (layer, pl.multiple_of(row0 + n * d + j * tn, SUBLANES), 0), n=n),
                              pipeline_mode=pl.Buffered(1))
                 for n in range(N_BRANCH)]
    in_specs += [pl.BlockSpec((tm, bw), lambda j, i: (i, 0))] * N_BRANCH
    in_specs += [pl.BlockSpec((1, N_BRANCH, bw, tn), lambda j, i: (layer, 0, 0, j))]
    return pl.pallas_call(
        _merge_kernel, grid=(nj, m // tm), in_specs=in_specs,
        out_specs=pl.BlockSpec((tm, tn), lambda j, i: (i, j)),
        out_shape=jax.ShapeDtypeStruct((m, d), BF16),
        scratch_shapes=[pltpu.VMEM((N_BRANCH, tn, d), BF16), pltpu.VMEM((N_BRANCH, bw, tn), BF16)],
        compiler_params=_cp(("arbitrary", "arbitrary")), name="merge",
    )(hx, w_gate, w_gate, w_gate, w_gate, *branches, w_br)


def _topk_kernel(aff_ref, tri_ref, sp_ref, spt_ref, g_ref, *, cap):
    afft = aff_ref[0].T[:N_EXPERTS, :]
    bits = lax.bitcast_convert_type(afft, jnp.int32)

    def body(i, prefix):
        cand = prefix | jnp.left_shift(jnp.int32(1), 30 - i)
        cnt = jnp.sum((bits >= cand).astype(F32), axis=1, keepdims=True)
        return jnp.where(cnt >= cap, cand, prefix)

    thr = lax.fori_loop(0, 31, body, jnp.zeros((N_EXPERTS, 1), jnp.int32))
    gt = bits > thr
    eq = bits == thr
    need = cap - jnp.sum(gt.astype(F32), axis=1, keepdims=True)
    tri = tri_ref[...]
    rank_eq = _dot(eq.astype(BF16), tri)
    sel = gt | (eq & (rank_eq <= need))
    pos = _dot(sel.astype(BF16), tri) - 1.0
    sp = jnp.where(sel, pos, -1.0)
    sp_ref[0] = sp.astype(jnp.int32)
    n = sp.shape[1]
    padded = jnp.concatenate([sp, jnp.full((LANES - N_EXPERTS, n), -1.0, F32)], axis=0)
    spt_ref[0] = padded.T
    icap = g_ref.shape[2]
    slot = lax.broadcasted_iota(jnp.int32, (icap, n), 0).astype(F32)
    for e in range(N_EXPERTS):
        hit = slot == sp[e:e + 1, :]
        g_ref[0, e] = jnp.sum(jnp.where(hit, afft[e:e + 1, :], 0.0), axis=1, keepdims=True)


def _topk(aff, cap):
    b, n, _ = aff.shape
    t = jnp.arange(n, dtype=jnp.int32)
    tri = (t[:, None] <= t[None, :]).astype(BF16)
    return pl.pallas_call(
        functools.partial(_topk_kernel, cap=float(cap)), grid=(b,),
        in_specs=[pl.BlockSpec((1, n, LANES), lambda i: (i, 0, 0)), pl.BlockSpec((n, n), lambda i: (0, 0))],
        out_specs=[pl.BlockSpec((1, N_EXPERTS, n), lambda i: (i, 0, 0)), pl.BlockSpec((1, n, LANES), lambda i: (i, 0, 0)),
                   pl.BlockSpec((1, N_EXPERTS, cap, 1), lambda i: (i, 0, 0, 0))],
        out_shape=[jax.ShapeDtypeStruct((b, N_EXPERTS, n), jnp.int32), jax.ShapeDtypeStruct((b, n, LANES), F32),
                   jax.ShapeDtypeStruct((b, N_EXPERTS, cap, 1), F32)],
        compiler_params=_cp(("arbitrary",)), name="topk",
    )(aff, tri)


def _gather_kernel(sp_ref, h_ref, o_ref, *, cap):
    eg, n = sp_ref.shape[1], sp_ref.shape[3]
    slot = lax.broadcasted_iota(jnp.int32, (cap, n), 0)
    onehot = jnp.concatenate([(slot == sp_ref[0, j]).astype(BF16) for j in range(eg)], axis=0)
    rows = _dot(onehot, h_ref[0]).astype(o_ref.dtype)
    for j in range(eg):
        o_ref[j] = rows[j * cap:(j + 1) * cap, :]


def _gather(sp, h, cap):
    b, n, d = h.shape
    eg = N_EXPERTS if N_EXPERTS * cap <= 512 and cap % BF16_ROWS == 0 else 1
    return pl.pallas_call(
        functools.partial(_gather_kernel, cap=cap), grid=(b, N_EXPERTS // eg),
        in_specs=[pl.BlockSpec((1, eg, 1, n), lambda bi, e: (bi, e, 0, 0)), pl.BlockSpec((1, n, d), lambda bi, e: (bi, 0, 0))],
        out_specs=pl.BlockSpec((eg, cap, d), lambda bi, e: (e, bi, 0)),
        out_shape=jax.ShapeDtypeStruct((N_EXPERTS, b * cap, d), BF16),
        compiler_params=_cp(("arbitrary", "arbitrary")), name="moe_gather",
    )(sp.reshape(b, N_EXPERTS, 1, n), h)


def _ffn_kernel(*refs, n_x):
    xs = refs[:n_x]
    gs = refs[n_x:2 * n_x]
    w1_ref, w3_ref, w2_ref = refs[2 * n_x:2 * n_x + 3]
    ys = refs[2 * n_x + 3:3 * n_x + 3]
    accs = refs[3 * n_x + 3:]
    f = pl.program_id(1)
    last = pl.num_programs(1) - 1
    w1, w3, w2 = w1_ref[0, 0].astype(BF16), w3_ref[0, 0].astype(BF16), w2_ref[0, 0].astype(BF16)

    @pl.when((pl.program_id(0) == 0) & (f == 0))
    def _():
        for acc in accs:
            acc[...] = jnp.zeros(acc.shape, acc.dtype)

    for x_ref, g_ref, y_ref, acc in zip(xs, gs, ys, accs):
        rows = x_ref.shape[1]
        grp = 512 if rows % 512 == 0 else rows
        for r in range(rows // grp):
            x = x_ref[0, r * grp:(r + 1) * grp, :]
            a = _dot(x, w1)
            hid = (a * jax.nn.sigmoid(a) * _dot(x, w3)).astype(BF16)
            part = _dot(hid, w2)
            acc[r * grp:(r + 1) * grp, :] = jnp.where(f == 0, part, acc[r * grp:(r + 1) * grp, :] + part)

        @pl.when(f == last)
        def _():
            nb, _, cap, _ = y_ref.shape
            for bi in range(nb):
                y_ref[bi, 0] = (acc[bi * cap:(bi + 1) * cap, :] * g_ref[bi, 0]).astype(y_ref.dtype)


def _expert_ffn(xs, gs, w1, w3, w2, layer):
    _, e, d, ff = w1.shape
    tf = _t(ff, 256)
    in_specs = [pl.BlockSpec((1, x.shape[1], d), lambda ei, f: (ei, 0, 0)) for x in xs]
    in_specs += [pl.BlockSpec((g.shape[0], 1, g.shape[2], 1), lambda ei, f: (0, ei, 0, 0)) for g in gs]
    in_specs += [pl.BlockSpec((1, 1, d, tf), lambda ei, f: (layer, ei, 0, f))] * 2
    in_specs += [pl.BlockSpec((1, 1, tf, d), lambda ei, f: (layer, ei, f, 0))]
    return pl.pallas_call(
        functools.partial(_ffn_kernel, n_x=len(xs)), grid=(e, ff // tf), in_specs=in_specs,
        out_specs=[pl.BlockSpec((g.shape[0], 1, g.shape[2], d), lambda ei, f: (0, ei, 0, 0)) for g in gs],
        out_shape=[jax.ShapeDtypeStruct((g.shape[0], e, g.shape[2], d), BF16) for g in gs],
        scratch_shapes=[pltpu.VMEM((x.shape[1], d), F32) for x in xs],
        compiler_params=_cp(("arbitrary", "arbitrary")), name="expert_ffn",
    )(*xs, *gs, w1, w3, w2)


def _scatter_kernel(spt_ref, y_ref, x_ref, g_ref, ng_ref, nsh_ref, nsc_ref, o_ref, h_ref, acc_ref, *, cap, kc):
    k = pl.program_id(2)
    tt = spt_ref.shape[1]
    blk = y_ref.shape[1]
    spt = spt_ref[0].astype(BF16)
    term = None
    for j in range(blk // kc):
        first = (k * blk + j * kc) // cap
        col = lax.broadcasted_iota(jnp.int32, (LANES, kc), 1)
        row = lax.broadcasted_iota(jnp.int32, (LANES, kc), 0)
        rep = (row == first + col // cap).astype(BF16)
        mine = _dot(spt, rep)
        slot = (lax.broadcasted_iota(jnp.int32, (tt, kc), 1) % cap).astype(F32)
        part = _dot((mine == slot).astype(BF16), y_ref[0, j * kc:(j + 1) * kc, :])
        term = part if term is None else term + part

    @pl.when((pl.program_id(0) == 0) & (pl.program_id(1) == 0) & (k == 0))
    def _():
        acc_ref[...] = jnp.zeros(acc_ref.shape, acc_ref.dtype)

    acc_ref[...] = jnp.where(k == 0, term, acc_ref[...] + term)

    @pl.when(k == pl.num_programs(2) - 1)
    def _():
        xn = x_ref[0] + g_ref[0] * acc_ref[...]
        o_ref[0] = xn
        yn = xn * lax.rsqrt(jnp.mean(xn * xn, axis=-1, keepdims=True) + EPS)
        h_ref[0] = ((yn * ng_ref[...]) * (1.0 + nsc_ref[0]) + nsh_ref[0]).astype(h_ref.dtype)


def _scatter(spt, y, x, g, cap, gain, shift, scale, h_dtype):
    b, n, d = x.shape
    tt = _t(n, 512)
    kc = _t(N_EXPERTS * cap, 1024)
    blk = _t(N_EXPERTS * cap, 2 * kc)
    assert kc % cap == 0 and cap <= 256
    return pl.pallas_call(
        functools.partial(_scatter_kernel, cap=cap, kc=kc), grid=(b, n // tt, N_EXPERTS * cap // blk),
        in_specs=[pl.BlockSpec((1, tt, LANES), lambda bi, i, k: (bi, i, 0)),
                  pl.BlockSpec((1, blk, d), lambda bi, i, k: (bi, k, 0)),
                  pl.BlockSpec((1, tt, d), lambda bi, i, k: (bi, i, 0)),
                  pl.BlockSpec((1, 1, d), lambda bi, i, k: (bi, 0, 0)),
                  pl.BlockSpec((1, d), lambda bi, i, k: (0, 0)),
                  pl.BlockSpec((1, 1, d), lambda bi, i, k: (bi, 0, 0)),
                  pl.BlockSpec((1, 1, d), lambda bi, i, k: (bi, 0, 0))],
        out_specs=[pl.BlockSpec((1, tt, d), lambda bi, i, k: (bi, i, 0))] * 2,
        out_shape=[jax.ShapeDtypeStruct((b, n, d), F32), jax.ShapeDtypeStruct((b, n, d), h_dtype)],
        scratch_shapes=[pltpu.VMEM((tt, d), F32)],
        compiler_params=_cp(("arbitrary", "arbitrary", "arbitrary")), name="moe_scatter",
    )(spt, y, x, g, gain.reshape(1, d), shift, scale)


def _rope_tables(s):
    rows = jnp.repeat(jnp.arange(s // GRID_W, dtype=jnp.int32), GRID_W).astype(F32)
    cols = jnp.tile(jnp.arange(GRID_W, dtype=jnp.int32), s // GRID_W).astype(F32)

    def half_tables(m):
        inv = ROPE_BASE ** (-jnp.arange(0, m, 2, dtype=F32) / m)
        out = []
        for pos in (rows, cols):
            ang = pos[:, None] * inv[None, :]
            c, sn = jnp.cos(ang), jnp.sin(ang)
            out.append((jnp.concatenate([c, c], axis=1), jnp.concatenate([-sn, sn], axis=1)))
        return (jnp.concatenate([out[0][0], out[1][0]], axis=1), jnp.concatenate([out[0][1], out[1][1]], axis=1))

    cg, sg = half_tables(GQA_HEAD_DIM // 2)
    cm, sm = half_tables(MLA_ROPE // 2)
    padm = lambda a, fill: jnp.concatenate(
        [jnp.full((s, MLA_NOPE), fill, F32), a, jnp.full((s, LANES - MLA_NOPE - MLA_ROPE), fill, F32)], axis=1)
    return jnp.stack([cg, sg]), jnp.stack([padm(cm, 1.0), padm(sm, 0.0)])


W_IN_Q = 2 * GQA_KV_HEADS * GQA_HEAD_DIM + MLA_KV_LORA + MLA_ROPE
W_IN_HY = W_IN_Q + GQA_HEADS * GQA_HEAD_DIM + MLA_Q_LORA
W_IN_CV = W_IN_HY + (HY_ORDER + 1) * HY_WIDTH
W_IN_GATE = W_IN_CV + 2 * CV_WIDTH


def _layer_weights(i, w):
    assert W_IN_Q + LANES - MLA_ROPE == ATT_KV_COLS and W_IN_HY - W_IN_Q == ATT_Q_COLS
    qb = w['mla_q_b'][i].reshape(MLA_Q_LORA, MLA_HEADS, MLA_NOPE + MLA_ROPE)
    qb = jnp.pad(qb, ((0, 0), (0, 0), (0, LANES - MLA_NOPE - MLA_ROPE))).reshape(MLA_Q_LORA, MLA_HEADS * LANES)
    kvb = w['mla_kv_b'][i].reshape(MLA_KV_LORA, MLA_HEADS, MLA_NOPE + MLA_V)
    kvb_k = jnp.pad(kvb[:, :, :MLA_NOPE], ((0, 0), (0, 0), (0, LANES - MLA_NOPE))).reshape(MLA_KV_LORA, MLA_HEADS * LANES)
    kvb_v = kvb[:, :, MLA_NOPE:].reshape(MLA_KV_LORA, MLA_HEADS * MLA_V)
    lw = {k: w[k][i] for k in ('gqa_q_gain', 'gqa_k_gain', 'mla_q_a_gain', 'mla_kv_a_gain', 'hy_conv_w', 'hy_conv_b',
                               'hf_w1', 'hf_b1', 'hf_w2', 'hf_b2', 'hf_w3', 'hf_freq', 'hf_log_rate', 'hy_bias',
                               'cv_w', 'cv_b', 'cv_ln_g', 'cv_ln_b', 'w_router', 'g_mix', 'g_ffn')}
    lw.update(layer=i, w_in_t=w['w_in_t'], q_b=qb, kv_b_k=kvb_k, kv_b_v=kvb_v, w_br=w['w_br'], w_out=w['w_out'])
    return lw


def _mixer_branches(hx, keys_extra, tabs, lw, dft):
    b, l, d = hx.shape
    hx2 = hx.reshape(b * l, d)
    proj = lambda row0, n, tn: _matmul_nt(hx2, lw['w_in_t'], lw['layer'], row0, n, F32, 1024, tn).reshape(b, l, n)
    kg, vg, km, vm, qg, qm = _attn_prep(hx, tabs, lw)
    own = (kg, vg, km, vm)
    if dft is None:
        return None, own
    ek = keys_extra
    o_gqa = _attention(qg, [(kg, vg)] + ([(ek[0], ek[1])] if ek else []), GQA_HEADS, GQA_KV_HEADS, GQA_HEAD_DIM)
    o_mla = _attention(qm, [(km, vm)] + ([(ek[2], ek[3])] if ek else []), MLA_HEADS, MLA_HEADS, MLA_NOPE + MLA_ROPE)
    phy = proj(W_IN_HY, W_IN_CV - W_IN_HY, 768)
    o_hy = _hyena(phy, lw, dft)
    pcv = proj(W_IN_CV, W_IN_GATE - W_IN_CV, 1024)
    o_cv = _dwconv(pcv, lw['cv_w'], lw['cv_b'], 1, glu=True, ln=(lw['cv_ln_g'], lw['cv_ln_b']), out_dtype=BF16)
    flat = lambda a: a.reshape(b * l, -1)
    return [flat(o_hy), flat(o_gqa), flat(o_mla), flat(o_cv)], own


def kernel(x, c, ctx, c_ctx, w_mod, b_mod, g_mix, w_in, gqa_q_gain, gqa_k_gain, mla_q_a_gain, mla_q_b, mla_kv_a_gain, mla_kv_b, hy_conv_w, hy_conv_b, hf_w1, hf_b1, hf_w2, hf_b2, hf_w3, hf_freq, hf_log_rate, hy_bias, cv_w, cv_b, cv_ln_g, cv_ln_b, w_br, w_out, g_ffn, w_router, w1, w3, w2, g_final):
    w = dict(g_mix=g_mix, w_in_t=jnp.swapaxes(w_in, 1, 2), gqa_q_gain=gqa_q_gain, gqa_k_gain=gqa_k_gain, mla_q_a_gain=mla_q_a_gain,
             mla_q_b=mla_q_b, mla_kv_a_gain=mla_kv_a_gain, mla_kv_b=mla_kv_b, hy_conv_w=hy_conv_w, hy_conv_b=hy_conv_b,
             hf_w1=hf_w1, hf_b1=hf_b1, hf_w2=hf_w2, hf_b2=hf_b2, hf_w3=hf_w3, hf_freq=hf_freq, hf_log_rate=hf_log_rate,
             hy_bias=hy_bias, cv_w=cv_w, cv_b=cv_b, cv_ln_g=cv_ln_g, cv_ln_b=cv_ln_b, w_br=w_br, w_out=w_out,
             g_ffn=g_ffn, w_router=w_router)
    bsz, s, d = x.shape
    lc = ctx.shape[1]
    depth = w_mod.shape[0]
    assert bsz < 8
    cvec = jnp.concatenate([c, c_ctx[None, :], jnp.zeros((8 - bsz - 1, d), F32)], axis=0)
    mod = _mod_all(cvec, w_mod, b_mod)
    tabs = _rope_tables(s)
    p_x, pt_x = _dft_tables(s)
    p_c, pt_c = _dft_tables(lc)
    cap_x = CAPACITY_FACTOR * s // N_EXPERTS
    cap_c = CAPACITY_FACTOR * lc // N_EXPERTS
    xc = ctx
    mods_x = [[mod[i, :bsz, j * d:(j + 1) * d].reshape(bsz, 1, d) for j in range(6)] for i in range(depth)]
    mods_c = [[jnp.broadcast_to(mod[i, bsz:bsz + 1, j * d:(j + 1) * d].reshape(1, 1, d), (bsz, 1, d)) for j in range(6)]
              for i in range(depth)]
    zero = jnp.zeros((bsz, 1, d), F32)
    hx = _norm_mod(x, g_mix[0], mods_x[0][0], mods_x[0][1])
    hc = _norm_mod(xc, g_mix[0], mods_c[0][0], mods_c[0][1])
    for i in range(depth):
        last = i == depth - 1
        lw = _layer_weights(i, w)
        mx, mc = mods_x[i], mods_c[i]
        nxt_x = (g_final, zero, zero, F32) if last else (g_mix[i + 1], mods_x[i + 1][0], mods_x[i + 1][1], BF16)

        dft_c = None if last else (p_c, pt_c, _hy_spectrum(p_c, _hy_filters(lc, lw)))
        br_c, keys_c = _mixer_branches(hc, None, None, lw, dft_c)
        dft_x = (p_x, pt_x, _hy_spectrum(p_x, _hy_filters(s, lw)))
        br_x, _ = _mixer_branches(hx, keys_c, tabs, lw, dft_x)

        merged = _merge(hx.reshape(bsz * s, d), lw['w_in_t'], i, W_IN_GATE, br_x, lw['w_br']).reshape(bsz, s, d)
        x = _matmul_residual(merged, lw['w_out'], i, x, mx[2])
        hf, aff = _norm_mod(x, lw['g_ffn'], mx[3], mx[4], lw['w_router'])
        sp, spt, gslot = _topk(aff, cap_x)
        xs, gs = [_gather(sp, hf, cap_x)], [gslot]
        if not last:
            merged_c = _merge(hc.reshape(bsz * lc, d), lw['w_in_t'], i, W_IN_GATE, br_c, lw['w_br']).reshape(bsz, lc, d)
            xc = _matmul_residual(merged_c, lw['w_out'], i, xc, mc[2])
            hfc, aff_c = _norm_mod(xc, lw['g_ffn'], mc[3], mc[4], lw['w_router'])
            sp_c, spt_c, gslot_c = _topk(aff_c, cap_c)
            xs.append(_gather(sp_c, hfc, cap_c))
            gs.append(gslot_c)
        ys = _expert_ffn(xs, gs, w1, w3, w2, i)
        x, hx = _scatter(spt, ys[0].reshape(bsz, N_EXPERTS * cap_x, d), x, mx[5], cap_x, *nxt_x)
        if not last:
            xc, hc = _scatter(spt_c, ys[1].reshape(bsz, N_EXPERTS * cap_c, d), xc, mc[5], cap_c,
                              g_mix[i + 1], mods_c[i + 1][0], mods_c[i + 1][1], BF16)
    return hx
```

```python
import functools
import math

import numpy as np
import jax
import jax.numpy as jnp
from jax import lax
from jax.experimental import pallas as pl
from jax.experimental.pallas import tpu as pltpu

F32 = jnp.float32
BF16 = jnp.bfloat16
HI = lax.Precision.HIGHEST

GRID_W = 64
EPS = 1e-6
ROPE_BASE = 10000.0
GQA_HEADS, GQA_KV_HEADS, GQA_HEAD_DIM = 4, 2, 128
MLA_HEADS, MLA_Q_LORA, MLA_KV_LORA, MLA_NOPE, MLA_ROPE, MLA_V = 4, 384, 256, 64, 32, 128
HY_WIDTH, HY_ORDER, HY_BANDS, HY_FILTER_HIDDEN = 512, 2, 16, 64
CV_WIDTH, CV_KERNEL = 512, 31
N_BRANCH, BRANCH_WIDTH = 4, 512
N_EXPERTS, EXPERT_FF, CAPACITY_FACTOR = 16, 1024, 2
LANES = 128
SUBLANES = 8
BF16_ROWS = 16
CONV_PAD = 16
VMEM_LIMIT = 56 * 1024 * 1024


def _cp(sem, vmem=VMEM_LIMIT):
    return pltpu.CompilerParams(dimension_semantics=sem, vmem_limit_bytes=vmem)


def _t(n, pref):
    return pref if n % pref == 0 else n


def _dot(a, b):
    return jnp.dot(a, b, preferred_element_type=F32)


def _mod_kernel(c_ref, w_ref, b_ref, o_ref):
    c = c_ref[...]
    a = (c * jax.nn.sigmoid(c)).astype(BF16)
    o_ref[0] = _dot(a, w_ref[0].astype(BF16)) + b_ref[0]


def _mod_all(cvec, w_mod, b_mod):
    depth, d, n6 = w_mod.shape
    tn = _t(n6, 1024)
    return pl.pallas_call(
        _mod_kernel, grid=(depth, n6 // tn),
        in_specs=[pl.BlockSpec((8, d), lambda l, j: (0, 0)),
                  pl.BlockSpec((1, d, tn), lambda l, j: (l, 0, j)),
                  pl.BlockSpec((1, 1, tn), lambda l, j: (l, 0, j))],
        out_specs=pl.BlockSpec((1, 8, tn), lambda l, j: (l, 0, j)),
        out_shape=jax.ShapeDtypeStruct((depth, 8, n6), F32),
        compiler_params=_cp(("arbitrary", "arbitrary")), name="mod",
    )(cvec, w_mod, b_mod.reshape(depth, 1, n6))


def _norm_h(x_ref, g_ref, sh_ref, sc_ref):
    x = x_ref[0]
    y = x * lax.rsqrt(jnp.mean(x * x, axis=-1, keepdims=True) + EPS)
    return (y * g_ref[...]) * (1.0 + sc_ref[0]) + sh_ref[0]


def _norm_mod_kernel(x_ref, g_ref, sh_ref, sc_ref, o_ref):
    o_ref[0] = _norm_h(x_ref, g_ref, sh_ref, sc_ref).astype(o_ref.dtype)


def _norm_mod(x, g, sh, sc):
    b, l, d = x.shape
    tm = _t(l, 256)
    specs = [pl.BlockSpec((1, tm, d), lambda i, j: (i, j, 0)),
             pl.BlockSpec((1, d), lambda i, j: (0, 0)),
             pl.BlockSpec((1, 1, d), lambda i, j: (i, 0, 0)),
             pl.BlockSpec((1, 1, d), lambda i, j: (i, 0, 0))]
    return pl.pallas_call(_norm_mod_kernel, grid=(b, l // tm), in_specs=specs,
                          out_specs=pl.BlockSpec((1, tm, d), lambda i, j: (i, j, 0)),
                          out_shape=jax.ShapeDtypeStruct((b, l, d), BF16),
                          compiler_params=_cp(("arbitrary", "arbitrary")), name="norm_mod")(x, g.reshape(1, d), sh, sc)


def _dot_nt(a, w):
    return lax.dot_general(a, w, (((1,), (1,)), ((), ())), preferred_element_type=F32)


def _mm_kernel(a_ref, w_ref, o_ref, wb_ref):
    @pl.when(pl.program_id(1) == 0)
    def _():
        wb_ref[...] = w_ref[0].astype(wb_ref.dtype)

    o_ref[...] = _dot_nt(a_ref[...], wb_ref[...]).astype(o_ref.dtype)


def _matmul_nt(a, w, layer, row0, n, out_dtype, tm, tn):
    m, k = a.shape
    tm, tn = _t(m, tm), _t(n, tn)
    return pl.pallas_call(
        _mm_kernel, grid=(n // tn, m // tm),
        in_specs=[pl.BlockSpec((tm, k), lambda j, i: (i, 0)),
                  pl.BlockSpec((pl.Element(1), pl.Element(tn), pl.Element(k)),
                               lambda j, i: (layer, pl.multiple_of(row0 + j * tn, SUBLANES), 0))],
        out_specs=pl.BlockSpec((tm, tn), lambda j, i: (i, j)),
        out_shape=jax.ShapeDtypeStruct((m, n), out_dtype),
        scratch_shapes=[pltpu.VMEM((tn, k), BF16)],
        compiler_params=_cp(("arbitrary", "arbitrary")), name="matmul",
    )(a, w)


def _out_router_kernel(a_ref, w_ref, x_ref, g_ref, ng_ref, nsh_ref, nsc_ref, wr_ref, o_ref, h_ref, aff_ref):
    w = wr_ref[...]
    w_hi = w.astype(BF16)
    w_lo = (w - w_hi.astype(F32)).astype(BF16)
    tm = a_ref.shape[1]
    grp = 256 if tm % 256 == 0 else tm
    for r in range(tm // grp):
        rows = slice(r * grp, (r + 1) * grp)
        xn = x_ref[0, rows, :] + g_ref[0] * _dot(a_ref[0, rows, :], w_ref[0])
        o_ref[0, rows, :] = xn
        y = xn * lax.rsqrt(jnp.mean(xn * xn, axis=-1, keepdims=True) + EPS)
        h = (y * ng_ref[...]) * (1.0 + nsc_ref[0]) + nsh_ref[0]
        h_hi = h.astype(BF16)
        h_ref[0, rows, :] = h_hi.astype(h_ref.dtype)
        h_lo = (h - h_hi.astype(F32)).astype(BF16)
        logits = _dot(h_hi, w_hi) + (_dot(h_lo, w_hi) + _dot(h_hi, w_lo))
        lane = lax.broadcasted_iota(jnp.int32, logits.shape, 1)
        logits = jnp.where(lane < N_EXPERTS, logits, -1e30)
        e = jnp.exp(logits - jnp.max(logits, axis=-1, keepdims=True))
        aff_ref[0, rows, :] = e / jnp.sum(e, axis=-1, keepdims=True)


def _out_proj_router(a, w, layer, x, g, gain, shift, scale, w_router):
    b, l, k = a.shape
    n = w.shape[2]
    tm = _t(l, 512)
    wr = jnp.pad(w_router, ((0, 0), (0, LANES - w_router.shape[1])))
    row = lambda width: pl.BlockSpec((1, tm, width), lambda bi, i: (bi, i, 0))
    per_sample = pl.BlockSpec((1, 1, n), lambda bi, i: (bi, 0, 0))
    return pl.pallas_call(
        _out_router_kernel, grid=(b, l // tm),
        in_specs=[row(k), pl.BlockSpec((1, k, n), lambda bi, i: (layer, 0, 0), pipeline_mode=pl.Buffered(1)),
                  row(n), per_sample, pl.BlockSpec((1, n), lambda bi, i: (0, 0)), per_sample, per_sample,
                  pl.BlockSpec((n, LANES), lambda bi, i: (0, 0))],
        out_specs=[row(n), row(n), row(LANES)],
        out_shape=[jax.ShapeDtypeStruct((b, l, n), F32), jax.ShapeDtypeStruct((b, l, n), BF16),
                   jax.ShapeDtypeStruct((b, l, LANES), F32)],
        compiler_params=_cp(("arbitrary", "arbitrary")), name="out_proj_router",
    )(a, w, x, g, gain.reshape(1, n), shift, scale, wr)


ATT_K, ATT_V, ATT_KVA, ATT_KPE, ATT_KV_COLS = 0, 256, 512, 768, 896
ATT_Q, ATT_QA, ATT_Q_COLS = 0, 512, 896


def _rope(xh, tab_ref, half):
    lane = lax.broadcasted_iota(jnp.int32, xh.shape, 1)
    partner = jnp.where((lane % (2 * half)) < half, pltpu.roll(xh, LANES - half, 1), pltpu.roll(xh, half, 1))
    return xh * tab_ref[0] + partner * tab_ref[1]


def _rms(x, gain):
    return x * lax.rsqrt(jnp.mean(x * x, axis=-1, keepdims=True) + EPS) * gain


def _prep_kernel(*refs, use_pos):
    hx_ref, wkv_ref, wq_ref = refs[:3]
    refs = refs[3:]
    if use_pos:
        tg_ref, tm_ref = refs[:2]
        refs = refs[2:]
    (gq_ref, gk_ref, gqa_ref, gkva_ref, qb_ref, kvbk_ref, kvbv_ref, kg_ref, vg_ref, km_ref, vm_ref, qg_ref, qm_ref,
     wkvb_ref, wqb_ref) = refs

    @pl.when((pl.program_id(0) == 0) & (pl.program_id(1) == 0))
    def _():
        wkvb_ref[...] = wkv_ref[0].astype(wkvb_ref.dtype)
        wqb_ref[...] = wq_ref[0].astype(wqb_ref.dtype)

    hx = hx_ref[0]
    p = _dot_nt(hx, wkvb_ref[...])
    pq = _dot_nt(hx, wqb_ref[...])
    hd = GQA_HEAD_DIM
    for h in range(GQA_KV_HEADS):
        kh = _rms(p[:, ATT_K + h * hd:ATT_K + (h + 1) * hd], gk_ref[...])
        if use_pos:
            kh = _rope(kh, tg_ref, 32)
        kg_ref[0, :, h * hd:(h + 1) * hd] = kh.astype(BF16)
    vg_ref[0] = p[:, ATT_V:ATT_V + GQA_KV_HEADS * hd].astype(BF16)
    for h in range(GQA_HEADS):
        qh = _rms(pq[:, ATT_Q + h * hd:ATT_Q + (h + 1) * hd], gq_ref[...])
        if use_pos:
            qh = _rope(qh, tg_ref, 32)
        qg_ref[0, :, h * hd:(h + 1) * hd] = qh.astype(BF16)
    kvn = _rms(p[:, ATT_KVA:ATT_KVA + MLA_KV_LORA], gkva_ref[...]).astype(BF16)
    knope = _dot(kvn, kvbk_ref[...].astype(BF16))
    vm_ref[0] = _dot(kvn, kvbv_ref[...].astype(BF16)).astype(BF16)
    kpe = p[:, ATT_KPE:ATT_KPE + LANES]
    lane = lax.broadcasted_iota(jnp.int32, kpe.shape, 1)
    kpe = pltpu.roll(jnp.where(lane < MLA_ROPE, kpe, 0.0), MLA_NOPE, 1)
    if use_pos:
        kpe = _rope(kpe, tm_ref, 8)
    qan = _rms(pq[:, ATT_QA:ATT_QA + MLA_Q_LORA], gqa_ref[...]).astype(BF16)
    qmf = _dot(qan, qb_ref[...].astype(BF16))
    for h in range(MLA_HEADS):
        km_ref[0, :, h * LANES:(h + 1) * LANES] = (knope[:, h * LANES:(h + 1) * LANES] + kpe).astype(BF16)
        qh = qmf[:, h * LANES:(h + 1) * LANES]
        if use_pos:
            qh = _rope(qh, tm_ref, 8)
        qm_ref[0, :, h * LANES:(h + 1) * LANES] = qh.astype(BF16)


def _attn_prep(hx, tabs, lw):
    b, l, d = hx.shape
    tm = _t(l, 512)
    use_pos = tabs is not None
    full = lambda shape: pl.BlockSpec(shape, lambda i, j: (0,) * len(shape))
    layer = lw['layer']
    window = lambda row0, n: pl.BlockSpec((pl.Element(1), pl.Element(n), pl.Element(d)),
                                          lambda i, j: (layer, row0, 0), pipeline_mode=pl.Buffered(1))
    in_specs = [pl.BlockSpec((1, tm, d), lambda i, j: (i, j, 0)), window(0, ATT_KV_COLS), window(W_IN_Q, ATT_Q_COLS)]
    args = [hx, lw['w_in_t'], lw['w_in_t']]
    if use_pos:
        in_specs += [pl.BlockSpec((2, tm, LANES), lambda i, j: (0, j, 0))] * 2
        args += list(tabs)
    small = [lw['gqa_q_gain'].reshape(1, -1), lw['gqa_k_gain'].reshape(1, -1), lw['mla_q_a_gain'].reshape(1, -1),
             lw['mla_kv_a_gain'].reshape(1, -1), lw['q_b'], lw['kv_b_k'], lw['kv_b_v']]
    in_specs += [full(a.shape) for a in small]
    widths = [256, 256, 512, 512, 512, 512]
    return pl.pallas_call(
        functools.partial(_prep_kernel, use_pos=use_pos), grid=(b, l // tm), in_specs=in_specs,
        out_specs=[pl.BlockSpec((1, tm, w), lambda i, j: (i, j, 0)) for w in widths],
        out_shape=[jax.ShapeDtypeStruct((b, l, w), BF16) for w in widths],
        scratch_shapes=[pltpu.VMEM((ATT_KV_COLS, d), BF16), pltpu.VMEM((ATT_Q_COLS, d), BF16)],
        compiler_params=_cp(("arbitrary", "arbitrary")), name="attn_prep",
    )(*args, *small)


def _attn_kernel(*refs, n_src, scale, nsplit):
    q_ref, o_ref = refs[0], refs[-1]
    rows = q_ref.shape[1] // nsplit
    c2 = scale * math.log2(math.e)
    for c in range(nsplit):
        q = q_ref[0, c * rows:(c + 1) * rows, :]
        ss = [lax.dot_general(q, refs[1 + 2 * j][0], (((1,), (1,)), ((), ())), preferred_element_type=F32)
              for j in range(n_src)]
        m = jnp.max(ss[0], axis=-1, keepdims=True)
        for s in ss[1:]:
            m = jnp.maximum(m, jnp.max(s, axis=-1, keepdims=True))
        acc, den = None, None
        for j, s in enumerate(ss):
            p = jnp.exp2((s - m) * c2)
            d = jnp.sum(p, axis=-1, keepdims=True)
            a = _dot(p.astype(BF16), refs[2 + 2 * j][0])
            acc = a if acc is None else acc + a
            den = d if den is None else den + d
        o_ref[0, c * rows:(c + 1) * rows, :] = (acc / den).astype(o_ref.dtype)


def _attention(q, srcs, heads, kv_heads, dk):
    b, s, _ = q.shape
    r = heads // kv_heads
    tq = _t(s, 2048)
    in_specs = [pl.BlockSpec((1, tq, LANES), lambda bi, h, i: (bi, i, h))]
    args = [q]
    for k, v in srcs:
        lk = k.shape[1]
        in_specs += [pl.BlockSpec((1, lk, LANES), lambda bi, h, i: (bi, 0, h // r))] * 2
        args += [k, v]
    return pl.pallas_call(
        functools.partial(_attn_kernel, n_src=len(srcs), scale=float(dk) ** -0.5, nsplit=tq // 256 if tq % 256 == 0 else 1),
        grid=(b, heads, s // tq), in_specs=in_specs,
        out_specs=pl.BlockSpec((1, tq, LANES), lambda bi, h, i: (bi, i, h)),
        out_shape=jax.ShapeDtypeStruct((b, s, heads * LANES), BF16),
        compiler_params=_cp(("arbitrary", "arbitrary", "arbitrary")), name="attention",
    )(*args)


def _dwconv_kernel(*refs, taps, glu, post_ln, chunk, sub):
    if glu:
        ap, ac, an, bp, bc, bn, w_ref, b_ref, lg_ref, lb_ref, o_ref, scr = refs
        load = lambda a, g: a[0] * jax.nn.sigmoid(g[0])
        prev, cur, nxt = load(ap, bp), load(ac, bc), load(an, bn)
    else:
        ap, ac, an, w_ref, b_ref, o_ref, scr = refs
        prev, cur, nxt = ap[0], ac[0], an[0]
    i = pl.program_id(2)
    last = pl.num_programs(2) - 1
    pad = CONV_PAD
    scr[0, 0:pad, :] = jnp.where(i > 0, prev, 0.0)
    scr[0, pad:pad + chunk, :] = cur
    scr[0, pad + chunk:2 * pad + chunk, :] = jnp.where(i < last, nxt, 0.0)
    lo = (taps - 1) // 2
    span = chunk + 2 * pad - SUBLANES
    for r in sorted({(pad - lo + j) % SUBLANES for j in range(taps)} - {0}):
        scr[r, 0:span, :] = scr[0, pl.ds(r, span), :]
    for c in range(chunk // sub):
        acc = None
        for j in range(taps):
            o = c * sub + pad - lo + j
            term = w_ref[j:j + 1, :] * scr[o % SUBLANES, pl.ds(o - o % SUBLANES, sub), :]
            acc = term if acc is None else acc + term
        y = acc + b_ref[...]
        if post_ln:
            mu = jnp.mean(y, axis=-1, keepdims=True)
            yc = y - mu
            var = jnp.mean(yc * yc, axis=-1, keepdims=True)
            y = yc * lax.rsqrt(var + EPS) * lg_ref[...] + lb_ref[...]
            y = y * jax.nn.sigmoid(y)
        o_ref[0, c * sub:(c + 1) * sub, :] = y.astype(o_ref.dtype)


def _dwconv(x, w, bias, ncol, *, glu=False, ln=None, out_dtype=F32, chunk=256):
    b, l, _ = x.shape
    taps = w.shape[0]
    cw = 512
    chunk = _t(l, chunk)
    nblk = l // chunk
    per = chunk // CONV_PAD
    nsmall = l // CONV_PAD

    def views(coff):
        return [pl.BlockSpec((1, CONV_PAD, cw), lambda bi, c, i: (bi, jnp.maximum(i * per - 1, 0), c + coff)),
                pl.BlockSpec((1, chunk, cw), lambda bi, c, i: (bi, i, c + coff)),
                pl.BlockSpec((1, CONV_PAD, cw), lambda bi, c, i: (bi, jnp.minimum((i + 1) * per, nsmall - 1), c + coff))]

    in_specs = views(0)
    args = [x, x, x]
    if glu:
        in_specs += views(ncol)
        args += [x, x, x]
    vec = lambda: pl.BlockSpec((1, cw), lambda bi, c, i: (0, c))
    in_specs += [pl.BlockSpec((taps, cw), lambda bi, c, i: (0, c)), vec()]
    args += [w, bias.reshape(1, -1)]
    if ln is not None:
        in_specs += [vec(), vec()]
        args += [ln[0].reshape(1, -1), ln[1].reshape(1, -1)]
    return pl.pallas_call(
        functools.partial(_dwconv_kernel, taps=taps, glu=glu, post_ln=ln is not None, chunk=chunk, sub=32),
        grid=(b, ncol, nblk), in_specs=in_specs,
        out_specs=pl.BlockSpec((1, chunk, cw), lambda bi, c, i: (bi, i, c)),
        out_shape=jax.ShapeDtypeStruct((b, l, ncol * cw), out_dtype),
        scratch_shapes=[pltpu.VMEM((SUBLANES, chunk + 2 * CONV_PAD, cw), F32)],
        compiler_params=_cp(("arbitrary", "arbitrary", "arbitrary")), name="dwconv",
    )(*args)


DFT_GROUP = 64


def _dft_kernel(ca_ref, sa_ref, cb_ref, sb_ref, p_ref, pt_ref):
    ca, sa = ca_ref[0], sa_ref[0]
    cb, sb = cb_ref[...], sb_ref[...]
    c = ca * cb - sa * sb
    s = -(sa * cb + ca * sb)
    row = pl.program_id(0) * DFT_GROUP + lax.broadcasted_iota(jnp.int32, c.shape, 0)
    col = lax.broadcasted_iota(jnp.int32, c.shape, 1)
    sign = lambda i: jnp.where(i % 2 == 0, 1.0, -1.0)
    p_ref[0] = c.astype(p_ref.dtype)
    p_ref[1] = jnp.where(row == 0, sign(col), s).astype(p_ref.dtype)
    pt_ref[0] = c.astype(pt_ref.dtype)
    pt_ref[1] = jnp.where(col == 0, sign(row), s).astype(pt_ref.dtype)


def _dft_tables(l):
    n2 = 2 * l
    g = DFT_GROUP
    n = jnp.arange(l, dtype=jnp.int32)
    ang = lambda k: ((k[:, None] * n[None, :]) & (n2 - 1)).astype(F32) * (2.0 * math.pi / n2)
    ang_a = ang(g * jnp.arange(l // g, dtype=jnp.int32)).reshape(l // g, 1, l)
    ang_b = ang(jnp.arange(g, dtype=jnp.int32))
    coarse = pl.BlockSpec((1, 1, l), lambda a: (a, 0, 0))
    fine = pl.BlockSpec((g, l), lambda a: (0, 0))
    out = pl.BlockSpec((2, g, l), lambda a: (0, a, 0))
    return pl.pallas_call(
        _dft_kernel, grid=(l // g,), in_specs=[coarse, coarse, fine, fine], out_specs=[out, out],
        out_shape=[jax.ShapeDtypeStruct((2, l, l), BF16)] * 2,
        compiler_params=_cp(("arbitrary",)), name="dft_tables",
    )(jnp.cos(ang_a), jnp.sin(ang_a), jnp.cos(ang_b), jnp.sin(ang_b))


def _hy_feats(l):
    pos = jnp.arange(l, dtype=F32)
    t01 = pos / (l - 1)
    bands = jnp.linspace(1e-4, HY_BANDS - 1, HY_BANDS, dtype=F32)
    ang = (2.0 * math.pi / l) * pos[:, None] * bands[None, :]
    feats = jnp.concatenate([t01[:, None], jnp.cos(ang), -jnp.sin(ang)], axis=-1)
    return jnp.pad(feats, ((0, 0), (0, LANES - feats.shape[1])))


def _hyfilt_kernel(f_ref, w1_ref, b1_ref, fr_ref, w2_ref, b2_ref, w3_ref, lr_ref, o_ref, *, tl):
    feats = f_ref[...]
    fr = fr_ref[...]
    hdot = lambda a, b: jnp.dot(a, b, preferred_element_type=F32, precision=HI)
    h = jnp.sin(fr * (hdot(feats, w1_ref[...]) + b1_ref[...]))
    h = jnp.sin(fr * (hdot(h, w2_ref[...]) + b2_ref[...]))
    h = hdot(h, w3_ref[...])
    h = h * jnp.exp(-feats[:, 0:1] * jnp.exp(lr_ref[...]))
    row = pl.program_id(0) * tl + lax.broadcasted_iota(jnp.int32, h.shape, 0)
    col = lax.broadcasted_iota(jnp.int32, h.shape, 1)
    is_bwd = ((col // HY_WIDTH) % 2) == 1
    o_ref[...] = jnp.where((row == 0) & is_bwd, 0.0, h).astype(o_ref.dtype)


def _hy_filters(l, lw):
    feats = _hy_feats(l)
    hid = HY_FILTER_HIDDEN
    padc = lambda a: jnp.pad(a, ((0, 0), (0, LANES - a.shape[1])))
    w1 = jnp.pad(lw['hf_w1'], ((0, LANES - lw['hf_w1'].shape[0]), (0, LANES - hid)))
    w2 = jnp.pad(lw['hf_w2'], ((0, LANES - hid), (0, LANES - hid)))
    w3 = jnp.pad(lw['hf_w3'], ((0, LANES - hid), (0, 0)))
    b1, b2, fr = padc(lw['hf_b1'].reshape(1, -1)), padc(lw['hf_b2'].reshape(1, -1)), padc(lw['hf_freq'].reshape(1, -1))
    lr = lw['hf_log_rate'].reshape(1, -1)
    nc = w3.shape[1]
    tl = _t(l, 256)
    full = lambda a: pl.BlockSpec(a.shape, lambda i: (0, 0))
    return pl.pallas_call(
        functools.partial(_hyfilt_kernel, tl=tl), grid=(l // tl,),
        in_specs=[pl.BlockSpec((tl, LANES), lambda i: (i, 0)), full(w1), full(b1), full(fr), full(w2), full(b2),
                  full(w3), full(lr)],
        out_specs=pl.BlockSpec((tl, nc), lambda i: (i, 0)),
        out_shape=jax.ShapeDtypeStruct((l, nc), BF16),
        compiler_params=_cp(("arbitrary",)), name="hy_filter",
    )(feats, w1, b1, fr, w2, b2, w3, lr)


def _hyspec_kernel(p_ref, h_ref, o_ref, *, tk, scale):
    hm = h_ref[...]
    ar = _dot(p_ref[0], hm)
    ai = _dot(p_ref[1], hm)
    c = HY_WIDTH
    row0 = (pl.program_id(0) * tk + lax.broadcasted_iota(jnp.int32, (tk, c), 0)) == 0
    s = jnp.where(row0, 0.5 * scale, scale)
    for n in range(HY_ORDER):
        o = 2 * n * c
        o_ref[n, 0] = (ar[:, o:o + c] + ar[:, o + c:o + 2 * c]) * s
        fi, bi = ai[:, o:o + c], ai[:, o + c:o + 2 * c]
        o_ref[n, 1] = jnp.where(row0, fi + bi, fi - bi) * s


def _hy_spectrum(p, filt):
    l = p.shape[1]
    tk = _t(l, 256)
    return pl.pallas_call(
        functools.partial(_hyspec_kernel, tk=tk, scale=1.0 / l), grid=(l // tk,),
        in_specs=[pl.BlockSpec((2, tk, l), lambda i: (0, i, 0)), pl.BlockSpec(filt.shape, lambda i: (0, 0))],
        out_specs=pl.BlockSpec((HY_ORDER, 2, tk, HY_WIDTH), lambda i: (0, 0, i, 0)),
        out_shape=jax.ShapeDtypeStruct((HY_ORDER, 2, l, HY_WIDTH), F32),
        compiler_params=_cp(("arbitrary",)), name="hy_spectrum",
    )(p, filt)


def _hyfwd_kernel(p_ref, z_ref, k_ref, y_ref, *, tk):
    z = z_ref[0].astype(BF16)
    xr = _dot(p_ref[0], z)
    xi = _dot(p_ref[1], z)
    kr, ki = k_ref[0, 0], k_ref[0, 1]
    row0 = (pl.program_id(0) * tk + lax.broadcasted_iota(jnp.int32, xr.shape, 0)) == 0
    xiki = xi * ki
    y_ref[0, 0] = (xr * kr - jnp.where(row0, 0.0, xiki)).astype(y_ref.dtype)
    y_ref[0, 1] = jnp.where(row0, xiki, xr * ki + xi * kr).astype(y_ref.dtype)


def _hy_forward(p, z, zcol, kf, order):
    b, l, _ = z.shape
    c = HY_WIDTH
    tk = _t(l, 1024)
    return pl.pallas_call(
        functools.partial(_hyfwd_kernel, tk=tk), grid=(l // tk, b),
        in_specs=[pl.BlockSpec((2, tk, l), lambda i, bi: (0, i, 0)),
                  pl.BlockSpec((1, l, c), lambda i, bi: (bi, 0, zcol)),
                  pl.BlockSpec((1, 2, tk, c), lambda i, bi: (order, 0, i, 0))],
        out_specs=pl.BlockSpec((1, 2, tk, c), lambda i, bi: (bi, 0, i, 0)),
        out_shape=jax.ShapeDtypeStruct((b, 2, l, c), BF16),
        compiler_params=_cp(("arbitrary", "arbitrary")), name="hy_forward",
    )(p, z, kf)


def _hyinv_kernel(pt_ref, y_ref, g_ref, z_ref, bias_ref, o_ref):
    conv = _dot(pt_ref[0], y_ref[0, 0]) + _dot(pt_ref[1], y_ref[0, 1])
    o_ref[0] = (g_ref[0] * (conv + bias_ref[...] * z_ref[0])).astype(o_ref.dtype)


def _hy_inverse(pt, y, u, gcol, z, zcol, bias, out_dtype):
    b, _, l, c = y.shape
    tn = _t(l, 1024)
    return pl.pallas_call(
        _hyinv_kernel, grid=(l // tn, b),
        in_specs=[pl.BlockSpec((2, tn, l), lambda i, bi: (0, i, 0)),
                  pl.BlockSpec((1, 2, l, c), lambda i, bi: (bi, 0, 0, 0)),
                  pl.BlockSpec((1, tn, c), lambda i, bi: (bi, i, gcol)),
                  pl.BlockSpec((1, tn, c), lambda i, bi: (bi, i, zcol)),
                  pl.BlockSpec((1, c), lambda i, bi: (0, 0))],
        out_specs=pl.BlockSpec((1, tn, c), lambda i, bi: (bi, i, 0)),
        out_shape=jax.ShapeDtypeStruct((b, l, c), out_dtype),
        compiler_params=_cp(("arbitrary", "arbitrary")), name="hy_inverse",
    )(pt, y, u, z, bias.reshape(1, c))


def _hyena(phy, lw, dft):
    p, pt, kf = dft
    u = _dwconv(phy, lw['hy_conv_w'], lw['hy_conv_b'], HY_ORDER + 1, chunk=1024)
    y = _hy_forward(p, u, 0, kf, 0)
    z = _hy_inverse(pt, y, u, 1, u, 0, lw['hy_bias'][0], F32)
    y = _hy_forward(p, z, 0, kf, 1)
    return _hy_inverse(pt, y, u, 2, z, 0, lw['hy_bias'][1], BF16)


def _merge_kernel(hx_ref, g0, g1, g2, g3, b0, b1, b2, b3, wbr_ref, o_ref, gb_ref, wb_ref):
    @pl.when(pl.program_id(1) == 0)
    def _():
        for n, wg in enumerate((g0, g1, g2, g3)):
            gb_ref[n] = wg[0].astype(gb_ref.dtype)
        wb_ref[...] = wbr_ref[0].astype(wb_ref.dtype)

    hx = hx_ref[...]
    acc = None
    for n, br in enumerate((b0, b1, b2, b3)):
        gate = jax.nn.sigmoid(_dot_nt(hx, gb_ref[n]))
        term = gate * _dot(br[...], wb_ref[n])
        acc = term if acc is None else acc + term
    o_ref[...] = acc.astype(o_ref.dtype)


def _merge(hx, w_gate, layer, row0, branches, w_br):
    m, d = hx.shape
    tm, tn = _t(m, 512), _t(d, 512)
    nj = d // tn
    bw = BRANCH_WIDTH
    in_specs = [pl.BlockSpec((tm, d), lambda j, i: (i, 0))]
    in_specs += [pl.BlockSpec((pl.Element(1), pl.Element(tn), pl.Element(d)),
                              functools.partial(lambda j, i, n: (layer, pl.multiple_of(row0 + n * d + j * tn, SUBLANES), 0), n=n),
                              pipeline_mode=pl.Buffered(1))
                 for n in range(N_BRANCH)]
    in_specs += [pl.BlockSpec((tm, bw), lambda j, i: (i, 0))] * N_BRANCH
    in_specs += [pl.BlockSpec((1, N_BRANCH, bw, tn), lambda j, i: (layer, 0, 0, j))]
    return pl.pallas_call(
        _merge_kernel, grid=(nj, m // tm), in_specs=in_specs,
        out_specs=pl.BlockSpec((tm, tn), lambda j, i: (i, j)),
        out_shape=jax.ShapeDtypeStruct((m, d), BF16),
        scratch_shapes=[pltpu.VMEM((N_BRANCH, tn, d), BF16), pltpu.VMEM((N_BRANCH, bw, tn), BF16)],
        compiler_params=_cp(("arbitrary", "arbitrary")), name="merge",
    )(hx, w_gate, w_gate, w_gate, w_gate, *branches, w_br)


def _topk_kernel(aff_ref, tri_ref, sp_ref, spt_ref, g_ref, *, cap):
    afft = aff_ref[0].T[:N_EXPERTS, :]
    bits = lax.bitcast_convert_type(afft, jnp.int32)

    def body(i, prefix):
        cand = prefix | jnp.left_shift(jnp.int32(1), 30 - i)
        cnt = jnp.sum((bits >= cand).astype(F32), axis=1, keepdims=True)
        return jnp.where(cnt >= cap, cand, prefix)

    thr = lax.fori_loop(0, 31, body, jnp.zeros((N_EXPERTS, 1), jnp.int32))
    gt = bits > thr
    eq = bits == thr
    need = cap - jnp.sum(gt.astype(F32), axis=1, keepdims=True)
    tri = tri_ref[...]
    rank_eq = _dot(eq.astype(BF16), tri)
    sel = gt | (eq & (rank_eq <= need))
    pos = _dot(sel.astype(BF16), tri) - 1.0
    sp = jnp.where(sel, pos, -1.0)
    sp_ref[0] = sp.astype(jnp.int32)
    n = sp.shape[1]
    padded = jnp.concatenate([sp, jnp.full((LANES - N_EXPERTS, n), -1.0, F32)], axis=0)
    spt_ref[0] = padded.T
    icap = g_ref.shape[2]
    slot = lax.broadcasted_iota(jnp.int32, (icap, n), 0).astype(F32)
    for e in range(N_EXPERTS):
        hit = slot == sp[e:e + 1, :]
        g_ref[0, e] = jnp.sum(jnp.where(hit, afft[e:e + 1, :], 0.0), axis=1, keepdims=True)


def _topk(aff, cap):
    b, n, _ = aff.shape
    t = jnp.arange(n, dtype=jnp.int32)
    tri = (t[:, None] <= t[None, :]).astype(BF16)
    return pl.pallas_call(
        functools.partial(_topk_kernel, cap=float(cap)), grid=(b,),
        in_specs=[pl.BlockSpec((1, n, LANES), lambda i: (i, 0, 0)), pl.BlockSpec((n, n), lambda i: (0, 0))],
        out_specs=[pl.BlockSpec((1, N_EXPERTS, n), lambda i: (i, 0, 0)), pl.BlockSpec((1, n, LANES), lambda i: (i, 0, 0)),
                   pl.BlockSpec((1, N_EXPERTS, cap, 1), lambda i: (i, 0, 0, 0))],
        out_shape=[jax.ShapeDtypeStruct((b, N_EXPERTS, n), jnp.int32), jax.ShapeDtypeStruct((b, n, LANES), F32),
                   jax.ShapeDtypeStruct((b, N_EXPERTS, cap, 1), F32)],
        compiler_params=_cp(("arbitrary",)), name="topk",
    )(aff, tri)


def _gather_kernel(sp_ref, h_ref, o_ref, *, cap):
    eg, n = sp_ref.shape[1], sp_ref.shape[3]
    slot = lax.broadcasted_iota(jnp.int32, (cap, n), 0)
    onehot = jnp.concatenate([(slot == sp_ref[0, j]).astype(BF16) for j in range(eg)], axis=0)
    rows = _dot(onehot, h_ref[0]).astype(o_ref.dtype)
    for j in range(eg):
        o_ref[j] = rows[j * cap:(j + 1) * cap, :]


def _gather(sp, h, cap):
    b, n, d = h.shape
    eg = N_EXPERTS if N_EXPERTS * cap <= 512 and cap % BF16_ROWS == 0 else 1
    return pl.pallas_call(
        functools.partial(_gather_kernel, cap=cap), grid=(b, N_EXPERTS // eg),
        in_specs=[pl.BlockSpec((1, eg, 1, n), lambda bi, e: (bi, e, 0, 0)), pl.BlockSpec((1, n, d), lambda bi, e: (bi, 0, 0))],
        out_specs=pl.BlockSpec((eg, cap, d), lambda bi, e: (e, bi, 0)),
        out_shape=jax.ShapeDtypeStruct((N_EXPERTS, b * cap, d), BF16),
        compiler_params=_cp(("arbitrary", "arbitrary")), name="moe_gather",
    )(sp.reshape(b, N_EXPERTS, 1, n), h)


def _ffn_kernel(*refs, n_x):
    xs = refs[:n_x]
    gs = refs[n_x:2 * n_x]
    w1_ref, w3_ref, w2_ref = refs[2 * n_x:2 * n_x + 3]
    ys = refs[2 * n_x + 3:3 * n_x + 3]
    accs = refs[3 * n_x + 3:]
    f = pl.program_id(1)
    last = pl.num_programs(1) - 1
    w1, w3, w2 = w1_ref[0, 0].astype(BF16), w3_ref[0, 0].astype(BF16), w2_ref[0, 0].astype(BF16)

    @pl.when((pl.program_id(0) == 0) & (f == 0))
    def _():
        for acc in accs:
            acc[...] = jnp.zeros(acc.shape, acc.dtype)

    for x_ref, g_ref, y_ref, acc in zip(xs, gs, ys, accs):
        rows = x_ref.shape[1]
        grp = 512 if rows % 512 == 0 else rows
        for r in range(rows // grp):
            x = x_ref[0, r * grp:(r + 1) * grp, :]
            a = _dot(x, w1)
            hid = (a * jax.nn.sigmoid(a) * _dot(x, w3)).astype(BF16)
            part = _dot(hid, w2)
            acc[r * grp:(r + 1) * grp, :] = jnp.where(f == 0, part, acc[r * grp:(r + 1) * grp, :] + part)

        @pl.when(f == last)
        def _():
            nb, _, cap, _ = y_ref.shape
            for bi in range(nb):
                y_ref[bi, 0] = (acc[bi * cap:(bi + 1) * cap, :] * g_ref[bi, 0]).astype(y_ref.dtype)


def _expert_ffn(xs, gs, w1, w3, w2, layer):
    _, e, d, ff = w1.shape
    tf = _t(ff, 256)
    in_specs = [pl.BlockSpec((1, x.shape[1], d), lambda ei, f: (ei, 0, 0)) for x in xs]
    in_specs += [pl.BlockSpec((g.shape[0], 1, g.shape[2], 1), lambda ei, f: (0, ei, 0, 0)) for g in gs]
    in_specs += [pl.BlockSpec((1, 1, d, tf), lambda ei, f: (layer, ei, 0, f))] * 2
    in_specs += [pl.BlockSpec((1, 1, tf, d), lambda ei, f: (layer, ei, f, 0))]
    return pl.pallas_call(
        functools.partial(_ffn_kernel, n_x=len(xs)), grid=(e, ff // tf), in_specs=in_specs,
        out_specs=[pl.BlockSpec((g.shape[0], 1, g.shape[2], d), lambda ei, f: (0, ei, 0, 0)) for g in gs],
        out_shape=[jax.ShapeDtypeStruct((g.shape[0], e, g.shape[2], d), BF16) for g in gs],
        scratch_shapes=[pltpu.VMEM((x.shape[1], d), F32) for x in xs],
        compiler_params=_cp(("arbitrary", "arbitrary")), name="expert_ffn",
    )(*xs, *gs, w1, w3, w2)


def _scatter_kernel(spt_ref, y_ref, x_ref, g_ref, ng_ref, nsh_ref, nsc_ref, o_ref, h_ref, acc_ref, *, cap, kc):
    k = pl.program_id(2)
    tt = spt_ref.shape[1]
    blk = y_ref.shape[1]
    spt = spt_ref[0].astype(BF16)
    term = None
    for j in range(blk // kc):
        first = (k * blk + j * kc) // cap
        col = lax.broadcasted_iota(jnp.int32, (LANES, kc), 1)
        row = lax.broadcasted_iota(jnp.int32, (LANES, kc), 0)
        rep = (row == first + col // cap).astype(BF16)
        mine = _dot(spt, rep)
        slot = (lax.broadcasted_iota(jnp.int32, (tt, kc), 1) % cap).astype(F32)
        part = _dot((mine == slot).astype(BF16), y_ref[0, j * kc:(j + 1) * kc, :])
        term = part if term is None else term + part

    @pl.when((pl.program_id(0) == 0) & (pl.program_id(1) == 0) & (k == 0))
    def _():
        acc_ref[...] = jnp.zeros(acc_ref.shape, acc_ref.dtype)

    acc_ref[...] = jnp.where(k == 0, term, acc_ref[...] + term)

    @pl.when(k == pl.num_programs(2) - 1)
    def _():
        xn = x_ref[0] + g_ref[0] * acc_ref[...]
        o_ref[0] = xn
        yn = xn * lax.rsqrt(jnp.mean(xn * xn, axis=-1, keepdims=True) + EPS)
        h_ref[0] = ((yn * ng_ref[...]) * (1.0 + nsc_ref[0]) + nsh_ref[0]).astype(h_ref.dtype)


def _scatter(spt, y, x, g, cap, gain, shift, scale, h_dtype):
    b, n, d = x.shape
    tt = _t(n, 512)
    kc = _t(N_EXPERTS * cap, 1024)
    blk = _t(N_EXPERTS * cap, 2 * kc)
    assert kc % cap == 0 and cap <= 256
    return pl.pallas_call(
        functools.partial(_scatter_kernel, cap=cap, kc=kc), grid=(b, n // tt, N_EXPERTS * cap // blk),
        in_specs=[pl.BlockSpec((1, tt, LANES), lambda bi, i, k: (bi, i, 0)),
                  pl.BlockSpec((1, blk, d), lambda bi, i, k: (bi, k, 0)),
                  pl.BlockSpec((1, tt, d), lambda bi, i, k: (bi, i, 0)),
                  pl.BlockSpec((1, 1, d), lambda bi, i, k: (bi, 0, 0)),
                  pl.BlockSpec((1, d), lambda bi, i, k: (0, 0)),
                  pl.BlockSpec((1, 1, d), lambda bi, i, k: (bi, 0, 0)),
                  pl.BlockSpec((1, 1, d), lambda bi, i, k: (bi, 0, 0))],
        out_specs=[pl.BlockSpec((1, tt, d), lambda bi, i, k: (bi, i, 0))] * 2,
        out_shape=[jax.ShapeDtypeStruct((b, n, d), F32), jax.ShapeDtypeStruct((b, n, d), h_dtype)],
        scratch_shapes=[pltpu.VMEM((tt, d), F32)],
        compiler_params=_cp(("arbitrary", "arbitrary", "arbitrary")), name="moe_scatter",
    )(spt, y, x, g, gain.reshape(1, d), shift, scale)


def _rope_tables(s):
    rows = jnp.repeat(jnp.arange(s // GRID_W, dtype=jnp.int32), GRID_W).astype(F32)
    cols = jnp.tile(jnp.arange(GRID_W, dtype=jnp.int32), s // GRID_W).astype(F32)

    def half_tables(m):
        inv = ROPE_BASE ** (-jnp.arange(0, m, 2, dtype=F32) / m)
        out = []
        for pos in (rows, cols):
            ang = pos[:, None] * inv[None, :]
            c, sn = jnp.cos(ang), jnp.sin(ang)
            out.append((jnp.concatenate([c, c], axis=1), jnp.concatenate([-sn, sn], axis=1)))
        return (jnp.concatenate([out[0][0], out[1][0]], axis=1), jnp.concatenate([out[0][1], out[1][1]], axis=1))

    cg, sg = half_tables(GQA_HEAD_DIM // 2)
    cm, sm = half_tables(MLA_ROPE // 2)
    padm = lambda a, fill: jnp.concatenate(
        [jnp.full((s, MLA_NOPE), fill, F32), a, jnp.full((s, LANES - MLA_NOPE - MLA_ROPE), fill, F32)], axis=1)
    return jnp.stack([cg, sg]), jnp.stack([padm(cm, 1.0), padm(sm, 0.0)])


W_IN_Q = 2 * GQA_KV_HEADS * GQA_HEAD_DIM + MLA_KV_LORA + MLA_ROPE
W_IN_HY = W_IN_Q + GQA_HEADS * GQA_HEAD_DIM + MLA_Q_LORA
W_IN_CV = W_IN_HY + (HY_ORDER + 1) * HY_WIDTH
W_IN_GATE = W_IN_CV + 2 * CV_WIDTH


def _layer_weights(i, w):
    assert W_IN_Q + LANES - MLA_ROPE == ATT_KV_COLS and W_IN_HY - W_IN_Q == ATT_Q_COLS
    qb = w['mla_q_b'][i].reshape(MLA_Q_LORA, MLA_HEADS, MLA_NOPE + MLA_ROPE)
    qb = jnp.pad(qb, ((0, 0), (0, 0), (0, LANES - MLA_NOPE - MLA_ROPE))).reshape(MLA_Q_LORA, MLA_HEADS * LANES)
    kvb = w['mla_kv_b'][i].reshape(MLA_KV_LORA, MLA_HEADS, MLA_NOPE + MLA_V)
    kvb_k = jnp.pad(kvb[:, :, :MLA_NOPE], ((0, 0), (0, 0), (0, LANES - MLA_NOPE))).reshape(MLA_KV_LORA, MLA_HEADS * LANES)
    kvb_v = kvb[:, :, MLA_NOPE:].reshape(MLA_KV_LORA, MLA_HEADS * MLA_V)
    lw = {k: w[k][i] for k in ('gqa_q_gain', 'gqa_k_gain', 'mla_q_a_gain', 'mla_kv_a_gain', 'hy_conv_w', 'hy_conv_b',
                               'hf_w1', 'hf_b1', 'hf_w2', 'hf_b2', 'hf_w3', 'hf_freq', 'hf_log_rate', 'hy_bias',
                               'cv_w', 'cv_b', 'cv_ln_g', 'cv_ln_b', 'w_router', 'g_mix', 'g_ffn')}
    lw.update(layer=i, w_in_t=w['w_in_t'], q_b=qb, kv_b_k=kvb_k, kv_b_v=kvb_v, w_br=w['w_br'])
    return lw


def _mixer_branches(hx, keys_extra, tabs, lw, dft):
    b, l, d = hx.shape
    hx2 = hx.reshape(b * l, d)
    proj = lambda row0, n, tn: _matmul_nt(hx2, lw['w_in_t'], lw['layer'], row0, n, F32, 1024, tn).reshape(b, l, n)
    kg, vg, km, vm, qg, qm = _attn_prep(hx, tabs, lw)
    own = (kg, vg, km, vm)
    if dft is None:
        return None, own
    ek = keys_extra
    o_gqa = _attention(qg, [(kg, vg)] + ([(ek[0], ek[1])] if ek else []), GQA_HEADS, GQA_KV_HEADS, GQA_HEAD_DIM)
    o_mla = _attention(qm, [(km, vm)] + ([(ek[2], ek[3])] if ek else []), MLA_HEADS, MLA_HEADS, MLA_NOPE + MLA_ROPE)
    phy = proj(W_IN_HY, W_IN_CV - W_IN_HY, 768)
    o_hy = _hyena(phy, lw, dft)
    pcv = proj(W_IN_CV, W_IN_GATE - W_IN_CV, 1024)
    o_cv = _dwconv(pcv, lw['cv_w'], lw['cv_b'], 1, glu=True, ln=(lw['cv_ln_g'], lw['cv_ln_b']), out_dtype=BF16)
    flat = lambda a: a.reshape(b * l, -1)
    return [flat(o_hy), flat(o_gqa), flat(o_mla), flat(o_cv)], own


def kernel(x, c, ctx, c_ctx, w_mod, b_mod, g_mix, w_in, gqa_q_gain, gqa_k_gain, mla_q_a_gain, mla_q_b, mla_kv_a_gain, mla_kv_b, hy_conv_w, hy_conv_b, hf_w1, hf_b1, hf_w2, hf_b2, hf_w3, hf_freq, hf_log_rate, hy_bias, cv_w, cv_b, cv_ln_g, cv_ln_b, w_br, w_out, g_ffn, w_router, w1, w3, w2, g_final):
    w = dict(g_mix=g_mix, w_in_t=jnp.swapaxes(w_in, 1, 2), gqa_q_gain=gqa_q_gain, gqa_k_gain=gqa_k_gain, mla_q_a_gain=mla_q_a_gain,
             mla_q_b=mla_q_b, mla_kv_a_gain=mla_kv_a_gain, mla_kv_b=mla_kv_b, hy_conv_w=hy_conv_w, hy_conv_b=hy_conv_b,
             hf_w1=hf_w1, hf_b1=hf_b1, hf_w2=hf_w2, hf_b2=hf_b2, hf_w3=hf_w3, hf_freq=hf_freq, hf_log_rate=hf_log_rate,
             hy_bias=hy_bias, cv_w=cv_w, cv_b=cv_b, cv_ln_g=cv_ln_g, cv_ln_b=cv_ln_b, w_br=w_br,
             g_ffn=g_ffn, w_router=w_router)
    bsz, s, d = x.shape
    lc = ctx.shape[1]
    depth = w_mod.shape[0]
    assert bsz < 8
    cvec = jnp.concatenate([c, c_ctx[None, :], jnp.zeros((8 - bsz - 1, d), F32)], axis=0)
    mod = _mod_all(cvec, w_mod, b_mod)
    tabs = _rope_tables(s)
    p_x, pt_x = _dft_tables(s)
    p_c, pt_c = _dft_tables(lc)
    cap_x = CAPACITY_FACTOR * s // N_EXPERTS
    cap_c = CAPACITY_FACTOR * lc // N_EXPERTS
    xc = ctx
    mods_x = [[mod[i, :bsz, j * d:(j + 1) * d].reshape(bsz, 1, d) for j in range(6)] for i in range(depth)]
    mods_c = [[jnp.broadcast_to(mod[i, bsz:bsz + 1, j * d:(j + 1) * d].reshape(1, 1, d), (bsz, 1, d)) for j in range(6)]
              for i in range(depth)]
    zero = jnp.zeros((bsz, 1, d), F32)
    w_out_b = w_out.astype(BF16)
    hx = _norm_mod(x, g_mix[0], mods_x[0][0], mods_x[0][1])
    hc = _norm_mod(xc, g_mix[0], mods_c[0][0], mods_c[0][1])
    for i in range(depth):
        last = i == depth - 1
        lw = _layer_weights(i, w)
        mx, mc = mods_x[i], mods_c[i]
        nxt_x = (g_final, zero, zero, F32) if last else (g_mix[i + 1], mods_x[i + 1][0], mods_x[i + 1][1], BF16)

        dft_c = None if last else (p_c, pt_c, _hy_spectrum(p_c, _hy_filters(lc, lw)))
        br_c, keys_c = _mixer_branches(hc, None, None, lw, dft_c)
        dft_x = (p_x, pt_x, _hy_spectrum(p_x, _hy_filters(s, lw)))
        br_x, _ = _mixer_branches(hx, keys_c, tabs, lw, dft_x)

        merged = _merge(hx.reshape(bsz * s, d), lw['w_in_t'], i, W_IN_GATE, br_x, lw['w_br']).reshape(bsz, s, d)
        x, hf, aff = _out_proj_router(merged, w_out_b, i, x, mx[2], lw['g_ffn'], mx[3], mx[4], lw['w_router'])
        sp, spt, gslot = _topk(aff, cap_x)
        xs, gs = [_gather(sp, hf, cap_x)], [gslot]
        if not last:
            merged_c = _merge(hc.reshape(bsz * lc, d), lw['w_in_t'], i, W_IN_GATE, br_c, lw['w_br']).reshape(bsz, lc, d)
            xc, hfc, aff_c = _out_proj_router(merged_c, w_out_b, i, xc, mc[2], lw['g_ffn'], mc[3], mc[4], lw['w_router'])
            sp_c, spt_c, gslot_c = _topk(aff_c, cap_c)
            xs.append(_gather(sp_c, hfc, cap_c))
            gs.append(gslot_c)
        ys = _expert_ffn(xs, gs, w1, w3, w2, i)
        x, hx = _scatter(spt, ys[0].reshape(bsz, N_EXPERTS * cap_x, d), x, mx[5], cap_x, *nxt_x)
        if not last:
            xc, hc = _scatter(spt_c, ys[1].reshape(bsz, N_EXPERTS * cap_c, d), xc, mc[5], cap_c,
                              g_mix[i + 1], mods_c[i + 1][0], mods_c[i + 1][1], BF16)
    return hx
```

```python
import functools
import math

import numpy as np
import jax
import jax.numpy as jnp
from jax import lax
from jax.experimental import pallas as pl
from jax.experimental.pallas import tpu as pltpu

F32 = jnp.float32
BF16 = jnp.bfloat16
HI = lax.Precision.HIGHEST

GRID_W = 64
EPS = 1e-6
ROPE_BASE = 10000.0
GQA_HEADS, GQA_KV_HEADS, GQA_HEAD_DIM = 4, 2, 128
MLA_HEADS, MLA_Q_LORA, MLA_KV_LORA, MLA_NOPE, MLA_ROPE, MLA_V = 4, 384, 256, 64, 32, 128
HY_WIDTH, HY_ORDER, HY_BANDS, HY_FILTER_HIDDEN = 512, 2, 16, 64
CV_WIDTH, CV_KERNEL = 512, 31
N_BRANCH, BRANCH_WIDTH = 4, 512
N_EXPERTS, EXPERT_FF, CAPACITY_FACTOR = 16, 1024, 2
LANES = 128
SUBLANES = 8
BF16_ROWS = 16
CONV_PAD = 16
VMEM_LIMIT = 56 * 1024 * 1024


def _cp(sem, vmem=VMEM_LIMIT):
    return pltpu.CompilerParams(dimension_semantics=sem, vmem_limit_bytes=vmem)


def _t(n, pref):
    return pref if n % pref == 0 else n


def _dot(a, b):
    return jnp.dot(a, b, preferred_element_type=F32)


def _mod_kernel(c_ref, w_ref, b_ref, o_ref):
    c = c_ref[...]
    a = (c * jax.nn.sigmoid(c)).astype(BF16)
    o_ref[0] = _dot(a, w_ref[0].astype(BF16)) + b_ref[0]


def _mod_all(cvec, w_mod, b_mod):
    depth, d, n6 = w_mod.shape
    tn = _t(n6, 1024)
    return pl.pallas_call(
        _mod_kernel, grid=(depth, n6 // tn),
        in_specs=[pl.BlockSpec((8, d), lambda l, j: (0, 0)),
                  pl.BlockSpec((1, d, tn), lambda l, j: (l, 0, j)),
                  pl.BlockSpec((1, 1, tn), lambda l, j: (l, 0, j))],
        out_specs=pl.BlockSpec((1, 8, tn), lambda l, j: (l, 0, j)),
        out_shape=jax.ShapeDtypeStruct((depth, 8, n6), F32),
        compiler_params=_cp(("arbitrary", "arbitrary")), name="mod",
    )(cvec, w_mod, b_mod.reshape(depth, 1, n6))


def _norm_h(x_ref, g_ref, sh_ref, sc_ref):
    x = x_ref[0]
    y = x * lax.rsqrt(jnp.mean(x * x, axis=-1, keepdims=True) + EPS)
    return (y * g_ref[...]) * (1.0 + sc_ref[0]) + sh_ref[0]


def _norm_mod_kernel(x_ref, g_ref, sh_ref, sc_ref, o_ref):
    o_ref[0] = _norm_h(x_ref, g_ref, sh_ref, sc_ref).astype(o_ref.dtype)


def _norm_mod(x, g, sh, sc):
    b, l, d = x.shape
    tm = _t(l, 256)
    specs = [pl.BlockSpec((1, tm, d), lambda i, j: (i, j, 0)),
             pl.BlockSpec((1, d), lambda i, j: (0, 0)),
             pl.BlockSpec((1, 1, d), lambda i, j: (i, 0, 0)),
             pl.BlockSpec((1, 1, d), lambda i, j: (i, 0, 0))]
    return pl.pallas_call(_norm_mod_kernel, grid=(b, l // tm), in_specs=specs,
                          out_specs=pl.BlockSpec((1, tm, d), lambda i, j: (i, j, 0)),
                          out_shape=jax.ShapeDtypeStruct((b, l, d), BF16),
                          compiler_params=_cp(("arbitrary", "arbitrary")), name="norm_mod")(x, g.reshape(1, d), sh, sc)


def _dot_nt(a, w):
    return lax.dot_general(a, w, (((1,), (1,)), ((), ())), preferred_element_type=F32)


def _mm_kernel(a_ref, w_ref, o_ref, wb_ref):
    @pl.when(pl.program_id(1) == 0)
    def _():
        wb_ref[...] = w_ref[0].astype(wb_ref.dtype)

    o_ref[...] = _dot_nt(a_ref[...], wb_ref[...]).astype(o_ref.dtype)


def _matmul_nt(a, w, layer, row0, n, out_dtype, tm, tn):
    m, k = a.shape
    tm, tn = _t(m, tm), _t(n, tn)
    return pl.pallas_call(
        _mm_kernel, grid=(n // tn, m // tm),
        in_specs=[pl.BlockSpec((tm, k), lambda j, i: (i, 0)),
                  pl.BlockSpec((pl.Element(1), pl.Element(tn), pl.Element(k)),
                               lambda j, i: (layer, pl.multiple_of(row0 + j * tn, SUBLANES), 0))],
        out_specs=pl.BlockSpec((tm, tn), lambda j, i: (i, j)),
        out_shape=jax.ShapeDtypeStruct((m, n), out_dtype),
        scratch_shapes=[pltpu.VMEM((tn, k), BF16)],
        compiler_params=_cp(("arbitrary", "arbitrary")), name="matmul",
    )(a, w)


def _out_router_kernel(a_ref, w_ref, x_ref, g_ref, ng_ref, nsh_ref, nsc_ref, wr_ref, o_ref, h_ref, aff_ref):
    w = wr_ref[...]
    w_hi = w.astype(BF16)
    w_lo = (w - w_hi.astype(F32)).astype(BF16)
    tm = a_ref.shape[1]
    grp = 256 if tm % 256 == 0 else tm
    for r in range(tm // grp):
        rows = slice(r * grp, (r + 1) * grp)
        xn = x_ref[0, rows, :] + g_ref[0] * _dot(a_ref[0, rows, :], w_ref[0])
        o_ref[0, rows, :] = xn
        y = xn * lax.rsqrt(jnp.mean(xn * xn, axis=-1, keepdims=True) + EPS)
        h = (y * ng_ref[...]) * (1.0 + nsc_ref[0]) + nsh_ref[0]
        h_hi = h.astype(BF16)
        h_ref[0, rows, :] = h_hi.astype(h_ref.dtype)
        h_lo = (h - h_hi.astype(F32)).astype(BF16)
        logits = _dot(h_hi, w_hi) + (_dot(h_lo, w_hi) + _dot(h_hi, w_lo))
        lane = lax.broadcasted_iota(jnp.int32, logits.shape, 1)
        logits = jnp.where(lane < N_EXPERTS, logits, -1e30)
        e = jnp.exp(logits - jnp.max(logits, axis=-1, keepdims=True))
        aff_ref[0, rows, :] = e / jnp.sum(e, axis=-1, keepdims=True)


def _out_proj_router(a, w, layer, x, g, gain, shift, scale, w_router):
    b, l, k = a.shape
    n = w.shape[2]
    tm = _t(l, 512)
    wr = jnp.pad(w_router, ((0, 0), (0, LANES - w_router.shape[1])))
    row = lambda width: pl.BlockSpec((1, tm, width), lambda bi, i: (bi, i, 0))
    per_sample = pl.BlockSpec((1, 1, n), lambda bi, i: (bi, 0, 0))
    return pl.pallas_call(
        _out_router_kernel, grid=(b, l // tm),
        in_specs=[row(k), pl.BlockSpec((1, k, n), lambda bi, i: (layer, 0, 0), pipeline_mode=pl.Buffered(1)),
                  row(n), per_sample, pl.BlockSpec((1, n), lambda bi, i: (0, 0)), per_sample, per_sample,
                  pl.BlockSpec((n, LANES), lambda bi, i: (0, 0))],
        out_specs=[row(n), row(n), row(LANES)],
        out_shape=[jax.ShapeDtypeStruct((b, l, n), F32), jax.ShapeDtypeStruct((b, l, n), BF16),
                   jax.ShapeDtypeStruct((b, l, LANES), F32)],
        compiler_params=_cp(("arbitrary", "arbitrary")), name="out_proj_router",
    )(a, w, x, g, gain.reshape(1, n), shift, scale, wr)


ATT_K, ATT_V, ATT_KVA, ATT_KPE, ATT_KV_COLS = 0, 256, 512, 768, 896
ATT_Q, ATT_QA, ATT_Q_COLS = 0, 512, 896


def _rope(xh, tab, half):
    lane = lax.broadcasted_iota(jnp.int32, xh.shape, 1)
    partner = jnp.where((lane % (2 * half)) < half, pltpu.roll(xh, LANES - half, 1), pltpu.roll(xh, half, 1))
    return xh * tab[0] + partner * tab[1]


def _rms(x, gain):
    return x * lax.rsqrt(jnp.mean(x * x, axis=-1, keepdims=True) + EPS) * gain


def _prep_kernel(*refs, use_pos):
    hx_ref, wkv_ref, wq_ref = refs[:3]
    refs = refs[3:]
    if use_pos:
        tg_ref, tm_ref = refs[:2]
        refs = refs[2:]
    (gq_ref, gk_ref, gqa_ref, gkva_ref, qb_ref, kvbk_ref, kvbv_ref, kg_ref, vg_ref, km_ref, vm_ref, qg_ref, qm_ref,
     wkvb_ref, wqb_ref) = refs

    @pl.when((pl.program_id(0) == 0) & (pl.program_id(1) == 0))
    def _():
        wkvb_ref[...] = wkv_ref[0].astype(wkvb_ref.dtype)
        wqb_ref[...] = wq_ref[0].astype(wqb_ref.dtype)

    tm = hx_ref.shape[1]
    grp = 256 if tm % 256 == 0 else tm
    hd = GQA_HEAD_DIM
    for r in range(tm // grp):
        rs = slice(r * grp, (r + 1) * grp)
        hx = hx_ref[0, rs, :]
        p = _dot_nt(hx, wkvb_ref[...])
        pq = _dot_nt(hx, wqb_ref[...])
        if use_pos:
            tg = (tg_ref[0, rs, :], tg_ref[1, rs, :])
            tml = (tm_ref[0, rs, :], tm_ref[1, rs, :])
        for h in range(GQA_KV_HEADS):
            kh = _rms(p[:, ATT_K + h * hd:ATT_K + (h + 1) * hd], gk_ref[...])
            if use_pos:
                kh = _rope(kh, tg, 32)
            kg_ref[0, rs, h * hd:(h + 1) * hd] = kh.astype(BF16)
        vg_ref[0, rs, :] = p[:, ATT_V:ATT_V + GQA_KV_HEADS * hd].astype(BF16)
        for h in range(GQA_HEADS):
            qh = _rms(pq[:, ATT_Q + h * hd:ATT_Q + (h + 1) * hd], gq_ref[...])
            if use_pos:
                qh = _rope(qh, tg, 32)
            qg_ref[0, rs, h * hd:(h + 1) * hd] = qh.astype(BF16)
        kvn = _rms(p[:, ATT_KVA:ATT_KVA + MLA_KV_LORA], gkva_ref[...]).astype(BF16)
        knope = _dot(kvn, kvbk_ref[...].astype(BF16))
        vm_ref[0, rs, :] = _dot(kvn, kvbv_ref[...].astype(BF16)).astype(BF16)
        kpe = p[:, ATT_KPE:ATT_KPE + LANES]
        lane = lax.broadcasted_iota(jnp.int32, kpe.shape, 1)
        kpe = pltpu.roll(jnp.where(lane < MLA_ROPE, kpe, 0.0), MLA_NOPE, 1)
        if use_pos:
            kpe = _rope(kpe, tml, 8)
        qan = _rms(pq[:, ATT_QA:ATT_QA + MLA_Q_LORA], gqa_ref[...]).astype(BF16)
        qmf = _dot(qan, qb_ref[...].astype(BF16))
        for h in range(MLA_HEADS):
            km_ref[0, rs, h * LANES:(h + 1) * LANES] = (knope[:, h * LANES:(h + 1) * LANES] + kpe).astype(BF16)
            qh = qmf[:, h * LANES:(h + 1) * LANES]
            if use_pos:
                qh = _rope(qh, tml, 8)
            qm_ref[0, rs, h * LANES:(h + 1) * LANES] = qh.astype(BF16)


def _attn_prep(hx, tabs, lw):
    b, l, d = hx.shape
    tm = _t(l, 512)
    use_pos = tabs is not None
    full = lambda shape: pl.BlockSpec(shape, lambda i, j: (0,) * len(shape))
    layer = lw['layer']
    window = lambda row0, n: pl.BlockSpec((pl.Element(1), pl.Element(n), pl.Element(d)),
                                          lambda i, j: (layer, row0, 0), pipeline_mode=pl.Buffered(1))
    in_specs = [pl.BlockSpec((1, tm, d), lambda i, j: (i, j, 0)), window(0, ATT_KV_COLS), window(W_IN_Q, ATT_Q_COLS)]
    args = [hx, lw['w_in_t'], lw['w_in_t']]
    if use_pos:
        in_specs += [pl.BlockSpec((2, tm, LANES), lambda i, j: (0, j, 0))] * 2
        args += list(tabs)
    small = [lw['gqa_q_gain'].reshape(1, -1), lw['gqa_k_gain'].reshape(1, -1), lw['mla_q_a_gain'].reshape(1, -1),
             lw['mla_kv_a_gain'].reshape(1, -1), lw['q_b'], lw['kv_b_k'], lw['kv_b_v']]
    in_specs += [full(a.shape) for a in small]
    widths = [256, 256, 512, 512, 512, 512]
    return pl.pallas_call(
        functools.partial(_prep_kernel, use_pos=use_pos), grid=(b, l // tm), in_specs=in_specs,
        out_specs=[pl.BlockSpec((1, tm, w), lambda i, j: (i, j, 0)) for w in widths],
        out_shape=[jax.ShapeDtypeStruct((b, l, w), BF16) for w in widths],
        scratch_shapes=[pltpu.VMEM((ATT_KV_COLS, d), BF16), pltpu.VMEM((ATT_Q_COLS, d), BF16)],
        compiler_params=_cp(("arbitrary", "arbitrary")), name="attn_prep",
    )(*args, *small)


def _attn_kernel(*refs, n_src, scale, nsplit):
    q_ref, o_ref = refs[0], refs[-1]
    rows = q_ref.shape[1] // nsplit
    c2 = scale * math.log2(math.e)
    for c in range(nsplit):
        q = q_ref[0, c * rows:(c + 1) * rows, :]
        ss = [lax.dot_general(q, refs[1 + 2 * j][0], (((1,), (1,)), ((), ())), preferred_element_type=F32)
              for j in range(n_src)]
        m = jnp.max(ss[0], axis=-1, keepdims=True)
        for s in ss[1:]:
            m = jnp.maximum(m, jnp.max(s, axis=-1, keepdims=True))
        acc, den = None, None
        for j, s in enumerate(ss):
            p = jnp.exp2((s - m) * c2)
            d = jnp.sum(p, axis=-1, keepdims=True)
            a = _dot(p.astype(BF16), refs[2 + 2 * j][0])
            acc = a if acc is None else acc + a
            den = d if den is None else den + d
        o_ref[0, c * rows:(c + 1) * rows, :] = (acc / den).astype(o_ref.dtype)


def _attention(q, srcs, heads, kv_heads, dk):
    b, s, _ = q.shape
    r = heads // kv_heads
    tq = _t(s, 2048)
    in_specs = [pl.BlockSpec((1, tq, LANES), lambda bi, h, i: (bi, i, h))]
    args = [q]
    for k, v in srcs:
        lk = k.shape[1]
        in_specs += [pl.BlockSpec((1, lk, LANES), lambda bi, h, i: (bi, 0, h // r))] * 2
        args += [k, v]
    return pl.pallas_call(
        functools.partial(_attn_kernel, n_src=len(srcs), scale=float(dk) ** -0.5, nsplit=tq // 256 if tq % 256 == 0 else 1),
        grid=(b, heads, s // tq), in_specs=in_specs,
        out_specs=pl.BlockSpec((1, tq, LANES), lambda bi, h, i: (bi, i, h)),
        out_shape=jax.ShapeDtypeStruct((b, s, heads * LANES), BF16),
        compiler_params=_cp(("arbitrary", "arbitrary", "arbitrary")), name="attention",
    )(*args)


def _dwconv_kernel(*refs, taps, glu, post_ln, chunk, sub):
    if glu:
        ap, ac, an, bp, bc, bn, w_ref, b_ref, lg_ref, lb_ref, o_ref, scr = refs
        load = lambda a, g: a[0] * jax.nn.sigmoid(g[0])
        prev, cur, nxt = load(ap, bp), load(ac, bc), load(an, bn)
    else:
        ap, ac, an, w_ref, b_ref, o_ref, scr = refs
        prev, cur, nxt = ap[0], ac[0], an[0]
    i = pl.program_id(2)
    last = pl.num_programs(2) - 1
    pad = CONV_PAD
    scr[0, 0:pad, :] = jnp.where(i > 0, prev, 0.0)
    scr[0, pad:pad + chunk, :] = cur
    scr[0, pad + chunk:2 * pad + chunk, :] = jnp.where(i < last, nxt, 0.0)
    lo = (taps - 1) // 2
    span = chunk + 2 * pad - SUBLANES
    for r in sorted({(pad - lo + j) % SUBLANES for j in range(taps)} - {0}):
        scr[r, 0:span, :] = scr[0, pl.ds(r, span), :]
    for c in range(chunk // sub):
        acc = None
        for j in range(taps):
            o = c * sub + pad - lo + j
            term = w_ref[j:j + 1, :] * scr[o % SUBLANES, pl.ds(o - o % SUBLANES, sub), :]
            acc = term if acc is None else acc + term
        y = acc + b_ref[...]
        if post_ln:
            mu = jnp.mean(y, axis=-1, keepdims=True)
            yc = y - mu
            var = jnp.mean(yc * yc, axis=-1, keepdims=True)
            y = yc * lax.rsqrt(var + EPS) * lg_ref[...] + lb_ref[...]
            y = y * jax.nn.sigmoid(y)
        o_ref[0, c * sub:(c + 1) * sub, :] = y.astype(o_ref.dtype)


def _dwconv(x, w, bias, ncol, *, glu=False, ln=None, out_dtype=F32, chunk=256):
    b, l, _ = x.shape
    taps = w.shape[0]
    cw = 512
    chunk = _t(l, chunk)
    nblk = l // chunk
    per = chunk // CONV_PAD
    nsmall = l // CONV_PAD

    def views(coff):
        return [pl.BlockSpec((1, CONV_PAD, cw), lambda bi, c, i: (bi, jnp.maximum(i * per - 1, 0), c + coff)),
                pl.BlockSpec((1, chunk, cw), lambda bi, c, i: (bi, i, c + coff)),
                pl.BlockSpec((1, CONV_PAD, cw), lambda bi, c, i: (bi, jnp.minimum((i + 1) * per, nsmall - 1), c + coff))]

    in_specs = views(0)
    args = [x, x, x]
    if glu:
        in_specs += views(ncol)
        args += [x, x, x]
    vec = lambda: pl.BlockSpec((1, cw), lambda bi, c, i: (0, c))
    in_specs += [pl.BlockSpec((taps, cw), lambda bi, c, i: (0, c)), vec()]
    args += [w, bias.reshape(1, -1)]
    if ln is not None:
        in_specs += [vec(), vec()]
        args += [ln[0].reshape(1, -1), ln[1].reshape(1, -1)]
    return pl.pallas_call(
        functools.partial(_dwconv_kernel, taps=taps, glu=glu, post_ln=ln is not None, chunk=chunk, sub=32),
        grid=(b, ncol, nblk), in_specs=in_specs,
        out_specs=pl.BlockSpec((1, chunk, cw), lambda bi, c, i: (bi, i, c)),
        out_shape=jax.ShapeDtypeStruct((b, l, ncol * cw), out_dtype),
        scratch_shapes=[pltpu.VMEM((SUBLANES, chunk + 2 * CONV_PAD, cw), F32)],
        compiler_params=_cp(("arbitrary", "arbitrary", "arbitrary")), name="dwconv",
    )(*args)


DFT_GROUP = 64


def _dft_kernel(ca_ref, sa_ref, cb_ref, sb_ref, p_ref, pt_ref):
    ca, sa = ca_ref[0], sa_ref[0]
    cb, sb = cb_ref[...], sb_ref[...]
    c = ca * cb - sa * sb
    s = -(sa * cb + ca * sb)
    row = pl.program_id(0) * DFT_GROUP + lax.broadcasted_iota(jnp.int32, c.shape, 0)
    col = lax.broadcasted_iota(jnp.int32, c.shape, 1)
    sign = lambda i: jnp.where(i % 2 == 0, 1.0, -1.0)
    p_ref[0] = c.astype(p_ref.dtype)
    p_ref[1] = jnp.where(row == 0, sign(col), s).astype(p_ref.dtype)
    pt_ref[0] = c.astype(pt_ref.dtype)
    pt_ref[1] = jnp.where(col == 0, sign(row), s).astype(pt_ref.dtype)


def _dft_tables(l):
    n2 = 2 * l
    g = DFT_GROUP
    n = jnp.arange(l, dtype=jnp.int32)
    ang = lambda k: ((k[:, None] * n[None, :]) & (n2 - 1)).astype(F32) * (2.0 * math.pi / n2)
    ang_a = ang(g * jnp.arange(l // g, dtype=jnp.int32)).reshape(l // g, 1, l)
    ang_b = ang(jnp.arange(g, dtype=jnp.int32))
    coarse = pl.BlockSpec((1, 1, l), lambda a: (a, 0, 0))
    fine = pl.BlockSpec((g, l), lambda a: (0, 0))
    out = pl.BlockSpec((2, g, l), lambda a: (0, a, 0))
    return pl.pallas_call(
        _dft_kernel, grid=(l // g,), in_specs=[coarse, coarse, fine, fine], out_specs=[out, out],
        out_shape=[jax.ShapeDtypeStruct((2, l, l), BF16)] * 2,
        compiler_params=_cp(("arbitrary",)), name="dft_tables",
    )(jnp.cos(ang_a), jnp.sin(ang_a), jnp.cos(ang_b), jnp.sin(ang_b))


def _hy_feats(l):
    pos = jnp.arange(l, dtype=F32)
    t01 = pos / (l - 1)
    bands = jnp.linspace(1e-4, HY_BANDS - 1, HY_BANDS, dtype=F32)
    ang = (2.0 * math.pi / l) * pos[:, None] * bands[None, :]
    feats = jnp.concatenate([t01[:, None], jnp.cos(ang), -jnp.sin(ang)], axis=-1)
    return jnp.pad(feats, ((0, 0), (0, LANES - feats.shape[1])))


def _hyfilt_kernel(f_ref, w1_ref, b1_ref, fr_ref, w2_ref, b2_ref, w3_ref, lr_ref, o_ref, *, tl):
    feats = f_ref[...]
    fr = fr_ref[...]
    hdot = lambda a, b: jnp.dot(a, b, preferred_element_type=F32, precision=HI)
    h = jnp.sin(fr * (hdot(feats, w1_ref[...]) + b1_ref[...]))
    h = jnp.sin(fr * (hdot(h, w2_ref[...]) + b2_ref[...]))
    h = hdot(h, w3_ref[...])
    h = h * jnp.exp(-feats[:, 0:1] * jnp.exp(lr_ref[...]))
    row = pl.program_id(0) * tl + lax.broadcasted_iota(jnp.int32, h.shape, 0)
    col = lax.broadcasted_iota(jnp.int32, h.shape, 1)
    is_bwd = ((col // HY_WIDTH) % 2) == 1
    o_ref[...] = jnp.where((row == 0) & is_bwd, 0.0, h).astype(o_ref.dtype)


def _hy_filters(l, lw):
    feats = _hy_feats(l)
    hid = HY_FILTER_HIDDEN
    padc = lambda a: jnp.pad(a, ((0, 0), (0, LANES - a.shape[1])))
    w1 = jnp.pad(lw['hf_w1'], ((0, LANES - lw['hf_w1'].shape[0]), (0, LANES - hid)))
    w2 = jnp.pad(lw['hf_w2'], ((0, LANES - hid), (0, LANES - hid)))
    w3 = jnp.pad(lw['hf_w3'], ((0, LANES - hid), (0, 0)))
    b1, b2, fr = padc(lw['hf_b1'].reshape(1, -1)), padc(lw['hf_b2'].reshape(1, -1)), padc(lw['hf_freq'].reshape(1, -1))
    lr = lw['hf_log_rate'].reshape(1, -1)
    nc = w3.shape[1]
    tl = _t(l, 256)
    full = lambda a: pl.BlockSpec(a.shape, lambda i: (0, 0))
    return pl.pallas_call(
        functools.partial(_hyfilt_kernel, tl=tl), grid=(l // tl,),
        in_specs=[pl.BlockSpec((tl, LANES), lambda i: (i, 0)), full(w1), full(b1), full(fr), full(w2), full(b2),
                  full(w3), full(lr)],
        out_specs=pl.BlockSpec((tl, nc), lambda i: (i, 0)),
        out_shape=jax.ShapeDtypeStruct((l, nc), BF16),
        compiler_params=_cp(("arbitrary",)), name="hy_filter",
    )(feats, w1, b1, fr, w2, b2, w3, lr)


def _hyspec_kernel(p_ref, h_ref, o_ref, *, tk, scale):
    hm = h_ref[...]
    ar = _dot(p_ref[0], hm)
    ai = _dot(p_ref[1], hm)
    c = HY_WIDTH
    row0 = (pl.program_id(0) * tk + lax.broadcasted_iota(jnp.int32, (tk, c), 0)) == 0
    s = jnp.where(row0, 0.5 * scale, scale)
    for n in range(HY_ORDER):
        o = 2 * n * c
        o_ref[n, 0] = (ar[:, o:o + c] + ar[:, o + c:o + 2 * c]) * s
        fi, bi = ai[:, o:o + c], ai[:, o + c:o + 2 * c]
        o_ref[n, 1] = jnp.where(row0, fi + bi, fi - bi) * s


def _hy_spectrum(p, filt):
    l = p.shape[1]
    tk = _t(l, 256)
    return pl.pallas_call(
        functools.partial(_hyspec_kernel, tk=tk, scale=1.0 / l), grid=(l // tk,),
        in_specs=[pl.BlockSpec((2, tk, l), lambda i: (0, i, 0)), pl.BlockSpec(filt.shape, lambda i: (0, 0))],
        out_specs=pl.BlockSpec((HY_ORDER, 2, tk, HY_WIDTH), lambda i: (0, 0, i, 0)),
        out_shape=jax.ShapeDtypeStruct((HY_ORDER, 2, l, HY_WIDTH), F32),
        compiler_params=_cp(("arbitrary",)), name="hy_spectrum",
    )(p, filt)


def _hyfwd_kernel(p_ref, z_ref, k_ref, y_ref, *, tk):
    z = z_ref[0].astype(BF16)
    xr = _dot(p_ref[0], z)
    xi = _dot(p_ref[1], z)
    kr, ki = k_ref[0, 0], k_ref[0, 1]
    row0 = (pl.program_id(0) * tk + lax.broadcasted_iota(jnp.int32, xr.shape, 0)) == 0
    xiki = xi * ki
    y_ref[0, 0] = (xr * kr - jnp.where(row0, 0.0, xiki)).astype(y_ref.dtype)
    y_ref[0, 1] = jnp.where(row0, xiki, xr * ki + xi * kr).astype(y_ref.dtype)


def _hy_forward(p, z, zcol, kf, order):
    b, l, _ = z.shape
    c = HY_WIDTH
    tk = _t(l, 1024)
    return pl.pallas_call(
        functools.partial(_hyfwd_kernel, tk=tk), grid=(l // tk, b),
        in_specs=[pl.BlockSpec((2, tk, l), lambda i, bi: (0, i, 0)),
                  pl.BlockSpec((1, l, c), lambda i, bi: (bi, 0, zcol)),
                  pl.BlockSpec((1, 2, tk, c), lambda i, bi: (order, 0, i, 0))],
        out_specs=pl.BlockSpec((1, 2, tk, c), lambda i, bi: (bi, 0, i, 0)),
        out_shape=jax.ShapeDtypeStruct((b, 2, l, c), BF16),
        compiler_params=_cp(("arbitrary", "arbitrary")), name="hy_forward",
    )(p, z, kf)


def _hyinv_kernel(pt_ref, y_ref, g_ref, z_ref, bias_ref, o_ref):
    conv = _dot(pt_ref[0], y_ref[0, 0]) + _dot(pt_ref[1], y_ref[0, 1])
    o_ref[0] = (g_ref[0] * (conv + bias_ref[...] * z_ref[0])).astype(o_ref.dtype)


def _hy_inverse(pt, y, u, gcol, z, zcol, bias, out_dtype):
    b, _, l, c = y.shape
    tn = _t(l, 1024)
    return pl.pallas_call(
        _hyinv_kernel, grid=(l // tn, b),
        in_specs=[pl.BlockSpec((2, tn, l), lambda i, bi: (0, i, 0)),
                  pl.BlockSpec((1, 2, l, c), lambda i, bi: (bi, 0, 0, 0)),
                  pl.BlockSpec((1, tn, c), lambda i, bi: (bi, i, gcol)),
                  pl.BlockSpec((1, tn, c), lambda i, bi: (bi, i, zcol)),
                  pl.BlockSpec((1, c), lambda i, bi: (0, 0))],
        out_specs=pl.BlockSpec((1, tn, c), lambda i, bi: (bi, i, 0)),
        out_shape=jax.ShapeDtypeStruct((b, l, c), out_dtype),
        compiler_params=_cp(("arbitrary", "arbitrary")), name="hy_inverse",
    )(pt, y, u, z, bias.reshape(1, c))


def _hyena(phy, lw, dft):
    p, pt, kf = dft
    u = _dwconv(phy, lw['hy_conv_w'], lw['hy_conv_b'], HY_ORDER + 1, chunk=1024)
    y = _hy_forward(p, u, 0, kf, 0)
    z = _hy_inverse(pt, y, u, 1, u, 0, lw['hy_bias'][0], F32)
    y = _hy_forward(p, z, 0, kf, 1)
    return _hy_inverse(pt, y, u, 2, z, 0, lw['hy_bias'][1], BF16)


def _merge_kernel(hx_ref, g0, g1, g2, g3, b0, b1, b2, b3, wbr_ref, o_ref, gb_ref, wb_ref):
    @pl.when(pl.program_id(1) == 0)
    def _():
        for n, wg in enumerate((g0, g1, g2, g3)):
            gb_ref[n] = wg[0].astype(gb_ref.dtype)
        wb_ref[...] = wbr_ref[0].astype(wb_ref.dtype)

    hx = hx_ref[...]
    acc = None
    for n, br in enumerate((b0, b1, b2, b3)):
        gate = jax.nn.sigmoid(_dot_nt(hx, gb_ref[n]))
        term = gate * _dot(br[...], wb_ref[n])
        acc = term if acc is None else acc + term
    o_ref[...] = acc.astype(o_ref.dtype)


def _merge(hx, w_gate, layer, row0, branches, w_br):
    m, d = hx.shape
    tm, tn = _t(m, 512), _t(d, 512)
    nj = d // tn
    bw = BRANCH_WIDTH
    in_specs = [pl.BlockSpec((tm, d), lambda j, i: (i, 0))]
    in_specs += [pl.BlockSpec((pl.Element(1), pl.Element(tn), pl.Element(d)),
                              functools.partial(lambda j, i, n: (layer, pl.multiple_of(row0 + n * d + j * tn, SUBLANES), 0), n=n),
                              pipeline_mode=pl.Buffered(1))
                 for n in range(N_BRANCH)]
    in_specs += [pl.BlockSpec((tm, bw), lambda j, i: (i, 0))] * N_BRANCH
    in_specs += [pl.BlockSpec((1, N_BRANCH, bw, tn), lambda j, i: (layer, 0, 0, j))]
    return pl.pallas_call(
        _merge_kernel, grid=(nj, m // tm), in_specs=in_specs,
        out_specs=pl.BlockSpec((tm, tn), lambda j, i: (i, j)),
        out_shape=jax.ShapeDtypeStruct((m, d), BF16),
        scratch_shapes=[pltpu.VMEM((N_BRANCH, tn, d), BF16), pltpu.VMEM((N_BRANCH, bw, tn), BF16)],
        compiler_params=_cp(("arbitrary", "arbitrary")), name="merge",
    )(hx, w_gate, w_gate, w_gate, w_gate, *branches, w_br)


def _topk_kernel(aff_ref, tri_ref, sp_ref, spt_ref, g_ref, *, cap):
    afft = aff_ref[0].T[:N_EXPERTS, :]
    bits = lax.bitcast_convert_type(afft, jnp.int32)

    def body(i, prefix):
        cand = prefix | jnp.left_shift(jnp.int32(1), 30 - i)
        cnt = jnp.sum((bits >= cand).astype(F32), axis=1, keepdims=True)
        return jnp.where(cnt >= cap, cand, prefix)

    thr = lax.fori_loop(0, 31, body, jnp.zeros((N_EXPERTS, 1), jnp.int32))
    gt = bits > thr
    eq = bits == thr
    need = cap - jnp.sum(gt.astype(F32), axis=1, keepdims=True)
    tri = tri_ref[...]
    rank_eq = _dot(eq.astype(BF16), tri)
    sel = gt | (eq & (rank_eq <= need))
    pos = _dot(sel.astype(BF16), tri) - 1.0
    sp = jnp.where(sel, pos, -1.0)
    sp_ref[0] = sp.astype(jnp.int32)
    n = sp.shape[1]
    padded = jnp.concatenate([sp, jnp.full((LANES - N_EXPERTS, n), -1.0, F32)], axis=0)
    spt_ref[0] = padded.T
    icap = g_ref.shape[2]
    slot = lax.broadcasted_iota(jnp.int32, (icap, n), 0).astype(F32)
    for e in range(N_EXPERTS):
        hit = slot == sp[e:e + 1, :]
        g_ref[0, e] = jnp.sum(jnp.where(hit, afft[e:e + 1, :], 0.0), axis=1, keepdims=True)


def _topk(aff, cap):
    b, n, _ = aff.shape
    t = jnp.arange(n, dtype=jnp.int32)
    tri = (t[:, None] <= t[None, :]).astype(BF16)
    return pl.pallas_call(
        functools.partial(_topk_kernel, cap=float(cap)), grid=(b,),
        in_specs=[pl.BlockSpec((1, n, LANES), lambda i: (i, 0, 0)), pl.BlockSpec((n, n), lambda i: (0, 0))],
        out_specs=[pl.BlockSpec((1, N_EXPERTS, n), lambda i: (i, 0, 0)), pl.BlockSpec((1, n, LANES), lambda i: (i, 0, 0)),
                   pl.BlockSpec((1, N_EXPERTS, cap, 1), lambda i: (i, 0, 0, 0))],
        out_shape=[jax.ShapeDtypeStruct((b, N_EXPERTS, n), jnp.int32), jax.ShapeDtypeStruct((b, n, LANES), F32),
                   jax.ShapeDtypeStruct((b, N_EXPERTS, cap, 1), F32)],
        compiler_params=_cp(("arbitrary",)), name="topk",
    )(aff, tri)


def _gather_kernel(sp_ref, h_ref, o_ref, *, cap):
    eg, n = sp_ref.shape[1], sp_ref.shape[3]
    slot = lax.broadcasted_iota(jnp.int32, (cap, n), 0)
    onehot = jnp.concatenate([(slot == sp_ref[0, j]).astype(BF16) for j in range(eg)], axis=0)
    rows = _dot(onehot, h_ref[0]).astype(o_ref.dtype)
    for j in range(eg):
        o_ref[j] = rows[j * cap:(j + 1) * cap, :]


def _gather(sp, h, cap):
    b, n, d = h.shape
    eg = N_EXPERTS if N_EXPERTS * cap <= 512 and cap % BF16_ROWS == 0 else 1
    return pl.pallas_call(
        functools.partial(_gather_kernel, cap=cap), grid=(b, N_EXPERTS // eg),
        in_specs=[pl.BlockSpec((1, eg, 1, n), lambda bi, e: (bi, e, 0, 0)), pl.BlockSpec((1, n, d), lambda bi, e: (bi, 0, 0))],
        out_specs=pl.BlockSpec((eg, cap, d), lambda bi, e: (e, bi, 0)),
        out_shape=jax.ShapeDtypeStruct((N_EXPERTS, b * cap, d), BF16),
        compiler_params=_cp(("arbitrary", "arbitrary")), name="moe_gather",
    )(sp.reshape(b, N_EXPERTS, 1, n), h)


def _ffn_kernel(*refs, n_x):
    xs = refs[:n_x]
    gs = refs[n_x:2 * n_x]
    w1_ref, w3_ref, w2_ref = refs[2 * n_x:2 * n_x + 3]
    ys = refs[2 * n_x + 3:3 * n_x + 3]
    accs = refs[3 * n_x + 3:]
    f = pl.program_id(1)
    last = pl.num_programs(1) - 1
    w1, w3, w2 = w1_ref[0, 0].astype(BF16), w3_ref[0, 0].astype(BF16), w2_ref[0, 0].astype(BF16)

    @pl.when((pl.program_id(0) == 0) & (f == 0))
    def _():
        for acc in accs:
            acc[...] = jnp.zeros(acc.shape, acc.dtype)

    for x_ref, g_ref, y_ref, acc in zip(xs, gs, ys, accs):
        rows = x_ref.shape[1]
        grp = 512 if rows % 512 == 0 else rows
        for r in range(rows // grp):
            x = x_ref[0, r * grp:(r + 1) * grp, :]
            a = _dot(x, w1)
            hid = (a * jax.nn.sigmoid(a) * _dot(x, w3)).astype(BF16)
            part = _dot(hid, w2)
            acc[r * grp:(r + 1) * grp, :] = jnp.where(f == 0, part, acc[r * grp:(r + 1) * grp, :] + part)

        @pl.when(f == last)
        def _():
            nb, _, cap, _ = y_ref.shape
            for bi in range(nb):
                y_ref[bi, 0] = (acc[bi * cap:(bi + 1) * cap, :] * g_ref[bi, 0]).astype(y_ref.dtype)


def _expert_ffn(xs, gs, w1, w3, w2, layer):
    _, e, d, ff = w1.shape
    tf = _t(ff, 256)
    in_specs = [pl.BlockSpec((1, x.shape[1], d), lambda ei, f: (ei, 0, 0)) for x in xs]
    in_specs += [pl.BlockSpec((g.shape[0], 1, g.shape[2], 1), lambda ei, f: (0, ei, 0, 0)) for g in gs]
    in_specs += [pl.BlockSpec((1, 1, d, tf), lambda ei, f: (layer, ei, 0, f))] * 2
    in_specs += [pl.BlockSpec((1, 1, tf, d), lambda ei, f: (layer, ei, f, 0))]
    return pl.pallas_call(
        functools.partial(_ffn_kernel, n_x=len(xs)), grid=(e, ff // tf), in_specs=in_specs,
        out_specs=[pl.BlockSpec((g.shape[0], 1, g.shape[2], d), lambda ei, f: (0, ei, 0, 0)) for g in gs],
        out_shape=[jax.ShapeDtypeStruct((g.shape[0], e, g.shape[2], d), BF16) for g in gs],
        scratch_shapes=[pltpu.VMEM((x.shape[1], d), F32) for x in xs],
        compiler_params=_cp(("arbitrary", "arbitrary")), name="expert_ffn",
    )(*xs, *gs, w1, w3, w2)


def _scatter_kernel(spt_ref, y_ref, x_ref, g_ref, ng_ref, nsh_ref, nsc_ref, o_ref, h_ref, acc_ref, *, cap, kc):
    k = pl.program_id(2)
    tt = spt_ref.shape[1]
    blk = y_ref.shape[1]
    spt = spt_ref[0].astype(BF16)
    term = None
    for j in range(blk // kc):
        first = (k * blk + j * kc) // cap
        col = lax.broadcasted_iota(jnp.int32, (LANES, kc), 1)
        row = lax.broadcasted_iota(jnp.int32, (LANES, kc), 0)
        rep = (row == first + col // cap).astype(BF16)
        mine = _dot(spt, rep)
        slot = (lax.broadcasted_iota(jnp.int32, (tt, kc), 1) % cap).astype(F32)
        part = _dot((mine == slot).astype(BF16), y_ref[0, j * kc:(j + 1) * kc, :])
        term = part if term is None else term + part

    @pl.when((pl.program_id(0) == 0) & (pl.program_id(1) == 0) & (k == 0))
    def _():
        acc_ref[...] = jnp.zeros(acc_ref.shape, acc_ref.dtype)

    acc_ref[...] = jnp.where(k == 0, term, acc_ref[...] + term)

    @pl.when(k == pl.num_programs(2) - 1)
    def _():
        xn = x_ref[0] + g_ref[0] * acc_ref[...]
        o_ref[0] = xn
        yn = xn * lax.rsqrt(jnp.mean(xn * xn, axis=-1, keepdims=True) + EPS)
        h_ref[0] = ((yn * ng_ref[...]) * (1.0 + nsc_ref[0]) + nsh_ref[0]).astype(h_ref.dtype)


def _scatter(spt, y, x, g, cap, gain, shift, scale, h_dtype):
    b, n, d = x.shape
    tt = _t(n, 512)
    kc = _t(N_EXPERTS * cap, 1024)
    blk = _t(N_EXPERTS * cap, 2 * kc)
    assert kc % cap == 0 and cap <= 256
    return pl.pallas_call(
        functools.partial(_scatter_kernel, cap=cap, kc=kc), grid=(b, n // tt, N_EXPERTS * cap // blk),
        in_specs=[pl.BlockSpec((1, tt, LANES), lambda bi, i, k: (bi, i, 0)),
                  pl.BlockSpec((1, blk, d), lambda bi, i, k: (bi, k, 0)),
                  pl.BlockSpec((1, tt, d), lambda bi, i, k: (bi, i, 0)),
                  pl.BlockSpec((1, 1, d), lambda bi, i, k: (bi, 0, 0)),
                  pl.BlockSpec((1, d), lambda bi, i, k: (0, 0)),
                  pl.BlockSpec((1, 1, d), lambda bi, i, k: (bi, 0, 0)),
                  pl.BlockSpec((1, 1, d), lambda bi, i, k: (bi, 0, 0))],
        out_specs=[pl.BlockSpec((1, tt, d), lambda bi, i, k: (bi, i, 0))] * 2,
        out_shape=[jax.ShapeDtypeStruct((b, n, d), F32), jax.ShapeDtypeStruct((b, n, d), h_dtype)],
        scratch_shapes=[pltpu.VMEM((tt, d), F32)],
        compiler_params=_cp(("arbitrary", "arbitrary", "arbitrary")), name="moe_scatter",
    )(spt, y, x, g, gain.reshape(1, d), shift, scale)


def _rope_tables(s):
    rows = jnp.repeat(jnp.arange(s // GRID_W, dtype=jnp.int32), GRID_W).astype(F32)
    cols = jnp.tile(jnp.arange(GRID_W, dtype=jnp.int32), s // GRID_W).astype(F32)

    def half_tables(m):
        inv = ROPE_BASE ** (-jnp.arange(0, m, 2, dtype=F32) / m)
        out = []
        for pos in (rows, cols):
            ang = pos[:, None] * inv[None, :]
            c, sn = jnp.cos(ang), jnp.sin(ang)
            out.append((jnp.concatenate([c, c], axis=1), jnp.concatenate([-sn, sn], axis=1)))
        return (jnp.concatenate([out[0][0], out[1][0]], axis=1), jnp.concatenate([out[0][1], out[1][1]], axis=1))

    cg, sg = half_tables(GQA_HEAD_DIM // 2)
    cm, sm = half_tables(MLA_ROPE // 2)
    padm = lambda a, fill: jnp.concatenate(
        [jnp.full((s, MLA_NOPE), fill, F32), a, jnp.full((s, LANES - MLA_NOPE - MLA_ROPE), fill, F32)], axis=1)
    return jnp.stack([cg, sg]), jnp.stack([padm(cm, 1.0), padm(sm, 0.0)])


W_IN_Q = 2 * GQA_KV_HEADS * GQA_HEAD_DIM + MLA_KV_LORA + MLA_ROPE
W_IN_HY = W_IN_Q + GQA_HEADS * GQA_HEAD_DIM + MLA_Q_LORA
W_IN_CV = W_IN_HY + (HY_ORDER + 1) * HY_WIDTH
W_IN_GATE = W_IN_CV + 2 * CV_WIDTH


def _layer_weights(i, w):
    assert W_IN_Q + LANES - MLA_ROPE == ATT_KV_COLS and W_IN_HY - W_IN_Q == ATT_Q_COLS
    qb = w['mla_q_b'][i].reshape(MLA_Q_LORA, MLA_HEADS, MLA_NOPE + MLA_ROPE)
    qb = jnp.pad(qb, ((0, 0), (0, 0), (0, LANES - MLA_NOPE - MLA_ROPE))).reshape(MLA_Q_LORA, MLA_HEADS * LANES)
    kvb = w['mla_kv_b'][i].reshape(MLA_KV_LORA, MLA_HEADS, MLA_NOPE + MLA_V)
    kvb_k = jnp.pad(kvb[:, :, :MLA_NOPE], ((0, 0), (0, 0), (0, LANES - MLA_NOPE))).reshape(MLA_KV_LORA, MLA_HEADS * LANES)
    kvb_v = kvb[:, :, MLA_NOPE:].reshape(MLA_KV_LORA, MLA_HEADS * MLA_V)
    lw = {k: w[k][i] for k in ('gqa_q_gain', 'gqa_k_gain', 'mla_q_a_gain', 'mla_kv_a_gain', 'hy_conv_w', 'hy_conv_b',
                               'hf_w1', 'hf_b1', 'hf_w2', 'hf_b2', 'hf_w3', 'hf_freq', 'hf_log_rate', 'hy_bias',
                               'cv_w', 'cv_b', 'cv_ln_g', 'cv_ln_b', 'w_router', 'g_mix', 'g_ffn')}
    lw.update(layer=i, w_in_t=w['w_in_t'], q_b=qb, kv_b_k=kvb_k, kv_b_v=kvb_v, w_br=w['w_br'])
    return lw


def _mixer_branches(hx, keys_extra, tabs, lw, dft):
    b, l, d = hx.shape
    hx2 = hx.reshape(b * l, d)
    proj = lambda row0, n, tn: _matmul_nt(hx2, lw['w_in_t'], lw['layer'], row0, n, F32, 1024, tn).reshape(b, l, n)
    kg, vg, km, vm, qg, qm = _attn_prep(hx, tabs, lw)
    own = (kg, vg, km, vm)
    if dft is None:
        return None, own
    ek = keys_extra
    o_gqa = _attention(qg, [(kg, vg)] + ([(ek[0], ek[1])] if ek else []), GQA_HEADS, GQA_KV_HEADS, GQA_HEAD_DIM)
    o_mla = _attention(qm, [(km, vm)] + ([(ek[2], ek[3])] if ek else []), MLA_HEADS, MLA_HEADS, MLA_NOPE + MLA_ROPE)
    phy = proj(W_IN_HY, W_IN_CV - W_IN_HY, 768)
    o_hy = _hyena(phy, lw, dft)
    pcv = proj(W_IN_CV, W_IN_GATE - W_IN_CV, 1024)
    o_cv = _dwconv(pcv, lw['cv_w'], lw['cv_b'], 1, glu=True, ln=(lw['cv_ln_g'], lw['cv_ln_b']), out_dtype=BF16,
                   chunk=512)
    flat = lambda a: a.reshape(b * l, -1)
    return [flat(o_hy), flat(o_gqa), flat(o_mla), flat(o_cv)], own


def kernel(x, c, ctx, c_ctx, w_mod, b_mod, g_mix, w_in, gqa_q_gain, gqa_k_gain, mla_q_a_gain, mla_q_b, mla_kv_a_gain, mla_kv_b, hy_conv_w, hy_conv_b, hf_w1, hf_b1, hf_w2, hf_b2, hf_w3, hf_freq, hf_log_rate, hy_bias, cv_w, cv_b, cv_ln_g, cv_ln_b, w_br, w_out, g_ffn, w_router, w1, w3, w2, g_final):
    w = dict(g_mix=g_mix, w_in_t=jnp.swapaxes(w_in, 1, 2), gqa_q_gain=gqa_q_gain, gqa_k_gain=gqa_k_gain, mla_q_a_gain=mla_q_a_gain,
             mla_q_b=mla_q_b, mla_kv_a_gain=mla_kv_a_gain, mla_kv_b=mla_kv_b, hy_conv_w=hy_conv_w, hy_conv_b=hy_conv_b,
             hf_w1=hf_w1, hf_b1=hf_b1, hf_w2=hf_w2, hf_b2=hf_b2, hf_w3=hf_w3, hf_freq=hf_freq, hf_log_rate=hf_log_rate,
             hy_bias=hy_bias, cv_w=cv_w, cv_b=cv_b, cv_ln_g=cv_ln_g, cv_ln_b=cv_ln_b, w_br=w_br,
             g_ffn=g_ffn, w_router=w_router)
    bsz, s, d = x.shape
    lc = ctx.shape[1]
    depth = w_mod.shape[0]
    assert bsz < 8
    cvec = jnp.concatenate([c, c_ctx[None, :], jnp.zeros((8 - bsz - 1, d), F32)], axis=0)
    mod = _mod_all(cvec, w_mod, b_mod)
    tabs = _rope_tables(s)
    p_x, pt_x = _dft_tables(s)
    p_c, pt_c = _dft_tables(lc)
    cap_x = CAPACITY_FACTOR * s // N_EXPERTS
    cap_c = CAPACITY_FACTOR * lc // N_EXPERTS
    xc = ctx
    mods_x = [[mod[i, :bsz, j * d:(j + 1) * d].reshape(bsz, 1, d) for j in range(6)] for i in range(depth)]
    mods_c = [[jnp.broadcast_to(mod[i, bsz:bsz + 1, j * d:(j + 1) * d].reshape(1, 1, d), (bsz, 1, d)) for j in range(6)]
              for i in range(depth)]
    zero = jnp.zeros((bsz, 1, d), F32)
    w_out_b = w_out.astype(BF16)
    hx = _norm_mod(x, g_mix[0], mods_x[0][0], mods_x[0][1])
    hc = _norm_mod(xc, g_mix[0], mods_c[0][0], mods_c[0][1])
    for i in range(depth):
        last = i == depth - 1
        lw = _layer_weights(i, w)
        mx, mc = mods_x[i], mods_c[i]
        nxt_x = (g_final, zero, zero, F32) if last else (g_mix[i + 1], mods_x[i + 1][0], mods_x[i + 1][1], BF16)

        dft_c = None if last else (p_c, pt_c, _hy_spectrum(p_c, _hy_filters(lc, lw)))
        br_c, keys_c = _mixer_branches(hc, None, None, lw, dft_c)
        dft_x = (p_x, pt_x, _hy_spectrum(p_x, _hy_filters(s, lw)))
        br_x, _ = _mixer_branches(hx, keys_c, tabs, lw, dft_x)

        merged = _merge(hx.reshape(bsz * s, d), lw['w_in_t'], i, W_IN_GATE, br_x, lw['w_br']).reshape(bsz, s, d)
        x, hf, aff = _out_proj_router(merged, w_out_b, i, x, mx[2], lw['g_ffn'], mx[3], mx[4], lw['w_router'])
        sp, spt, gslot = _topk(aff, cap_x)
        xs, gs = [_gather(sp, hf, cap_x)], [gslot]
        if not last:
            merged_c = _merge(hc.reshape(bsz * lc, d), lw['w_in_t'], i, W_IN_GATE, br_c, lw['w_br']).reshape(bsz, lc, d)
            xc, hfc, aff_c = _out_proj_router(merged_c, w_out_b, i, xc, mc[2], lw['g_ffn'], mc[3], mc[4], lw['w_router'])
            sp_c, spt_c, gslot_c = _topk(aff_c, cap_c)
            xs.append(_gather(sp_c, hfc, cap_c))
            gs.append(gslot_c)
        ys = _expert_ffn(xs, gs, w1, w3, w2, i)
        x, hx = _scatter(spt, ys[0].reshape(bsz, N_EXPERTS * cap_x, d), x, mx[5], cap_x, *nxt_x)
        if not last:
            xc, hc = _scatter(spt_c, ys[1].reshape(bsz, N_EXPERTS * cap_c, d), xc, mc[5], cap_c,
                              g_mix[i + 1], mods_c[i + 1][0], mods_c[i + 1][1], BF16)
    return hx
```
